```python
import math
import jax, jax.numpy as jnp
from jax import lax
import numpy as np

D_MODEL = 1024
BATCH = 8
SEQ = 8192
DEPTH = 1
DEC_BATCH = 2
DEC_SEQ = 8192
PAST_LEN = 128

HEAD_DIM = 64
N_HEADS_A = 4
WIDTH_A = N_HEADS_A * 2 * HEAD_DIM
N_HEADS_B = 8
N_KV_B = 2
GROUP_B = N_HEADS_B // N_KV_B
WIDTH_B = N_HEADS_B * HEAD_DIM
MIX_WIDTH = WIDTH_A + WIDTH_B
KV_B = N_KV_B * HEAD_DIM
IN_SIZES = (WIDTH_A, WIDTH_A, WIDTH_A, WIDTH_B, KV_B, KV_B)
IN_COLS = sum(IN_SIZES)
IN_SPLITS = tuple(int(v) for v in np.cumsum(IN_SIZES)[:-1])
GRID_W = 64
ROPE_THETA = 10000.0
Q_BLOCK = 128
N_EXPERTS = 16
EXPERT_FF = 2048
CAPACITY_FACTOR = 2
ALPHA = (2.0 * DEPTH) ** 0.25
BETA = (8.0 * DEPTH) ** -0.25
LN_EPS = 1e-5
RMS_EPS = 1e-6

kernel_name = "hybrid_diffattn_gqa_axial_ec_moe_encoder"


def layer_norm(x, g=None, b=None):
    xf = x.astype(jnp.float32)
    mu = jnp.mean(xf, -1, keepdims=True)
    var = jnp.mean(jnp.square(xf - mu), -1, keepdims=True)
    y = (xf - mu) * lax.rsqrt(var + LN_EPS)
    if g is not None:
        y = y * g.astype(jnp.float32) + b.astype(jnp.float32)
    return y.astype(x.dtype)


def rms_norm(x, g):
    xf = x.astype(jnp.float32)
    y = xf * lax.rsqrt(jnp.mean(jnp.square(xf), -1, keepdims=True) + RMS_EPS)
    return (y * g.astype(jnp.float32)).astype(x.dtype)


def alibi_slopes(n):
    return jnp.asarray([2.0 ** (-8.0 * (h + 1) / n) for h in range(n)], jnp.float32)


def lambda_init_fn(layer):
    return 0.8 - 0.6 * math.exp(-0.3 * layer)


def axial_rope_tables(n_tok):
    rows = n_tok // GRID_W
    row = jnp.repeat(jnp.arange(rows), GRID_W).astype(jnp.float32)
    col = jnp.tile(jnp.arange(GRID_W), rows).astype(jnp.float32)
    half = HEAD_DIM // 2
    inv_freq = ROPE_THETA ** (-jnp.arange(0, half, 2, dtype=jnp.float32) / half)
    ang_r = row[:, None] * inv_freq[None, :]
    ang_c = col[:, None] * inv_freq[None, :]
    return (jnp.cos(ang_r), jnp.sin(ang_r), jnp.cos(ang_c), jnp.sin(ang_c))


def rotate_section(x, cos, sin):
    n = x.shape[-1] // 2
    x1, x2 = x[..., :n], x[..., n:]
    c = cos[:, None, :]
    s = sin[:, None, :]
    return jnp.concatenate([x1 * c - x2 * s, x2 * c + x1 * s], -1)


def apply_axial_rope(x, tables):
    cr, sr, cc, sc = tables
    half = HEAD_DIM // 2
    xf = x.astype(jnp.float32)
    out = jnp.concatenate([rotate_section(xf[..., :half], cr, sr),
                           rotate_section(xf[..., half:], cc, sc)], -1)
    return out.astype(x.dtype)


def diff_attention(q, k, v, lam, slopes):
    b, s = q.shape[0], q.shape[1]
    nblk = s // Q_BLOCK
    scale = HEAD_DIM ** -0.5
    qb = q.reshape(b, nblk, Q_BLOCK, N_HEADS_A, 2, HEAD_DIM).transpose(1, 0, 2, 3, 4, 5)
    starts = jnp.arange(nblk, dtype=jnp.float32) * Q_BLOCK
    kpos = jnp.arange(s, dtype=jnp.float32)

    def block(args):
        qi, start = args
        sc = jnp.einsum('bqhcd,bkhcd->bhcqk', qi, k,
                        preferred_element_type=jnp.float32) * scale
        qpos = start + jnp.arange(Q_BLOCK, dtype=jnp.float32)
        dist = jnp.abs(qpos[:, None] - kpos[None, :])
        sc = sc - slopes[:, None, None, None] * dist
        p = jax.nn.softmax(sc, axis=-1)
        a = (p[:, :, 0] - lam * p[:, :, 1]).astype(v.dtype)
        return jnp.einsum('bhqk,bkhe->bqhe', a, v)

    out = lax.map(block, (qb, starts))
    return out.transpose(1, 0, 2, 3, 4).reshape(b, s, N_HEADS_A, 2 * HEAD_DIM)


def gqa_attention(q, k, v):
    b, s = q.shape[0], q.shape[1]
    nblk = s // Q_BLOCK
    scale = HEAD_DIM ** -0.5
    qb = q.reshape(b, nblk, Q_BLOCK, N_KV_B, GROUP_B, HEAD_DIM).transpose(1, 0, 2, 3, 4, 5)

    def block(qi):
        sc = jnp.einsum('bqngd,bknd->bngqk', qi, k,
                        preferred_element_type=jnp.float32) * scale
        p = jax.nn.softmax(sc, axis=-1).astype(v.dtype)
        return jnp.einsum('bngqk,bknd->bqngd', p, v)

    out = lax.map(block, qb)
    return out.transpose(1, 0, 2, 3, 4, 5).reshape(b, s, WIDTH_B)


def expert_choice_ffn(h, w_router, w_gate, w_up, w_down):
    b, s, d = h.shape
    n = b * s
    cap = CAPACITY_FACTOR * n // N_EXPERTS
    t = h.reshape(n, d)
    aff = jax.nn.softmax(jnp.einsum('nd,de->ne', t, w_router,
                                    preferred_element_type=jnp.float32), axis=-1)
    g, idx = lax.top_k(aff.T, cap)
    xe = t[idx]
    hid = jax.nn.silu(jnp.einsum('ecd,edf->ecf', xe, w_gate)) * jnp.einsum('ecd,edf->ecf', xe, w_up)
    ye = jnp.einsum('ecf,efd->ecd', hid, w_down) * g[..., None].astype(t.dtype)
    out = jnp.zeros_like(t).at[idx.reshape(-1)].add(ye.reshape(-1, d))
    return out.reshape(b, s, d)


def run_trunk(x, c, w_ada, b_ada, w_in, lam_q1, lam_k1, lam_q2, lam_k2, subln_g,
              q_norm_g, k_norm_g, w_out, ln1_g, ln1_b, w_router, w_gate, w_up,
              w_down, ln2_g, ln2_b):
    b, s, _ = x.shape
    tables = axial_rope_tables(s)
    slopes = alibi_slopes(N_HEADS_A)
    for l in range(DEPTH):
        mod = jnp.einsum('bd,de->be', jax.nn.silu(c), w_ada[l]) + b_ada[l]
        sh1, sc1, g1, sh2, sc2, g2 = jnp.split(mod[:, None, :], 6, axis=-1)

        h = layer_norm(x) * (1.0 + sc1) + sh1
        proj = jnp.einsum('bsd,de->bse', h, w_in[l])
        qa, ka, va, qb, kb, vb = jnp.split(proj, IN_SPLITS, axis=-1)

        lam_init = lambda_init_fn(l)
        lam = (jnp.exp(jnp.sum(lam_q1[l].astype(jnp.float32) * lam_k1[l].astype(jnp.float32)))
               - jnp.exp(jnp.sum(lam_q2[l].astype(jnp.float32) * lam_k2[l].astype(jnp.float32)))
               + lam_init)
        qa = qa.reshape(b, s, N_HEADS_A, 2, HEAD_DIM)
        ka = ka.reshape(b, s, N_HEADS_A, 2, HEAD_DIM)
        va = va.reshape(b, s, N_HEADS_A, 2 * HEAD_DIM)
        oa = diff_attention(qa, ka, va, lam, slopes)
        oa = (rms_norm(oa, subln_g[l]) * (1.0 - lam_init)).reshape(b, s, WIDTH_A)

        qb = apply_axial_rope(rms_norm(qb.reshape(b, s, N_HEADS_B, HEAD_DIM), q_norm_g[l]), tables)
        kb = apply_axial_rope(rms_norm(kb.reshape(b, s, N_KV_B, HEAD_DIM), k_norm_g[l]), tables)
        vb = vb.reshape(b, s, N_KV_B, HEAD_DIM)
        ob = gqa_attention(qb.reshape(b, s, N_KV_B, GROUP_B, HEAD_DIM), kb, vb)

        o = jnp.einsum('bse,ed->bsd', jnp.concatenate([oa, ob], -1), w_out[l])
        x = layer_norm(ALPHA * x + g1 * o, ln1_g[l], ln1_b[l])

        h2 = layer_norm(x) * (1.0 + sc2) + sh2
        y = expert_choice_ffn(h2, w_router[l], w_gate[l], w_up[l], w_down[l])
        x = layer_norm(ALPHA * x + g2 * y, ln2_g[l], ln2_b[l])
    return x


def setup_inputs(seed: int = 0) -> dict:
    key = jax.random.key(seed)
    ks = jax.random.split(key, 24)
    f32 = jnp.float32
    D = D_MODEL

    def nrm(k, shape, scale):
        return jax.random.normal(k, shape, f32) * scale

    return {
        "x_prompt": nrm(ks[0], (BATCH, SEQ, D), 1.0),
        "x_sample": nrm(ks[1], (DEC_BATCH, DEC_SEQ, D), 1.0),
        "c_prompt": nrm(ks[2], (BATCH, D), 1.0),
        "c_sample": nrm(ks[3], (DEC_BATCH, D), 1.0),
        "w_ada": nrm(ks[4], (DEPTH, D, 6 * D), D ** -0.5),
        "b_ada": nrm(ks[5], (DEPTH, 6 * D), 0.02),
        "w_in": nrm(ks[6], (DEPTH, D, IN_COLS), D ** -0.5),
        "lam_q1": nrm(ks[7], (DEPTH, HEAD_DIM), 0.1),
        "lam_k1": nrm(ks[8], (DEPTH, HEAD_DIM), 0.1),
        "lam_q2": nrm(ks[9], (DEPTH, HEAD_DIM), 0.1),
        "lam_k2": nrm(ks[10], (DEPTH, HEAD_DIM), 0.1),
        "subln_g": 1.0 + nrm(ks[11], (DEPTH, 2 * HEAD_DIM), 0.02),
        "q_norm_g": 1.0 + nrm(ks[12], (DEPTH, HEAD_DIM), 0.02),
        "k_norm_g": 1.0 + nrm(ks[13], (DEPTH, HEAD_DIM), 0.02),
        "w_out": nrm(ks[14], (DEPTH, MIX_WIDTH, D), BETA * MIX_WIDTH ** -0.5),
        "ln1_g": 1.0 + nrm(ks[15], (DEPTH, D), 0.02),
        "ln1_b": nrm(ks[16], (DEPTH, D), 0.02),
        "w_router": nrm(ks[17], (DEPTH, D, N_EXPERTS), D ** -0.5),
        "w_gate": nrm(ks[18], (DEPTH, N_EXPERTS, D, EXPERT_FF), D ** -0.5),
        "w_up": nrm(ks[19], (DEPTH, N_EXPERTS, D, EXPERT_FF), D ** -0.5),
        "w_down": nrm(ks[20], (DEPTH, N_EXPERTS, EXPERT_FF, D), BETA * EXPERT_FF ** -0.5),
        "ln2_g": 1.0 + nrm(ks[21], (DEPTH, D), 0.02),
        "ln2_b": nrm(ks[22], (DEPTH, D), 0.02),
    }


def reference(x_prompt, x_sample, c_prompt, c_sample, w_ada, b_ada, w_in, lam_q1,
              lam_k1, lam_q2, lam_k2, subln_g, q_norm_g, k_norm_g, w_out, ln1_g,
              ln1_b, w_router, w_gate, w_up, w_down, ln2_g, ln2_b):
    y_prompt = run_trunk(x_prompt, c_prompt, w_ada, b_ada, w_in, lam_q1, lam_k1, lam_q2,
                         lam_k2, subln_g, q_norm_g, k_norm_g, w_out, ln1_g, ln1_b,
                         w_router, w_gate, w_up, w_down, ln2_g, ln2_b)
    y_sample = run_trunk(x_sample, c_sample, w_ada, b_ada, w_in, lam_q1, lam_k1, lam_q2,
                         lam_k2, subln_g, q_norm_g, k_norm_g, w_out, ln1_g, ln1_b,
                         w_router, w_gate, w_up, w_down, ln2_g, ln2_b)
    return (y_prompt, y_sample)
```

```python
import functools
import math

import jax
import jax.numpy as jnp
from jax import lax
from jax.experimental import pallas as pl
from jax.experimental.pallas import tpu as pltpu

F32 = jnp.float32
BF16 = jnp.bfloat16
I32 = jnp.int32

HEAD_DIM = 64
N_HEADS_A = 4
WIDTH_A = N_HEADS_A * 2 * HEAD_DIM
N_HEADS_B = 8
N_KV_B = 2
GROUP_B = N_HEADS_B // N_KV_B
WIDTH_B = N_HEADS_B * HEAD_DIM
KV_B = N_KV_B * HEAD_DIM
GRID_W = 64
ROPE_THETA = 10000.0
N_EXPERTS = 16
CAPACITY_FACTOR = 2
DEPTH = 1
ALPHA = (2.0 * DEPTH) ** 0.25
LN_EPS = 1e-5
RMS_EPS = 1e-6
LAM_INIT = 0.8 - 0.6 * math.exp(-0.3 * 0)

LANES = 128
SLOT_BLOCK = 256
VMEM_LIMIT = 56 * 1024 * 1024

_NT = (((1,), (1,)), ((), ()))


def _split_bf16(a):
    hi = a.astype(BF16)
    lo = (a - hi.astype(F32)).astype(BF16)
    return hi, lo


def _dot3(a, b, dims=(((1,), (0,)), ((), ()))):
    ah, al = _split_bf16(a)
    bh, bl = _split_bf16(b)
    d = functools.partial(lax.dot_general, dimension_numbers=dims, preferred_element_type=F32)
    return d(ah, bh) + (d(ah, bl) + d(al, bh))


def _layer_norm(x):
    mu = jnp.mean(x, axis=-1, keepdims=True)
    xc = x - mu
    var = jnp.mean(xc * xc, axis=-1, keepdims=True)
    return xc * lax.rsqrt(var + LN_EPS)


def _lanes(x, n):
    if n == LANES:
        return x
    if n < LANES:
        return x[:, :n]
    return jnp.concatenate([x] * (n // LANES), axis=1)


def _mod_kernel(c_ref, w_ref, b_ref, lq1_ref, lk1_ref, lq2_ref, lk2_ref, mod_ref, lam_ref):
    c = c_ref[...]
    a = c * jax.nn.sigmoid(c)
    mod_ref[...] = _dot3(a, w_ref[...]) + b_ref[...]
    s1 = jnp.sum(lq1_ref[...] * lk1_ref[...], axis=-1, keepdims=True)
    s2 = jnp.sum(lq2_ref[...] * lk2_ref[...], axis=-1, keepdims=True)
    lam = jnp.exp(s1) - jnp.exp(s2) + LAM_INIT
    lam_ref[...] = jnp.broadcast_to(lam, lam_ref.shape)


def _mod_call(c, w_ada, b_ada, lq1, lk1, lq2, lk2):
    bt, d = c.shape
    n_chunks = w_ada.shape[1] // d
    vec = pl.BlockSpec((1, HEAD_DIM), lambda j: (0, 0))
    return pl.pallas_call(
        _mod_kernel,
        grid=(n_chunks,),
        in_specs=[
            pl.BlockSpec((bt, d), lambda j: (0, 0)),
            pl.BlockSpec((d, d), lambda j: (0, j)),
            pl.BlockSpec((1, d), lambda j: (0, j)),
            vec, vec, vec, vec,
        ],
        out_specs=[
            pl.BlockSpec((bt, d), lambda j: (0, j)),
            pl.BlockSpec((1, LANES), lambda j: (0, 0)),
        ],
        out_shape=[
            jax.ShapeDtypeStruct((bt, n_chunks * d), F32),
            jax.ShapeDtypeStruct((1, LANES), F32),
        ],
        compiler_params=pltpu.CompilerParams(vmem_limit_bytes=VMEM_LIMIT),
        name="mod",
    )(c, w_ada, b_ada, lq1, lk1, lq2, lk2)


def _head_rms(x, bd, g):
    sq = x * x
    hi, lo = _split_bf16(sq)
    ss = jnp.dot(hi, bd, preferred_element_type=F32) + jnp.dot(lo, bd, preferred_element_type=F32)
    return x * lax.rsqrt(ss * (1.0 / HEAD_DIM) + RMS_EPS) * g


def _rope(x, cos, sin_lo, sin_hi):
    w = x.shape[1]
    nxt = pltpu.roll(x, w - 16, axis=1)
    prv = pltpu.roll(x, 16, axis=1)
    return x * cos + nxt * sin_lo + prv * sin_hi


def _proj_kernel(x_ref, mod_ref, w_ref, gq_ref, gk_ref, cos_ref, slo_ref, shi_ref, bd_ref,
                 qa_ref, ka_ref, va_ref, qb_ref, kb_ref, vb_ref):
    x = x_ref[0]
    h = _layer_norm(x) * (1.0 + mod_ref[0, 1:2, :]) + mod_ref[0, 0:1, :]
    hb = h.astype(BF16)

    def proj(lo, width):
        return jnp.dot(hb, w_ref[:, lo:lo + width], preferred_element_type=F32)

    scale = HEAD_DIM ** -0.5
    qa_ref[0] = (proj(0, WIDTH_A) * scale).astype(BF16)
    ka_ref[0] = proj(WIDTH_A, WIDTH_A).astype(BF16)
    va_ref[0] = proj(2 * WIDTH_A, WIDTH_A).astype(BF16)

    cos, slo, shi = cos_ref[...], slo_ref[...], shi_ref[...]
    bd = bd_ref[...]
    qb = _head_rms(proj(3 * WIDTH_A, WIDTH_B), bd, gq_ref[...])
    qb = _rope(qb, _lanes(cos, WIDTH_B), _lanes(slo, WIDTH_B), _lanes(shi, WIDTH_B)) * scale
    for hd in range(N_HEADS_B):
        qb_ref[0, hd] = qb[:, hd * HEAD_DIM:(hd + 1) * HEAD_DIM].astype(BF16)
    kb = _head_rms(proj(3 * WIDTH_A + WIDTH_B, KV_B), bd[:KV_B, :KV_B], gk_ref[...])
    kb = _rope(kb, cos, slo, shi)
    vb = proj(3 * WIDTH_A + WIDTH_B + KV_B, KV_B)
    for hd in range(N_KV_B):
        kb_ref[0, hd] = kb[:, hd * HEAD_DIM:(hd + 1) * HEAD_DIM].astype(BF16)
        vb_ref[0, hd] = vb[:, hd * HEAD_DIM:(hd + 1) * HEAD_DIM].astype(BF16)


def _proj_call(x, mod, w_in, gq, gk, cos, slo, shi, bd, tm):
    b, s, d = x.shape
    nt = s // tm
    const = lambda shape: pl.BlockSpec(shape, lambda bi, ti: (0,) * len(shape))
    tok = lambda width: pl.BlockSpec((1, tm, width), lambda bi, ti: (bi, ti, 0))
    heads = lambda n: pl.BlockSpec((1, n, tm, HEAD_DIM), lambda bi, ti: (bi, 0, ti, 0))
    tab = pl.BlockSpec((tm, LANES), lambda bi, ti: (ti, 0))
    return pl.pallas_call(
        _proj_kernel,
        grid=(b, nt),
        in_specs=[
            tok(d),
            pl.BlockSpec((1, 6, d), lambda bi, ti: (bi, 0, 0)),
            const(w_in.shape),
            const(gq.shape), const(gk.shape),
            tab, tab, tab,
            const(bd.shape),
        ],
        out_specs=[tok(WIDTH_A), tok(WIDTH_A), tok(WIDTH_A),
                   heads(N_HEADS_B), heads(N_KV_B), heads(N_KV_B)],
        out_shape=[
            jax.ShapeDtypeStruct((b, s, WIDTH_A), BF16),
            jax.ShapeDtypeStruct((b, s, WIDTH_A), BF16),
            jax.ShapeDtypeStruct((b, s, WIDTH_A), BF16),
            jax.ShapeDtypeStruct((b, N_HEADS_B, s, HEAD_DIM), BF16),
            jax.ShapeDtypeStruct((b, N_KV_B, s, HEAD_DIM), BF16),
            jax.ShapeDtypeStruct((b, N_KV_B, s, HEAD_DIM), BF16),
        ],
        compiler_params=pltpu.CompilerParams(
            dimension_semantics=("parallel", "parallel"), vmem_limit_bytes=VMEM_LIMIT),
        name="proj",
    )(x, mod, w_in, gq, gk, cos, slo, shi, bd)


def _online_softmax_step(s, v, m_ref, l_ref, acc_ref):
    tk = s.shape[1]
    width = acc_ref.shape[-1]
    m_prev = m_ref[...]
    m_new = jnp.maximum(m_prev, jnp.max(s, axis=1, keepdims=True))
    alpha = jnp.exp(m_prev - m_new)
    p = jnp.exp(s - _lanes(m_new, tk))
    l_ref[...] = alpha * l_ref[...] + jnp.sum(p, axis=1, keepdims=True)
    acc_ref[...] = _lanes(alpha, width) * acc_ref[...] + jnp.dot(
        p.astype(BF16), v, preferred_element_type=F32)
    m_ref[...] = m_new


def _diff_attn_kernel(slope_ref, lam_ref, g_ref, q_ref, k_ref, v_ref, o_ref,
                      m_sc, l_sc, acc_sc, *, tq, tk):
    qi = pl.program_id(2)
    ki = pl.program_id(3)

    @pl.when(ki == 0)
    def _():
        m_sc[...] = jnp.full(m_sc.shape, -jnp.inf, F32)
        l_sc[...] = jnp.zeros(l_sc.shape, F32)
        acc_sc[...] = jnp.zeros(acc_sc.shape, F32)

    q = q_ref[0]
    k = k_ref[0]
    v = v_ref[0]
    rows = qi * tq + lax.broadcasted_iota(I32, (tq, tk), 0)
    cols = ki * tk + lax.broadcasted_iota(I32, (tq, tk), 1)
    bias = jnp.abs(rows - cols).astype(F32) * slope_ref[0][:, :1]
    lane = lax.broadcasted_iota(I32, q.shape, 1)
    for c in range(2):
        keep = (lane < HEAD_DIM) if c == 0 else (lane >= HEAD_DIM)
        qc = jnp.where(keep, q, jnp.zeros_like(q))
        s = lax.dot_general(qc, k, _NT, preferred_element_type=F32) - bias
        _online_softmax_step(s, v, m_sc.at[c], l_sc.at[c], acc_sc.at[c])

    @pl.when(ki == pl.num_programs(3) - 1)
    def _():
        lam = lam_ref[...]
        o = acc_sc[0] / l_sc[0] - lam * (acc_sc[1] / l_sc[1])
        ms = jnp.mean(o * o, axis=-1, keepdims=True)
        o = o * lax.rsqrt(ms + RMS_EPS) * g_ref[...] * (1.0 - LAM_INIT)
        o_ref[0] = o.astype(o_ref.dtype)


def _diff_attn_call(qa, ka, va, slopes, lam, g, tq, tk):
    b, s, _ = qa.shape
    hw = 2 * HEAD_DIM
    return pl.pallas_call(
        functools.partial(_diff_attn_kernel, tq=tq, tk=tk),
        grid=(b, N_HEADS_A, s // tq, s // tk),
        in_specs=[
            pl.BlockSpec((1, 1, LANES), lambda bi, h, qi, ki: (h, 0, 0)),
            pl.BlockSpec((1, LANES), lambda bi, h, qi, ki: (0, 0)),
            pl.BlockSpec((1, hw), lambda bi, h, qi, ki: (0, 0)),
            pl.BlockSpec((1, tq, hw), lambda bi, h, qi, ki: (bi, qi, h)),
            pl.BlockSpec((1, tk, hw), lambda bi, h, qi, ki: (bi, ki, h)),
            pl.BlockSpec((1, tk, hw), lambda bi, h, qi, ki: (bi, ki, h)),
        ],
        out_specs=pl.BlockSpec((1, tq, hw), lambda bi, h, qi, ki: (bi, qi, h)),
        out_shape=jax.ShapeDtypeStruct((b, s, WIDTH_A), BF16),
        scratch_shapes=[
            pltpu.VMEM((2, tq, LANES), F32),
            pltpu.VMEM((2, tq, LANES), F32),
            pltpu.VMEM((2, tq, hw), F32),
        ],
        compiler_params=pltpu.CompilerParams(
            dimension_semantics=("parallel", "parallel", "parallel", "arbitrary"),
            vmem_limit_bytes=VMEM_LIMIT),
        name="diff_attn",
    )(slopes, lam, g, qa, ka, va)


def _gqa_kernel(q_ref, k_ref, v_ref, o_ref, m_sc, l_sc, acc_sc, *, tq):
    ki = pl.program_id(3)

    @pl.when(ki == 0)
    def _():
        m_sc[...] = jnp.full(m_sc.shape, -jnp.inf, F32)
        l_sc[...] = jnp.zeros(l_sc.shape, F32)
        acc_sc[...] = jnp.zeros(acc_sc.shape, F32)

    q = q_ref[0].reshape(GROUP_B * tq, HEAD_DIM)
    s = lax.dot_general(q, k_ref[0, 0], _NT, preferred_element_type=F32)
    _online_softmax_step(s, v_ref[0, 0], m_sc, l_sc, acc_sc)

    @pl.when(ki == pl.num_programs(3) - 1)
    def _():
        o = acc_sc[...] / _lanes(l_sc[...], HEAD_DIM)
        o_ref[0] = o.reshape(GROUP_B, tq, HEAD_DIM).astype(o_ref.dtype)


def _gqa_call(qb, kb, vb, tq, tk):
    b, _, s, _ = qb.shape
    rows = GROUP_B * tq
    return pl.pallas_call(
        functools.partial(_gqa_kernel, tq=tq),
        grid=(b, N_KV_B, s // tq, s // tk),
        in_specs=[
            pl.BlockSpec((1, GROUP_B, tq, HEAD_DIM), lambda bi, n, qi, ki: (bi, n, qi, 0)),
            pl.BlockSpec((1, 1, tk, HEAD_DIM), lambda bi, n, qi, ki: (bi, n, ki, 0)),
            pl.BlockSpec((1, 1, tk, HEAD_DIM), lambda bi, n, qi, ki: (bi, n, ki, 0)),
        ],
        out_specs=pl.BlockSpec((1, GROUP_B, tq, HEAD_DIM), lambda bi, n, qi, ki: (bi, n, qi, 0)),
        out_shape=jax.ShapeDtypeStruct((b, N_HEADS_B, s, HEAD_DIM), BF16),
        scratch_shapes=[
            pltpu.VMEM((rows, LANES), F32),
            pltpu.VMEM((rows, LANES), F32),
            pltpu.VMEM((rows, HEAD_DIM), F32),
        ],
        compiler_params=pltpu.CompilerParams(
            dimension_semantics=("parallel", "parallel", "parallel", "arbitrary"),
            vmem_limit_bytes=VMEM_LIMIT),
        name="gqa_attn",
    )(qb, kb, vb)


def _post_kernel(oa_ref, ob_ref, x_ref, mod_ref, w_ref, g_ref, b_ref, wrt_ref,
                 x1_ref, h2_ref, aff_ref):
    o = jnp.dot(oa_ref[0], w_ref[:WIDTH_A, :], preferred_element_type=F32)
    for hd in range(N_HEADS_B):
        lo = WIDTH_A + hd * HEAD_DIM
        o = o + jnp.dot(ob_ref[0, hd], w_ref[lo:lo + HEAD_DIM, :], preferred_element_type=F32)
    r = ALPHA * x_ref[0] + mod_ref[0, 2:3, :] * o
    x1 = _layer_norm(r) * g_ref[...] + b_ref[...]
    x1_ref[0] = x1
    h2 = _layer_norm(x1) * (1.0 + mod_ref[0, 4:5, :]) + mod_ref[0, 3:4, :]
    h2_ref[0] = h2.astype(BF16)
    logits = _dot3(wrt_ref[...], h2, _NT)
    z = jnp.exp(logits - jnp.max(logits, axis=0, keepdims=True))
    aff_ref[...] = z / jnp.sum(z, axis=0, keepdims=True)


def _post_call(oa, ob, x, mod, w_out, ln_g, ln_b, w_router_t, tm):
    b, s, d = x.shape
    nt = s // tm
    const = lambda shape: pl.BlockSpec(shape, lambda bi, ti: (0,) * len(shape))
    tok = lambda width: pl.BlockSpec((1, tm, width), lambda bi, ti: (bi, ti, 0))
    return pl.pallas_call(
        _post_kernel,
        grid=(b, nt),
        in_specs=[
            tok(WIDTH_A),
            pl.BlockSpec((1, N_HEADS_B, tm, HEAD_DIM), lambda bi, ti: (bi, 0, ti, 0)),
            tok(d),
            pl.BlockSpec((1, 6, d), lambda bi, ti: (bi, 0, 0)),
            const(w_out.shape), const(ln_g.shape), const(ln_b.shape), const(w_router_t.shape),
        ],
        out_specs=[tok(d), tok(d),
                   pl.BlockSpec((N_EXPERTS, tm), lambda bi, ti: (0, bi * nt + ti))],
        out_shape=[
            jax.ShapeDtypeStruct((b, s, d), F32),
            jax.ShapeDtypeStruct((b, s, d), BF16),
            jax.ShapeDtypeStruct((N_EXPERTS, b * s), F32),
        ],
        compiler_params=pltpu.CompilerParams(
            dimension_semantics=("parallel", "parallel"), vmem_limit_bytes=VMEM_LIMIT),
        name="post",
    )(oa, ob, x, mod, w_out, ln_g, ln_b, w_router_t)


def _route_kernel(aff_ref, spos_ref, cnt_ref, *, cap, idx_bits):
    aff = aff_ref[...]
    n_e, n_r, _ = aff.shape
    bits = pltpu.bitcast(aff, I32)
    idx = (lax.broadcasted_iota(I32, aff.shape, 1) * LANES
           + lax.broadcasted_iota(I32, aff.shape, 2))

    def count(pred):
        part = jnp.sum(jnp.where(pred, 1.0, 0.0), axis=1, keepdims=True)
        return jnp.sum(part, axis=2, keepdims=True)

    def value_step(t, prefix):
        cand = prefix | (1 << (30 - t))
        return jnp.where(count(bits >= cand) >= cap, cand, prefix)

    thr = lax.fori_loop(0, 31, value_step, jnp.zeros((n_e, 1, 1), I32))
    above = bits > thr
    tie = bits == thr
    need = cap - count(above)

    def index_step(t, prefix):
        cand = prefix | (1 << (idx_bits - 1 - t))
        return jnp.where(count(tie & (idx < cand)) < need, cand, prefix)

    last = lax.fori_loop(0, idx_bits, index_step, jnp.zeros((n_e, 1, 1), I32))
    sel = above | (tie & (idx <= last))

    upper = (lax.broadcasted_iota(I32, (LANES, LANES), 0)
             <= lax.broadcasted_iota(I32, (LANES, LANES), 1)).astype(BF16)
    strict_lower = (lax.broadcasted_iota(I32, (n_r, n_r), 1)
                    < lax.broadcasted_iota(I32, (n_r, n_r), 0)).astype(BF16)
    for e in range(n_e):
        sel_e = sel[e]
        m = sel_e.astype(BF16)
        incl = jnp.dot(m, upper, preferred_element_type=F32)
        row_tot = jnp.broadcast_to(incl[:, LANES - 1:], incl.shape).astype(BF16)
        offs = jnp.dot(strict_lower, row_tot, preferred_element_type=F32)
        cnt = (incl + offs).astype(I32)
        cnt_ref[e] = cnt
        spos_ref[e] = jnp.where(sel_e, cnt - 1, -1)


def _route_call(aff3, cap):
    n_e, n_r, _ = aff3.shape
    idx_bits = max(1, (n_r * LANES - 1).bit_length())
    return pl.pallas_call(
        functools.partial(_route_kernel, cap=cap, idx_bits=idx_bits),
        out_shape=[jax.ShapeDtypeStruct(aff3.shape, I32), jax.ShapeDtypeStruct(aff3.shape, I32)],
        compiler_params=pltpu.CompilerParams(vmem_limit_bytes=VMEM_LIMIT),
        name="route",
    )(aff3)


def _ffn_kernel(tend_ref, spos_ref, g_ref, x_ref, wg_ref, wu_ref, wd_ref, ye_ref,
                stage_sc, gate_sc):
    e = pl.program_id(0)
    i = pl.program_id(1)
    prev_end = jnp.where(i > 0, tend_ref[e, jnp.maximum(i - 1, 0)], 0)
    base = (prev_end // SLOT_BLOCK) * SLOT_BLOCK
    fill = tend_ref[e, i] - base

    @pl.when(i == 0)
    def _():
        stage_sc[...] = jnp.zeros(stage_sc.shape, F32)
        gate_sc[...] = jnp.zeros(gate_sc.shape, F32)

    tt = x_ref.shape[0]
    rel = spos_ref[0, 0] - base
    hit = lax.broadcasted_iota(I32, (2 * SLOT_BLOCK, tt), 0) == rel
    onehot = jnp.where(hit, 1.0, 0.0).astype(BF16)
    stage_sc[...] += jnp.dot(onehot, x_ref[...], preferred_element_type=F32)
    gate_sc[...] += jnp.sum(jnp.where(hit, g_ref[0, 0], 0.0), axis=1, keepdims=True)

    @pl.when(fill >= SLOT_BLOCK)
    def _():
        xb = stage_sc[:SLOT_BLOCK, :].astype(BF16)
        gate = jnp.dot(xb, wg_ref[0], preferred_element_type=F32)
        up = jnp.dot(xb, wu_ref[0], preferred_element_type=F32)
        hid = (gate * jax.nn.sigmoid(gate) * up).astype(BF16)
        y = jnp.dot(hid, wd_ref[0], preferred_element_type=F32) * gate_sc[:SLOT_BLOCK, :1]
        ye_ref[0] = y.astype(ye_ref.dtype)
        stage_sc[:SLOT_BLOCK, :] = stage_sc[SLOT_BLOCK:, :]
        stage_sc[SLOT_BLOCK:, :] = jnp.zeros((SLOT_BLOCK, stage_sc.shape[1]), F32)
        gate_sc[:SLOT_BLOCK, :] = gate_sc[SLOT_BLOCK:, :]
        gate_sc[SLOT_BLOCK:, :] = jnp.zeros((SLOT_BLOCK, LANES), F32)


def _ffn_call(tile_end, spos4, aff4, h2, wg, wu, wd, cap, tt):
    n, d = h2.shape
    n_t = n // tt
    ff = wg.shape[2]
    n_blk = cap // SLOT_BLOCK

    def out_map(e, i, tend):
        prev_end = jnp.where(i > 0, tend[e, jnp.maximum(i - 1, 0)], 0)
        return (e, jnp.clip(prev_end // SLOT_BLOCK, 0, n_blk - 1), 0)

    grid_spec = pltpu.PrefetchScalarGridSpec(
        num_scalar_prefetch=1,
        grid=(N_EXPERTS, n_t),
        in_specs=[
            pl.BlockSpec((1, 1, 1, tt), lambda e, i, tend: (e, i, 0, 0)),
            pl.BlockSpec((1, 1, 1, tt), lambda e, i, tend: (e, i, 0, 0)),
            pl.BlockSpec((tt, d), lambda e, i, tend: (i, 0)),
            pl.BlockSpec((1, d, ff), lambda e, i, tend: (e, 0, 0)),
            pl.BlockSpec((1, d, ff), lambda e, i, tend: (e, 0, 0)),
            pl.BlockSpec((1, ff, d), lambda e, i, tend: (e, 0, 0)),
        ],
        out_specs=pl.BlockSpec((1, SLOT_BLOCK, d), out_map),
        scratch_shapes=[
            pltpu.VMEM((2 * SLOT_BLOCK, d), F32),
            pltpu.VMEM((2 * SLOT_BLOCK, LANES), F32),
        ],
    )
    return pl.pallas_call(
        _ffn_kernel,
        grid_spec=grid_spec,
        out_shape=jax.ShapeDtypeStruct((N_EXPERTS, cap, d), BF16),
        compiler_params=pltpu.CompilerParams(
            dimension_semantics=("arbitrary", "arbitrary"), vmem_limit_bytes=VMEM_LIMIT),
        name="ffn",
    )(tile_end, spos4, aff4, h2, wg, wu, wd)


def _combine_kernel(blk_ref, spos_ref, ya_ref, yb_ref, x1_ref, mod_ref, g_ref, b_ref, o_ref, acc_sc):
    i = pl.program_id(0)
    e = pl.program_id(1)

    @pl.when(e == 0)
    def _():
        acc_sc[...] = jnp.zeros(acc_sc.shape, F32)

    spos = spos_ref[...]
    lane = lax.broadcasted_iota(I32, spos.shape, 1)
    pcol = jnp.sum(jnp.where(lane == e, spos, 0), axis=1, keepdims=True)
    rel = pcol - blk_ref[e, i] * SLOT_BLOCK
    slot = lax.broadcasted_iota(I32, (spos.shape[0], SLOT_BLOCK), 1)
    first = jnp.where(slot == rel, 1.0, 0.0).astype(BF16)
    second = jnp.where(slot == rel - SLOT_BLOCK, 1.0, 0.0).astype(BF16)
    acc_sc[...] += (jnp.dot(first, ya_ref[0], preferred_element_type=F32)
                    + jnp.dot(second, yb_ref[0], preferred_element_type=F32))

    @pl.when(e == pl.num_programs(1) - 1)
    def _():
        r = ALPHA * x1_ref[...] + mod_ref[0, 5:6, :] * acc_sc[...]
        o_ref[...] = _layer_norm(r) * g_ref[...] + b_ref[...]


def _combine_call(blk, spos_t, ye, x1, mod, ln_g, ln_b, tt, tokens_per_batch):
    n, d = x1.shape
    n_t = n // tt
    n_blk = ye.shape[1] // SLOT_BLOCK
    tiles_per_batch = tokens_per_batch // tt

    grid_spec = pltpu.PrefetchScalarGridSpec(
        num_scalar_prefetch=1,
        grid=(n_t, N_EXPERTS),
        in_specs=[
            pl.BlockSpec((tt, N_EXPERTS), lambda i, e, blk: (i, 0)),
            pl.BlockSpec((1, SLOT_BLOCK, d),
                         lambda i, e, blk: (e, jnp.clip(blk[e, i], 0, n_blk - 1), 0)),
            pl.BlockSpec((1, SLOT_BLOCK, d),
                         lambda i, e, blk: (e, jnp.clip(blk[e, i] + 1, 0, n_blk - 1), 0)),
            pl.BlockSpec((tt, d), lambda i, e, blk: (i, 0)),
            pl.BlockSpec((1, 6, d), lambda i, e, blk: (i // tiles_per_batch, 0, 0)),
            pl.BlockSpec((1, d), lambda i, e, blk: (0, 0)),
            pl.BlockSpec((1, d), lambda i, e, blk: (0, 0)),
        ],
        out_specs=pl.BlockSpec((tt, d), lambda i, e, blk: (i, 0)),
        scratch_shapes=[pltpu.VMEM((tt, d), F32)],
    )
    return pl.pallas_call(
        _combine_kernel,
        grid_spec=grid_spec,
        out_shape=jax.ShapeDtypeStruct((n, d), F32),
        compiler_params=pltpu.CompilerParams(
            dimension_semantics=("parallel", "arbitrary"), vmem_limit_bytes=VMEM_LIMIT),
        name="combine",
    )(blk, spos_t, ye, ye, x1, mod, ln_g, ln_b)


def _rope_tables(s):
    pos = jnp.arange(s)
    half = HEAD_DIM // 2
    inv_freq = ROPE_THETA ** (-jnp.arange(0, half, 2, dtype=F32) / half)
    ang_r = (pos // GRID_W).astype(F32)[:, None] * inv_freq[None, :]
    ang_c = (pos % GRID_W).astype(F32)[:, None] * inv_freq[None, :]
    zero = jnp.zeros_like(ang_r)
    cos = jnp.concatenate([jnp.cos(ang_r)] * 2 + [jnp.cos(ang_c)] * 2, axis=1)
    slo = jnp.concatenate([-jnp.sin(ang_r), zero, -jnp.sin(ang_c), zero], axis=1)
    shi = jnp.concatenate([zero, jnp.sin(ang_r), zero, jnp.sin(ang_c)], axis=1)
    rep = LANES // HEAD_DIM
    return tuple(jnp.concatenate([t] * rep, axis=1) for t in (cos, slo, shi))


def _tile(dim, target):
    t = min(dim, target)
    assert dim % t == 0, (dim, t)
    return t


def _trunk(x, mod, lam, consts, w):
    b, s, d = x.shape
    n = b * s
    cap = CAPACITY_FACTOR * n // N_EXPERTS
    assert cap % SLOT_BLOCK == 0 and n % LANES == 0
    tm = _tile(s, 512)
    tt = _tile(n, SLOT_BLOCK)

    qa, ka, va, qb, kb, vb = _proj_call(
        x, mod, w["w_in"], w["gq"], w["gk"], *consts["rope"], consts["bd"], tm)
    oa = _diff_attn_call(qa, ka, va, consts["slopes"], lam, w["subln_g"],
                         _tile(s, 512), _tile(s, 512))
    ob = _gqa_call(qb, kb, vb, _tile(s, 256), _tile(s, 512))
    x1, h2, aff_t = _post_call(oa, ob, x, mod, w["w_out"], w["ln1_g"], w["ln1_b"],
                               w["w_router_t"], tm)

    spos, cnt = _route_call(aff_t.reshape(N_EXPERTS, n // LANES, LANES), cap)
    spos = spos.reshape(N_EXPERTS, n)
    n_t = n // tt
    incl_end = cnt.reshape(N_EXPERTS, n)[:, tt - 1::tt]
    first_blk = jnp.concatenate(
        [jnp.zeros((N_EXPERTS, 1), I32), incl_end[:, :-1]], axis=1) // SLOT_BLOCK

    ye = _ffn_call(incl_end, spos.reshape(N_EXPERTS, n_t, 1, tt),
                   aff_t.reshape(N_EXPERTS, n_t, 1, tt), h2.reshape(n, d),
                   w["w_gate"], w["w_up"], w["w_down"], cap, tt)
    out = _combine_call(first_blk, spos.T, ye, x1.reshape(n, d), mod,
                        w["ln2_g"], w["ln2_b"], tt, s)
    return out.reshape(b, s, d)


def kernel(x_prompt, x_sample, c_prompt, c_sample, w_ada, b_ada, w_in, lam_q1, lam_k1, lam_q2,
           lam_k2, subln_g, q_norm_g, k_norm_g, w_out, ln1_g, ln1_b, w_router, w_gate, w_up,
           w_down, ln2_g, ln2_b):
    l = 0
    d = x_prompt.shape[-1]
    c_all = jnp.concatenate([c_prompt, c_sample], axis=0)
    mod, lam = _mod_call(c_all, w_ada[l], b_ada[l][None, :], lam_q1[l][None, :], lam_k1[l][None, :],
                         lam_q2[l][None, :], lam_k2[l][None, :])
    mod = mod.reshape(c_all.shape[0], 6, d)

    w = {
        "w_in": w_in[l].astype(BF16),
        "gq": jnp.tile(q_norm_g[l], N_HEADS_B)[None, :],
        "gk": jnp.tile(k_norm_g[l], N_KV_B)[None, :],
        "subln_g": subln_g[l][None, :],
        "w_out": w_out[l].astype(BF16),
        "ln1_g": ln1_g[l][None, :], "ln1_b": ln1_b[l][None, :],
        "w_router_t": w_router[l].T,
        "w_gate": w_gate[l].astype(BF16), "w_up": w_up[l].astype(BF16),
        "w_down": w_down[l].astype(BF16),
        "ln2_g": ln2_g[l][None, :], "ln2_b": ln2_b[l][None, :],
    }
    head_of = jnp.arange(WIDTH_B) // HEAD_DIM
    slopes = jnp.asarray([2.0 ** (-8.0 * (h + 1) / N_HEADS_A) for h in range(N_HEADS_A)], F32)
    shared = {
        "bd": (head_of[:, None] == head_of[None, :]).astype(BF16),
        "slopes": jnp.broadcast_to(slopes[:, None, None], (N_HEADS_A, 1, LANES)),
    }

    outs = []
    nb = x_prompt.shape[0]
    for x, m in ((x_prompt, mod[:nb]), (x_sample, mod[nb:])):
        consts = dict(shared, rope=_rope_tables(x.shape[1]))
        outs.append(_trunk(x, m, lam, consts, w))
    return tuple(outs)
```

```python
import functools
import math

import jax
import jax.numpy as jnp
import numpy as np
from jax import lax
from jax.experimental import pallas as pl
from jax.experimental.pallas import tpu as pltpu

F32 = jnp.float32
BF16 = jnp.bfloat16
I32 = jnp.int32

HEAD_DIM = 64
N_HEADS_A = 4
WIDTH_A = N_HEADS_A * 2 * HEAD_DIM
N_HEADS_B = 8
N_KV_B = 2
GROUP_B = N_HEADS_B // N_KV_B
WIDTH_B = N_HEADS_B * HEAD_DIM
KV_B = N_KV_B * HEAD_DIM
GRID_W = 64
ROPE_THETA = 10000.0
N_EXPERTS = 16
CAPACITY_FACTOR = 2
DEPTH = 1
ALPHA = (2.0 * DEPTH) ** 0.25
LN_EPS = 1e-5
RMS_EPS = 1e-6
LAM_INIT = 0.8 - 0.6 * math.exp(-0.3 * 0)

LOG2E = 1.4426950408889634
SOFTMAX_CHUNK = 32
N_BIAS_COLS = 12
ATTN_SUBTILES = 4

LANES = 128
SLOT_BLOCK = 256
VMEM_LIMIT = 56 * 1024 * 1024

_NT = (((1,), (1,)), ((), ()))


def _split_bf16(a):
    hi = a.astype(BF16)
    lo = (a - hi.astype(F32)).astype(BF16)
    return hi, lo


def _dot3(a, b, dims=(((1,), (0,)), ((), ()))):
    ah, al = _split_bf16(a)
    bh, bl = _split_bf16(b)
    d = functools.partial(lax.dot_general, dimension_numbers=dims, preferred_element_type=F32)
    return d(ah, bh) + (d(ah, bl) + d(al, bh))


def _layer_norm(x):
    mu = jnp.mean(x, axis=-1, keepdims=True)
    xc = x - mu
    var = jnp.mean(xc * xc, axis=-1, keepdims=True)
    return xc * lax.rsqrt(var + LN_EPS)


def _lanes(x, n):
    if n == LANES:
        return x
    if n < LANES:
        return x[:, :n]
    return jnp.concatenate([x] * (n // LANES), axis=1)


def _mod_kernel(c_ref, w_ref, b_ref, lq1_ref, lk1_ref, lq2_ref, lk2_ref, mod_ref, lam_ref):
    c = c_ref[...]
    a = c * jax.nn.sigmoid(c)
    mod_ref[...] = _dot3(a, w_ref[...]) + b_ref[...]
    s1 = jnp.sum(lq1_ref[...] * lk1_ref[...], axis=-1, keepdims=True)
    s2 = jnp.sum(lq2_ref[...] * lk2_ref[...], axis=-1, keepdims=True)
    lam = jnp.exp(s1) - jnp.exp(s2) + LAM_INIT
    lam_ref[...] = jnp.broadcast_to(lam, lam_ref.shape)


def _mod_call(c, w_ada, b_ada, lq1, lk1, lq2, lk2):
    bt, d = c.shape
    n_chunks = w_ada.shape[1] // d
    vec = pl.BlockSpec((1, HEAD_DIM), lambda j: (0, 0))
    return pl.pallas_call(
        _mod_kernel,
        grid=(n_chunks,),
        in_specs=[
            pl.BlockSpec((bt, d), lambda j: (0, 0)),
            pl.BlockSpec((d, d), lambda j: (0, j)),
            pl.BlockSpec((1, d), lambda j: (0, j)),
            vec, vec, vec, vec,
        ],
        out_specs=[
            pl.BlockSpec((bt, d), lambda j: (0, j)),
            pl.BlockSpec((1, LANES), lambda j: (0, 0)),
        ],
        out_shape=[
            jax.ShapeDtypeStruct((bt, n_chunks * d), F32),
            jax.ShapeDtypeStruct((1, LANES), F32),
        ],
        compiler_params=pltpu.CompilerParams(vmem_limit_bytes=VMEM_LIMIT),
        name="mod",
    )(c, w_ada, b_ada, lq1, lk1, lq2, lk2)


def _head_rms(x, bd, g):
    sq = x * x
    hi, lo = _split_bf16(sq)
    ss = jnp.dot(hi, bd, preferred_element_type=F32) + jnp.dot(lo, bd, preferred_element_type=F32)
    return x * lax.rsqrt(ss * (1.0 / HEAD_DIM) + RMS_EPS) * g


def _rope(x, cos, sin_lo, sin_hi):
    w = x.shape[1]
    nxt = pltpu.roll(x, w - 16, axis=1)
    prv = pltpu.roll(x, 16, axis=1)
    return x * cos + nxt * sin_lo + prv * sin_hi


def _proj_kernel(x_ref, mod_ref, w_ref, gq_ref, gk_ref, cos_ref, slo_ref, shi_ref, bd_ref, aq_ref, ak_ref,
                 qa_ref, ka_ref, va_ref, qb_ref, kb_ref, vb_ref):
    x = x_ref[0]
    h = _layer_norm(x) * (1.0 + mod_ref[0, 1:2, :]) + mod_ref[0, 0:1, :]
    hb = h.astype(BF16)

    def proj(lo, width):
        return jnp.dot(hb, w_ref[:, lo:lo + width], preferred_element_type=F32)

    scale = HEAD_DIM ** -0.5 * LOG2E
    qa = (proj(0, WIDTH_A) * scale).astype(BF16)
    ka = proj(WIDTH_A, WIDTH_A).astype(BF16)
    hw = 2 * HEAD_DIM
    for hd in range(N_HEADS_A):
        lo = hd * (hw + LANES)
        qa_ref[0, :, lo:lo + hw] = qa[:, hd * hw:(hd + 1) * hw]
        qa_ref[0, :, lo + hw:lo + hw + LANES] = aq_ref[:, hd * LANES:(hd + 1) * LANES]
        ka_ref[0, :, lo:lo + hw] = ka[:, hd * hw:(hd + 1) * hw]
        ka_ref[0, :, lo + hw:lo + hw + LANES] = ak_ref[:, hd * LANES:(hd + 1) * LANES]
    va_ref[0] = proj(2 * WIDTH_A, WIDTH_A).astype(BF16)

    cos, slo, shi = cos_ref[...], slo_ref[...], shi_ref[...]
    bd = bd_ref[...]
    qb = _head_rms(proj(3 * WIDTH_A, WIDTH_B), bd, gq_ref[...])
    qb = _rope(qb, _lanes(cos, WIDTH_B), _lanes(slo, WIDTH_B), _lanes(shi, WIDTH_B)) * scale
    for hd in range(N_HEADS_B):
        qb_ref[0, hd] = qb[:, hd * HEAD_DIM:(hd + 1) * HEAD_DIM].astype(BF16)
    kb = _head_rms(proj(3 * WIDTH_A + WIDTH_B, KV_B), bd[:KV_B, :KV_B], gk_ref[...])
    kb = _rope(kb, cos, slo, shi)
    vb = proj(3 * WIDTH_A + WIDTH_B + KV_B, KV_B)
    for hd in range(N_KV_B):
        kb_ref[0, hd] = kb[:, hd * HEAD_DIM:(hd + 1) * HEAD_DIM].astype(BF16)
        vb_ref[0, hd] = vb[:, hd * HEAD_DIM:(hd + 1) * HEAD_DIM].astype(BF16)


def _proj_call(x, mod, w_in, gq, gk, cos, slo, shi, bd, aq, ak, tm):
    b, s, d = x.shape
    nt = s // tm
    wide_a = N_HEADS_A * (2 * HEAD_DIM + LANES)
    bias_tab = pl.BlockSpec((tm, N_HEADS_A * LANES), lambda bi, ti: (ti, 0))
    const = lambda shape: pl.BlockSpec(shape, lambda bi, ti: (0,) * len(shape))
    tok = lambda width: pl.BlockSpec((1, tm, width), lambda bi, ti: (bi, ti, 0))
    heads = lambda n: pl.BlockSpec((1, n, tm, HEAD_DIM), lambda bi, ti: (bi, 0, ti, 0))
    tab = pl.BlockSpec((tm, LANES), lambda bi, ti: (ti, 0))
    return pl.pallas_call(
        _proj_kernel,
        grid=(b, nt),
        in_specs=[
            tok(d),
            pl.BlockSpec((1, 6, d), lambda bi, ti: (bi, 0, 0)),
            const(w_in.shape),
            const(gq.shape), const(gk.shape),
            tab, tab, tab,
            const(bd.shape),
            bias_tab, bias_tab,
        ],
        out_specs=[tok(wide_a), tok(wide_a), tok(WIDTH_A),
                   heads(N_HEADS_B), heads(N_KV_B), heads(N_KV_B)],
        out_shape=[
            jax.ShapeDtypeStruct((b, s, wide_a), BF16),
            jax.ShapeDtypeStruct((b, s, wide_a), BF16),
            jax.ShapeDtypeStruct((b, s, WIDTH_A), BF16),
            jax.ShapeDtypeStruct((b, N_HEADS_B, s, HEAD_DIM), BF16),
            jax.ShapeDtypeStruct((b, N_KV_B, s, HEAD_DIM), BF16),
            jax.ShapeDtypeStruct((b, N_KV_B, s, HEAD_DIM), BF16),
        ],
        compiler_params=pltpu.CompilerParams(
            dimension_semantics=("parallel", "parallel"), vmem_limit_bytes=VMEM_LIMIT),
        name="proj",
    )(x, mod, w_in, gq, gk, cos, slo, shi, bd, aq, ak)


def _softmax_update(s_ref, p_ref, m_ref, l_ref, acc_ref, bias_ref=None):
    rows, tk = s_ref.shape
    width = acc_ref.shape[-1]
    for r0 in range(0, rows, SOFTMAX_CHUNK):
        sl = slice(r0, r0 + SOFTMAX_CHUNK)
        s = s_ref[sl, :]
        if bias_ref is not None:
            s = s + bias_ref[sl, :]
            s_ref[sl, :] = s
        m_prev = m_ref[sl, :]
        m_new = jnp.maximum(m_prev, jnp.max(s, axis=1, keepdims=True))
        alpha = jnp.exp2(m_prev - m_new)
        m_ref[sl, :] = m_new
        l_ref[sl, :] = alpha * l_ref[sl, :]
        acc_ref[sl, :] = _lanes(alpha, width) * acc_ref[sl, :]
    for r0 in range(0, rows, SOFTMAX_CHUNK):
        sl = slice(r0, r0 + SOFTMAX_CHUNK)
        p = jnp.exp2(s_ref[sl, :] - _lanes(m_ref[sl, :], tk))
        part = p[:, :LANES]
        for j in range(1, tk // LANES):
            part = part + p[:, j * LANES:(j + 1) * LANES]
        l_ref[sl, :] += part
        p_ref[sl, :] = p.astype(BF16)


def _diff_attn_kernel(lam_ref, g_ref, dbias_ref, q_ref, *refs, n_sub, n_kt):
    k_refs, v_refs = refs[:n_sub], refs[n_sub:2 * n_sub]
    o_ref, qv_sc, s_sc, p_sc, m_sc, l_sc, acc_sc = refs[2 * n_sub:]
    qi = pl.program_id(2)
    ki = pl.program_id(3)
    hw = 2 * HEAD_DIM

    @pl.when(ki == 0)
    def _():
        m_sc[...] = jnp.full(m_sc.shape, -jnp.inf, F32)
        l_sc[...] = jnp.zeros(l_sc.shape, F32)
        acc_sc[...] = jnp.zeros(acc_sc.shape, F32)
        q = q_ref[0]
        zero = jnp.zeros_like(q)
        lane = lax.broadcasted_iota(I32, q.shape, 1)
        is_bias = lane >= hw
        for c in range(2):
            qc = jnp.where(is_bias | ((lane >= c * HEAD_DIM) & (lane < (c + 1) * HEAD_DIM)), q, zero)
            qv_sc[0, c] = qc
            qv_sc[1, c] = jnp.where(is_bias, -qc, qc)
            qv_sc[2, c] = jnp.where(is_bias, zero, qc)

    def body(ends_on_diagonal):
        for j in range(n_sub):
            diagonal = ends_on_diagonal and j == n_sub - 1
            kt = (qi + 1 + ki * n_sub + j) % n_kt
            version = 2 if diagonal else jnp.where(kt > qi, 0, 1)
            for c in range(2):
                s_sc[j, c] = lax.dot_general(qv_sc[version, c], k_refs[j][0], _NT,
                                             preferred_element_type=F32)
        for j in range(n_sub):
            diagonal = ends_on_diagonal and j == n_sub - 1
            for c in range(2):
                _softmax_update(s_sc.at[j, c], p_sc.at[j, c], m_sc.at[c], l_sc.at[c], acc_sc.at[c],
                                dbias_ref.at[0] if diagonal else None)
                acc_sc[c] += jnp.dot(p_sc[j, c], v_refs[j][0], preferred_element_type=F32)

    last = pl.num_programs(3) - 1

    @pl.when(ki != last)
    def _():
        body(False)

    @pl.when(ki == last)
    def _():
        body(True)

    @pl.when(ki == last)
    def _():
        l0 = jnp.sum(l_sc[0], axis=1, keepdims=True)
        l1 = jnp.sum(l_sc[1], axis=1, keepdims=True)
        o = acc_sc[0] / l0 - lam_ref[...] * (acc_sc[1] / l1)
        ms = jnp.mean(o * o, axis=-1, keepdims=True)
        o = o * lax.rsqrt(ms + RMS_EPS) * g_ref[...] * (1.0 - LAM_INIT)
        o_ref[0] = o.astype(o_ref.dtype)


def _diff_attn_call(qa, ka, va, dbias, lam, g, t, n_sub):
    b, s, _ = va.shape
    hw = 2 * HEAD_DIM
    qw = hw + LANES
    n_kt = s // t

    def kv_spec(width, j):
        return pl.BlockSpec(
            (1, t, width), lambda bi, h, qi, ki: (bi, (qi + 1 + ki * n_sub + j) % n_kt, h))

    return pl.pallas_call(
        functools.partial(_diff_attn_kernel, n_sub=n_sub, n_kt=n_kt),
        grid=(b, N_HEADS_A, n_kt, n_kt // n_sub),
        in_specs=[
            pl.BlockSpec((1, LANES), lambda bi, h, qi, ki: (0, 0)),
            pl.BlockSpec((1, hw), lambda bi, h, qi, ki: (0, 0)),
            pl.BlockSpec((1, t, t), lambda bi, h, qi, ki: (h, 0, 0)),
            pl.BlockSpec((1, t, qw), lambda bi, h, qi, ki: (bi, qi, h)),
            *[kv_spec(qw, j) for j in range(n_sub)],
            *[kv_spec(hw, j) for j in range(n_sub)],
        ],
        out_specs=pl.BlockSpec((1, t, hw), lambda bi, h, qi, ki: (bi, qi, h)),
        out_shape=jax.ShapeDtypeStruct((b, s, WIDTH_A), BF16),
        scratch_shapes=[
            pltpu.VMEM((3, 2, t, qw), BF16),
            pltpu.VMEM((n_sub, 2, t, t), F32),
            pltpu.VMEM((n_sub, 2, t, t), BF16),
            pltpu.VMEM((2, t, LANES), F32),
            pltpu.VMEM((2, t, LANES), F32),
            pltpu.VMEM((2, t, hw), F32),
        ],
        compiler_params=pltpu.CompilerParams(
            dimension_semantics=("parallel", "parallel", "parallel", "arbitrary"),
            vmem_limit_bytes=VMEM_LIMIT),
        name="diff_attn",
    )(lam, g, dbias, qa, *([ka] * n_sub), *([va] * n_sub))


def _gqa_kernel(q_ref, k_ref, v_ref, o_ref, s_sc, p_sc, m_sc, l_sc, acc_sc, *, tq, tk):
    ki = pl.program_id(3)
    rows = GROUP_B * tq

    @pl.when(ki == 0)
    def _():
        m_sc[...] = jnp.full(m_sc.shape, -jnp.inf, F32)
        l_sc[...] = jnp.zeros(l_sc.shape, F32)
        acc_sc[...] = jnp.zeros(acc_sc.shape, F32)

    n_sub = k_ref.shape[2] // tk
    q = q_ref[0].reshape(rows, HEAD_DIM)
    for j in range(n_sub):
        s_sc[j] = lax.dot_general(q, k_ref[0, 0, j * tk:(j + 1) * tk, :], _NT,
                                  preferred_element_type=F32)
    for j in range(n_sub):
        _softmax_update(s_sc.at[j], p_sc.at[j], m_sc, l_sc, acc_sc)
        acc_sc[...] += jnp.dot(p_sc[j], v_ref[0, 0, j * tk:(j + 1) * tk, :],
                               preferred_element_type=F32)

    @pl.when(ki == pl.num_programs(3) - 1)
    def _():
        o = acc_sc[...] / jnp.sum(l_sc[...], axis=1, keepdims=True)
        o_ref[0] = o.reshape(GROUP_B, tq, HEAD_DIM).astype(o_ref.dtype)


def _gqa_call(qb, kb, vb, tq, tk, n_sub):
    b, _, s, _ = qb.shape
    rows = GROUP_B * tq
    kv = n_sub * tk
    return pl.pallas_call(
        functools.partial(_gqa_kernel, tq=tq, tk=tk),
        grid=(b, N_KV_B, s // tq, s // kv),
        in_specs=[
            pl.BlockSpec((1, GROUP_B, tq, HEAD_DIM), lambda bi, n, qi, ki: (bi, n, qi, 0)),
            pl.BlockSpec((1, 1, kv, HEAD_DIM), lambda bi, n, qi, ki: (bi, n, ki, 0)),
            pl.BlockSpec((1, 1, kv, HEAD_DIM), lambda bi, n, qi, ki: (bi, n, ki, 0)),
        ],
        out_specs=pl.BlockSpec((1, GROUP_B, tq, HEAD_DIM), lambda bi, n, qi, ki: (bi, n, qi, 0)),
        out_shape=jax.ShapeDtypeStruct((b, N_HEADS_B, s, HEAD_DIM), BF16),
        scratch_shapes=[
            pltpu.VMEM((n_sub, rows, tk), F32),
            pltpu.VMEM((n_sub, rows, tk), BF16),
            pltpu.VMEM((rows, LANES), F32),
            pltpu.VMEM((rows, LANES), F32),
            pltpu.VMEM((rows, HEAD_DIM), F32),
        ],
        compiler_params=pltpu.CompilerParams(
            dimension_semantics=("parallel", "parallel", "parallel", "arbitrary"),
            vmem_limit_bytes=VMEM_LIMIT),
        name="gqa_attn",
    )(qb, kb, vb)


def _post_kernel(oa_ref, ob_ref, x_ref, mod_ref, w_ref, g_ref, b_ref, wrt_ref,
                 x1_ref, h2_ref, aff_ref):
    o = jnp.dot(oa_ref[0], w_ref[:WIDTH_A, :], preferred_element_type=F32)
    for hd in range(N_HEADS_B):
        lo = WIDTH_A + hd * HEAD_DIM
        o = o + jnp.dot(ob_ref[0, hd], w_ref[lo:lo + HEAD_DIM, :], preferred_element_type=F32)
    r = ALPHA * x_ref[0] + mod_ref[0, 2:3, :] * o
    x1 = _layer_norm(r) * g_ref[...] + b_ref[...]
    x1_ref[0] = x1
    h2 = _layer_norm(x1) * (1.0 + mod_ref[0, 4:5, :]) + mod_ref[0, 3:4, :]
    h2_ref[0] = h2.astype(BF16)
    logits = _dot3(wrt_ref[...], h2, _NT)
    z = jnp.exp(logits - jnp.max(logits, axis=0, keepdims=True))
    aff_ref[...] = z / jnp.sum(z, axis=0, keepdims=True)


def _post_call(oa, ob, x, mod, w_out, ln_g, ln_b, w_router_t, tm):
    b, s, d = x.shape
    nt = s // tm
    const = lambda shape: pl.BlockSpec(shape, lambda bi, ti: (0,) * len(shape))
    tok = lambda width: pl.BlockSpec((1, tm, width), lambda bi, ti: (bi, ti, 0))
    return pl.pallas_call(
        _post_kernel,
        grid=(b, nt),
        in_specs=[
            tok(WIDTH_A),
            pl.BlockSpec((1, N_HEADS_B, tm, HEAD_DIM), lambda bi, ti: (bi, 0, ti, 0)),
            tok(d),
            pl.BlockSpec((1, 6, d), lambda bi, ti: (bi, 0, 0)),
            const(w_out.shape), const(ln_g.shape), const(ln_b.shape), const(w_router_t.shape),
        ],
        out_specs=[tok(d), tok(d),
                   pl.BlockSpec((N_EXPERTS, tm), lambda bi, ti: (0, bi * nt + ti))],
        out_shape=[
            jax.ShapeDtypeStruct((b, s, d), F32),
            jax.ShapeDtypeStruct((b, s, d), BF16),
            jax.ShapeDtypeStruct((N_EXPERTS, b * s), F32),
        ],
        compiler_params=pltpu.CompilerParams(
            dimension_semantics=("parallel", "parallel"), vmem_limit_bytes=VMEM_LIMIT),
        name="post",
    )(oa, ob, x, mod, w_out, ln_g, ln_b, w_router_t)


def _route_kernel(aff_ref, spos_ref, cnt_ref, *, cap, idx_bits):
    aff = aff_ref[...]
    n_e, n_r, _ = aff.shape
    bits = pltpu.bitcast(aff, I32)
    idx = (lax.broadcasted_iota(I32, aff.shape, 1) * LANES
           + lax.broadcasted_iota(I32, aff.shape, 2))

    def count(pred):
        part = jnp.sum(jnp.where(pred, 1.0, 0.0), axis=1, keepdims=True)
        return jnp.sum(part, axis=2, keepdims=True)

    def value_step(t, prefix):
        cand = prefix | (1 << (30 - t))
        return jnp.where(count(bits >= cand) >= cap, cand, prefix)

    thr = lax.fori_loop(0, 31, value_step, jnp.zeros((n_e, 1, 1), I32))
    above = bits > thr
    tie = bits == thr
    need = cap - count(above)

    def index_step(t, prefix):
        cand = prefix | (1 << (idx_bits - 1 - t))
        return jnp.where(count(tie & (idx < cand)) < need, cand, prefix)

    last = lax.fori_loop(0, idx_bits, index_step, jnp.zeros((n_e, 1, 1), I32))
    sel = above | (tie & (idx <= last))

    upper = (lax.broadcasted_iota(I32, (LANES, LANES), 0)
             <= lax.broadcasted_iota(I32, (LANES, LANES), 1)).astype(BF16)
    strict_lower = (lax.broadcasted_iota(I32, (n_r, n_r), 1)
                    < lax.broadcasted_iota(I32, (n_r, n_r), 0)).astype(BF16)
    for e in range(n_e):
        sel_e = sel[e]
        m = sel_e.astype(BF16)
        incl = jnp.dot(m, upper, preferred_element_type=F32)
        row_tot = jnp.broadcast_to(incl[:, LANES - 1:], incl.shape).astype(BF16)
        offs = jnp.dot(strict_lower, row_tot, preferred_element_type=F32)
        cnt = (incl + offs).astype(I32)
        cnt_ref[e] = cnt
        spos_ref[e] = jnp.where(sel_e, cnt - 1, -1)


def _route_call(aff3, cap):
    n_e, n_r, _ = aff3.shape
    idx_bits = max(1, (n_r * LANES - 1).bit_length())
    return pl.pallas_call(
        functools.partial(_route_kernel, cap=cap, idx_bits=idx_bits),
        out_shape=[jax.ShapeDtypeStruct(aff3.shape, I32), jax.ShapeDtypeStruct(aff3.shape, I32)],
        compiler_params=pltpu.CompilerParams(vmem_limit_bytes=VMEM_LIMIT),
        name="route",
    )(aff3)


def _ffn_kernel(tend_ref, spos_ref, g_ref, x_ref, wg_ref, wu_ref, wd_ref, ye_ref,
                stage_sc, gate_sc):
    e = pl.program_id(0)
    i = pl.program_id(1)
    prev_end = jnp.where(i > 0, tend_ref[e, jnp.maximum(i - 1, 0)], 0)
    base = (prev_end // SLOT_BLOCK) * SLOT_BLOCK
    fill = tend_ref[e, i] - base

    @pl.when(i == 0)
    def _():
        stage_sc[...] = jnp.zeros(stage_sc.shape, F32)
        gate_sc[...] = jnp.zeros(gate_sc.shape, F32)

    tt = x_ref.shape[0]
    rel = spos_ref[0, 0] - base
    slot = lax.broadcasted_iota(I32, (SLOT_BLOCK, tt), 0)

    def place(lo, rel_lo):
        hit = slot == rel_lo
        onehot = jnp.where(hit, 1.0, 0.0).astype(BF16)
        rows = slice(lo, lo + SLOT_BLOCK)
        stage_sc[rows, :] += jnp.dot(onehot, x_ref[...], preferred_element_type=F32)
        gate_sc[rows, :] += jnp.sum(jnp.where(hit, g_ref[0, 0], 0.0), axis=1, keepdims=True)

    place(0, rel)

    @pl.when(fill > SLOT_BLOCK)
    def _():
        place(SLOT_BLOCK, rel - SLOT_BLOCK)

    @pl.when(fill >= SLOT_BLOCK)
    def _():
        xb = stage_sc[:SLOT_BLOCK, :].astype(BF16)
        gate = jnp.dot(xb, wg_ref[0], preferred_element_type=F32)
        up = jnp.dot(xb, wu_ref[0], preferred_element_type=F32)
        hid = (gate * jax.nn.sigmoid(gate) * up).astype(BF16)
        y = jnp.dot(hid, wd_ref[0], preferred_element_type=F32) * gate_sc[:SLOT_BLOCK, :1]
        ye_ref[0] = y.astype(ye_ref.dtype)
        stage_sc[:SLOT_BLOCK, :] = stage_sc[SLOT_BLOCK:, :]
        stage_sc[SLOT_BLOCK:, :] = jnp.zeros((SLOT_BLOCK, stage_sc.shape[1]), F32)
        gate_sc[:SLOT_BLOCK, :] = gate_sc[SLOT_BLOCK:, :]
        gate_sc[SLOT_BLOCK:, :] = jnp.zeros((SLOT_BLOCK, LANES), F32)


def _ffn_call(tile_end, spos4, aff4, h2, wg, wu, wd, cap, tt):
    n, d = h2.shape
    n_t = n // tt
    ff = wg.shape[2]
    n_blk = cap // SLOT_BLOCK

    def out_map(e, i, tend):
        prev_end = jnp.where(i > 0, tend[e, jnp.maximum(i - 1, 0)], 0)
        return (e, jnp.clip(prev_end // SLOT_BLOCK, 0, n_blk - 1), 0)

    grid_spec = pltpu.PrefetchScalarGridSpec(
        num_scalar_prefetch=1,
        grid=(N_EXPERTS, n_t),
        in_specs=[
            pl.BlockSpec((1, 1, 1, tt), lambda e, i, tend: (e, i, 0, 0)),
            pl.BlockSpec((1, 1, 1, tt), lambda e, i, tend: (e, i, 0, 0)),
            pl.BlockSpec((tt, d), lambda e, i, tend: (i, 0)),
            pl.BlockSpec((1, d, ff), lambda e, i, tend: (e, 0, 0)),
            pl.BlockSpec((1, d, ff), lambda e, i, tend: (e, 0, 0)),
            pl.BlockSpec((1, ff, d), lambda e, i, tend: (e, 0, 0)),
        ],
        out_specs=pl.BlockSpec((1, SLOT_BLOCK, d), out_map),
        scratch_shapes=[
            pltpu.VMEM((2 * SLOT_BLOCK, d), F32),
            pltpu.VMEM((2 * SLOT_BLOCK, LANES), F32),
        ],
    )
    return pl.pallas_call(
        _ffn_kernel,
        grid_spec=grid_spec,
        out_shape=jax.ShapeDtypeStruct((N_EXPERTS, cap, d), BF16),
        compiler_params=pltpu.CompilerParams(
            dimension_semantics=("arbitrary", "arbitrary"), vmem_limit_bytes=VMEM_LIMIT),
        name="ffn",
    )(tile_end, spos4, aff4, h2, wg, wu, wd)


def _slot_span(tend, e, i):
    lo = jnp.where(i > 0, tend[e, jnp.maximum(i - 1, 0)], 0)
    return lo, tend[e, i]


def _combine_kernel(tend_ref, spos_ref, ya_ref, yb_ref, x1_ref, mod_ref, g_ref, b_ref, o_ref, acc_sc):
    i = pl.program_id(0)
    e = pl.program_id(1)

    @pl.when(e == 0)
    def _():
        acc_sc[...] = jnp.zeros(acc_sc.shape, F32)

    lo, hi = _slot_span(tend_ref, e, i)
    blk = lo // SLOT_BLOCK
    spos = spos_ref[...].astype(F32)
    lane = lax.broadcasted_iota(I32, spos.shape, 1)
    pcol = jnp.sum(jnp.where(lane == e, spos, 0.0), axis=1, keepdims=True)
    rel = pcol.astype(I32) - blk * SLOT_BLOCK
    slot = lax.broadcasted_iota(I32, (spos.shape[0], SLOT_BLOCK), 1)

    @pl.when(hi > lo)
    def _():
        onehot = jnp.where(slot == rel, 1.0, 0.0).astype(BF16)
        acc_sc[...] += jnp.dot(onehot, ya_ref[0], preferred_element_type=F32)

    @pl.when(hi > (blk + 1) * SLOT_BLOCK)
    def _():
        onehot = jnp.where(slot == rel - SLOT_BLOCK, 1.0, 0.0).astype(BF16)
        acc_sc[...] += jnp.dot(onehot, yb_ref[0], preferred_element_type=F32)

    @pl.when(e == pl.num_programs(1) - 1)
    def _():
        r = ALPHA * x1_ref[...] + mod_ref[0, 5:6, :] * acc_sc[...]
        o_ref[...] = _layer_norm(r) * g_ref[...] + b_ref[...]


def _combine_call(tile_end, spos_t, ye, x1, mod, ln_g, ln_b, tt, tokens_per_batch):
    n, d = x1.shape
    n_t = n // tt
    n_blk = ye.shape[1] // SLOT_BLOCK
    tiles_per_batch = tokens_per_batch // tt
    ye_blocks = ye.reshape(N_EXPERTS * n_blk, SLOT_BLOCK, d)

    def first_map(i, e, tend):
        lo, _ = _slot_span(tend, e, i)
        return (e * n_blk + jnp.clip(lo // SLOT_BLOCK, 0, n_blk - 1), 0, 0)

    def second_map(i, e, tend):
        lo, hi = _slot_span(tend, e, i)
        nxt = lo // SLOT_BLOCK + 1
        spill = (hi > nxt * SLOT_BLOCK) & (nxt < n_blk)
        return (jnp.where(spill, e * n_blk + jnp.clip(nxt, 0, n_blk - 1), 0), 0, 0)

    grid_spec = pltpu.PrefetchScalarGridSpec(
        num_scalar_prefetch=1,
        grid=(n_t, N_EXPERTS),
        in_specs=[
            pl.BlockSpec((tt, N_EXPERTS), lambda i, e, tend: (i, 0)),
            pl.BlockSpec((1, SLOT_BLOCK, d), first_map),
            pl.BlockSpec((1, SLOT_BLOCK, d), second_map),
            pl.BlockSpec((tt, d), lambda i, e, tend: (i, 0)),
            pl.BlockSpec((1, 6, d), lambda i, e, tend: (i // tiles_per_batch, 0, 0)),
            pl.BlockSpec((1, d), lambda i, e, tend: (0, 0)),
            pl.BlockSpec((1, d), lambda i, e, tend: (0, 0)),
        ],
        out_specs=pl.BlockSpec((tt, d), lambda i, e, tend: (i, 0)),
        scratch_shapes=[pltpu.VMEM((tt, d), F32)],
    )
    return pl.pallas_call(
        _combine_kernel,
        grid_spec=grid_spec,
        out_shape=jax.ShapeDtypeStruct((n, d), F32),
        compiler_params=pltpu.CompilerParams(
            dimension_semantics=("parallel", "arbitrary"), vmem_limit_bytes=VMEM_LIMIT),
        name="combine",
    )(tile_end, spos_t, ye_blocks, ye_blocks, x1, mod, ln_g, ln_b)


def _rope_tables(s):
    pos = jnp.arange(s)
    half = HEAD_DIM // 2
    inv_freq = ROPE_THETA ** (-jnp.arange(0, half, 2, dtype=F32) / half)
    ang_r = (pos // GRID_W).astype(F32)[:, None] * inv_freq[None, :]
    ang_c = (pos % GRID_W).astype(F32)[:, None] * inv_freq[None, :]
    zero = jnp.zeros_like(ang_r)
    cos = jnp.concatenate([jnp.cos(ang_r)] * 2 + [jnp.cos(ang_c)] * 2, axis=1)
    slo = jnp.concatenate([-jnp.sin(ang_r), zero, -jnp.sin(ang_c), zero], axis=1)
    shi = jnp.concatenate([zero, jnp.sin(ang_r), zero, jnp.sin(ang_c)], axis=1)
    rep = LANES // HEAD_DIM
    return tuple(jnp.concatenate([t] * rep, axis=1) for t in (cos, slo, shi))


def _bf16_terms(value, n=3):
    terms, rem = [], np.float32(value)
    for _ in range(n):
        part = np.float32(np.asarray(rem, dtype=jnp.bfloat16))
        terms.append(float(part))
        rem = np.float32(rem - part)
    return terms


def _alibi_tables(s, t):
    pos = jnp.arange(s)
    hi = ((pos // LANES) * LANES).astype(F32)
    lo = (pos % LANES).astype(F32)
    d = jnp.arange(t)
    dist = jnp.abs(d[:, None] - d[None, :]).astype(F32)
    pad = [jnp.zeros((s,), F32)] * (LANES - N_BIAS_COLS)
    aq, ak, db = [], [], []
    for h in range(N_HEADS_A):
        terms = _bf16_terms(2.0 ** (-8.0 * (h + 1) / N_HEADS_A) * LOG2E)
        cc = [jnp.full((s,), term, F32) for term in terms]
        aq.append(jnp.stack(cc + cc + [hi] * 3 + [lo] * 3 + pad, axis=1))
        ak.append(jnp.stack([-hi] * 3 + [-lo] * 3 + cc + cc + pad, axis=1))
        db.append(-np.float32(sum(terms)) * dist)
    return (jnp.concatenate(aq, axis=1).astype(BF16), jnp.concatenate(ak, axis=1).astype(BF16),
            jnp.stack(db))


def _tile(dim, target):
    t = min(dim, target)
    assert dim % t == 0, (dim, t)
    return t


def _trunk(x, mod, lam, consts, w):
    b, s, d = x.shape
    n = b * s
    cap = CAPACITY_FACTOR * n // N_EXPERTS
    assert cap % SLOT_BLOCK == 0 and n % LANES == 0
    tm = _tile(s, 512)
    tt = _tile(n, SLOT_BLOCK)

    ta = _tile(s, 512)
    aq, ak, dbias = _alibi_tables(s, ta)
    qa, ka, va, qb, kb, vb = _proj_call(
        x, mod, w["w_in"], w["gq"], w["gk"], *consts["rope"], consts["bd"], aq, ak, tm)
    n_sub = min(ATTN_SUBTILES, s // ta)
    assert (s // ta) % n_sub == 0, (s, ta, n_sub)
    oa = _diff_attn_call(qa, ka, va, dbias, lam, w["subln_g"], ta, n_sub)
    ob = _gqa_call(qb, kb, vb, _tile(s, 256), ta, n_sub)
    x1, h2, aff_t = _post_call(oa, ob, x, mod, w["w_out"], w["ln1_g"], w["ln1_b"],
                               w["w_router_t"], tm)

    spos, cnt = _route_call(aff_t.reshape(N_EXPERTS, n // LANES, LANES), cap)
    spos = spos.reshape(N_EXPERTS, n)
    n_t = n // tt
    incl_end = cnt.reshape(N_EXPERTS, n)[:, tt - 1::tt]

    ye = _ffn_call(incl_end, spos.reshape(N_EXPERTS, n_t, 1, tt),
                   aff_t.reshape(N_EXPERTS, n_t, 1, tt), h2.reshape(n, d),
                   w["w_gate"], w["w_up"], w["w_down"], cap, tt)
    out = _combine_call(incl_end, spos.T, ye, x1.reshape(n, d), mod,
                        w["ln2_g"], w["ln2_b"], tt, s)
    return out.reshape(b, s, d)


def kernel(x_prompt, x_sample, c_prompt, c_sample, w_ada, b_ada, w_in, lam_q1, lam_k1, lam_q2,
           lam_k2, subln_g, q_norm_g, k_norm_g, w_out, ln1_g, ln1_b, w_router, w_gate, w_up,
           w_down, ln2_g, ln2_b):
    l = 0
    d = x_prompt.shape[-1]
    c_all = jnp.concatenate([c_prompt, c_sample], axis=0)
    mod, lam = _mod_call(c_all, w_ada[l], b_ada[l][None, :], lam_q1[l][None, :], lam_k1[l][None, :],
                         lam_q2[l][None, :], lam_k2[l][None, :])
    mod = mod.reshape(c_all.shape[0], 6, d)

    w = {
        "w_in": w_in[l].astype(BF16),
        "gq": jnp.tile(q_norm_g[l], N_HEADS_B)[None, :],
        "gk": jnp.tile(k_norm_g[l], N_KV_B)[None, :],
        "subln_g": subln_g[l][None, :],
        "w_out": w_out[l].astype(BF16),
        "ln1_g": ln1_g[l][None, :], "ln1_b": ln1_b[l][None, :],
        "w_router_t": w_router[l].T,
        "w_gate": w_gate[l].astype(BF16), "w_up": w_up[l].astype(BF16),
        "w_down": w_down[l].astype(BF16),
        "ln2_g": ln2_g[l][None, :], "ln2_b": ln2_b[l][None, :],
    }
    head_of = jnp.arange(WIDTH_B) // HEAD_DIM
    shared = {"bd": (head_of[:, None] == head_of[None, :]).astype(BF16)}

    outs = []
    nb = x_prompt.shape[0]
    for x, m in ((x_prompt, mod[:nb]), (x_sample, mod[nb:])):
        consts = dict(shared, rope=_rope_tables(x.shape[1]))
        outs.append(_trunk(x, m, lam, consts, w))
    return tuple(outs)
```

```python
import functools
import math

import jax
import jax.numpy as jnp
import numpy as np
from jax import lax
from jax.experimental import pallas as pl
from jax.experimental.pallas import tpu as pltpu

F32 = jnp.float32
BF16 = jnp.bfloat16
I32 = jnp.int32

HEAD_DIM = 64
N_HEADS_A = 4
WIDTH_A = N_HEADS_A * 2 * HEAD_DIM
N_HEADS_B = 8
N_KV_B = 2
GROUP_B = N_HEADS_B // N_KV_B
WIDTH_B = N_HEADS_B * HEAD_DIM
KV_B = N_KV_B * HEAD_DIM
GRID_W = 64
ROPE_THETA = 10000.0
N_EXPERTS = 16
CAPACITY_FACTOR = 2
DEPTH = 1
ALPHA = (2.0 * DEPTH) ** 0.25
LN_EPS = 1e-5
RMS_EPS = 1e-6
LAM_INIT = 0.8 - 0.6 * math.exp(-0.3 * 0)

LOG2E = 1.4426950408889634
SOFTMAX_CHUNK = 32
N_BIAS_COLS = 12
ATTN_SUBTILES = 4

LANES = 128
SLOT_BLOCK = 256
VMEM_LIMIT = 56 * 1024 * 1024

_NT = (((1,), (1,)), ((), ()))


def _split_bf16(a):
    hi = a.astype(BF16)
    lo = (a - hi.astype(F32)).astype(BF16)
    return hi, lo


def _dot3(a, b, dims=(((1,), (0,)), ((), ()))):
    ah, al = _split_bf16(a)
    bh, bl = _split_bf16(b)
    d = functools.partial(lax.dot_general, dimension_numbers=dims, preferred_element_type=F32)
    return d(ah, bh) + (d(ah, bl) + d(al, bh))


def _layer_norm(x):
    mu = jnp.mean(x, axis=-1, keepdims=True)
    xc = x - mu
    var = jnp.mean(xc * xc, axis=-1, keepdims=True)
    return xc * lax.rsqrt(var + LN_EPS)


def _lanes(x, n):
    if n == LANES:
        return x
    if n < LANES:
        return x[:, :n]
    return jnp.concatenate([x] * (n // LANES), axis=1)


def _mod_kernel(c_ref, w_ref, b_ref, lq1_ref, lk1_ref, lq2_ref, lk2_ref, mod_ref, lam_ref):
    c = c_ref[...]
    a = c * jax.nn.sigmoid(c)
    mod_ref[...] = _dot3(a, w_ref[...]) + b_ref[...]
    s1 = jnp.sum(lq1_ref[...] * lk1_ref[...], axis=-1, keepdims=True)
    s2 = jnp.sum(lq2_ref[...] * lk2_ref[...], axis=-1, keepdims=True)
    lam = jnp.exp(s1) - jnp.exp(s2) + LAM_INIT
    lam_ref[...] = jnp.broadcast_to(lam, lam_ref.shape)


def _mod_call(c, w_ada, b_ada, lq1, lk1, lq2, lk2):
    bt, d = c.shape
    n_chunks = w_ada.shape[1] // d
    vec = pl.BlockSpec((1, HEAD_DIM), lambda j: (0, 0))
    return pl.pallas_call(
        _mod_kernel,
        grid=(n_chunks,),
        in_specs=[
            pl.BlockSpec((bt, d), lambda j: (0, 0)),
            pl.BlockSpec((d, d), lambda j: (0, j)),
            pl.BlockSpec((1, d), lambda j: (0, j)),
            vec, vec, vec, vec,
        ],
        out_specs=[
            pl.BlockSpec((bt, d), lambda j: (0, j)),
            pl.BlockSpec((1, LANES), lambda j: (0, 0)),
        ],
        out_shape=[
            jax.ShapeDtypeStruct((bt, n_chunks * d), F32),
            jax.ShapeDtypeStruct((1, LANES), F32),
        ],
        compiler_params=pltpu.CompilerParams(vmem_limit_bytes=VMEM_LIMIT),
        name="mod",
    )(c, w_ada, b_ada, lq1, lk1, lq2, lk2)


def _head_rms(x, bd, g):
    sq = x * x
    hi, lo = _split_bf16(sq)
    ss = jnp.dot(hi, bd, preferred_element_type=F32) + jnp.dot(lo, bd, preferred_element_type=F32)
    return x * lax.rsqrt(ss * (1.0 / HEAD_DIM) + RMS_EPS) * g


def _rope(x, cos, sin_lo, sin_hi):
    w = x.shape[1]
    nxt = pltpu.roll(x, w - 16, axis=1)
    prv = pltpu.roll(x, 16, axis=1)
    return x * cos + nxt * sin_lo + prv * sin_hi


def _proj_kernel(x_ref, mod_ref, w_ref, gq_ref, gk_ref, cos_ref, slo_ref, shi_ref, bd_ref, cq_ref, ck_ref,
                 qa_ref, ka_ref, va_ref, qb_ref, kb_ref, vb_ref):
    x = x_ref[0]
    h = _layer_norm(x) * (1.0 + mod_ref[0, 1:2, :]) + mod_ref[0, 0:1, :]
    hb = h.astype(BF16)

    def proj(lo, width):
        return jnp.dot(hb, w_ref[:, lo:lo + width], preferred_element_type=F32)

    scale = HEAD_DIM ** -0.5 * LOG2E
    qa = (proj(0, WIDTH_A) * scale).astype(BF16)
    ka = proj(WIDTH_A, WIDTH_A).astype(BF16)
    hw = 2 * HEAD_DIM
    tm = x.shape[0]
    pos = pl.program_id(1) * tm + lax.broadcasted_iota(I32, (tm, LANES), 0)
    pos_hi = ((pos // LANES) * LANES).astype(F32)
    pos_lo = (pos % LANES).astype(F32)
    lane = lax.broadcasted_iota(I32, (tm, LANES), 1)
    zero = jnp.zeros((tm, LANES), F32)
    for hd in range(N_HEADS_A):
        lo = hd * (hw + LANES)
        bias_q = jnp.where(lane < 6, cq_ref[hd:hd + 1, :],
                           jnp.where(lane < 9, pos_hi, jnp.where(lane < N_BIAS_COLS, pos_lo, zero)))
        bias_k = jnp.where(lane < 3, -pos_hi,
                           jnp.where(lane < 6, -pos_lo, jnp.where(lane < N_BIAS_COLS, ck_ref[hd:hd + 1, :], zero)))
        qa_ref[0, :, lo:lo + hw] = qa[:, hd * hw:(hd + 1) * hw]
        qa_ref[0, :, lo + hw:lo + hw + LANES] = bias_q.astype(BF16)
        ka_ref[0, :, lo:lo + hw] = ka[:, hd * hw:(hd + 1) * hw]
        ka_ref[0, :, lo + hw:lo + hw + LANES] = bias_k.astype(BF16)
    va_ref[0] = proj(2 * WIDTH_A, WIDTH_A).astype(BF16)

    cos, slo, shi = cos_ref[...], slo_ref[...], shi_ref[...]
    bd = bd_ref[...]
    qb = _head_rms(proj(3 * WIDTH_A, WIDTH_B), bd, gq_ref[...])
    qb = _rope(qb, _lanes(cos, WIDTH_B), _lanes(slo, WIDTH_B), _lanes(shi, WIDTH_B)) * scale
    for hd in range(N_HEADS_B):
        qb_ref[0, hd] = qb[:, hd * HEAD_DIM:(hd + 1) * HEAD_DIM].astype(BF16)
    kb = _head_rms(proj(3 * WIDTH_A + WIDTH_B, KV_B), bd[:KV_B, :KV_B], gk_ref[...])
    kb = _rope(kb, cos, slo, shi)
    vb = proj(3 * WIDTH_A + WIDTH_B + KV_B, KV_B)
    for hd in range(N_KV_B):
        kb_ref[0, hd] = kb[:, hd * HEAD_DIM:(hd + 1) * HEAD_DIM].astype(BF16)
        vb_ref[0, hd] = vb[:, hd * HEAD_DIM:(hd + 1) * HEAD_DIM].astype(BF16)


def _proj_call(x, mod, w_in, gq, gk, cos, slo, shi, bd, cq, ck, tm):
    b, s, d = x.shape
    nt = s // tm
    wide_a = N_HEADS_A * (2 * HEAD_DIM + LANES)
    bias_tab = pl.BlockSpec((N_HEADS_A, LANES), lambda bi, ti: (0, 0))
    const = lambda shape: pl.BlockSpec(shape, lambda bi, ti: (0,) * len(shape))
    tok = lambda width: pl.BlockSpec((1, tm, width), lambda bi, ti: (bi, ti, 0))
    heads = lambda n: pl.BlockSpec((1, n, tm, HEAD_DIM), lambda bi, ti: (bi, 0, ti, 0))
    tab = pl.BlockSpec((tm, LANES), lambda bi, ti: (ti, 0))
    return pl.pallas_call(
        _proj_kernel,
        grid=(b, nt),
        in_specs=[
            tok(d),
            pl.BlockSpec((1, 6, d), lambda bi, ti: (bi, 0, 0)),
            const(w_in.shape),
            const(gq.shape), const(gk.shape),
            tab, tab, tab,
            const(bd.shape),
            bias_tab, bias_tab,
        ],
        out_specs=[tok(wide_a), tok(wide_a), tok(WIDTH_A),
                   heads(N_HEADS_B), heads(N_KV_B), heads(N_KV_B)],
        out_shape=[
            jax.ShapeDtypeStruct((b, s, wide_a), BF16),
            jax.ShapeDtypeStruct((b, s, wide_a), BF16),
            jax.ShapeDtypeStruct((b, s, WIDTH_A), BF16),
            jax.ShapeDtypeStruct((b, N_HEADS_B, s, HEAD_DIM), BF16),
            jax.ShapeDtypeStruct((b, N_KV_B, s, HEAD_DIM), BF16),
            jax.ShapeDtypeStruct((b, N_KV_B, s, HEAD_DIM), BF16),
        ],
        compiler_params=pltpu.CompilerParams(
            dimension_semantics=("parallel", "parallel"), vmem_limit_bytes=VMEM_LIMIT),
        name="proj",
    )(x, mod, w_in, gq, gk, cos, slo, shi, bd, cq, ck)


def _softmax_update(s_ref, p_ref, m_ref, l_ref, acc_ref, bias_ref=None):
    rows, tk = s_ref.shape
    width = acc_ref.shape[-1]
    for r0 in range(0, rows, SOFTMAX_CHUNK):
        sl = slice(r0, r0 + SOFTMAX_CHUNK)
        s = s_ref[sl, :]
        if bias_ref is not None:
            s = s + bias_ref[sl, :]
            s_ref[sl, :] = s
        m_prev = m_ref[sl, :]
        m_new = jnp.maximum(m_prev, jnp.max(s, axis=1, keepdims=True))
        alpha = jnp.exp2(m_prev - m_new)
        m_ref[sl, :] = m_new
        l_ref[sl, :] = alpha * l_ref[sl, :]
        acc_ref[sl, :] = _lanes(alpha, width) * acc_ref[sl, :]
    for r0 in range(0, rows, SOFTMAX_CHUNK):
        sl = slice(r0, r0 + SOFTMAX_CHUNK)
        p = jnp.exp2(s_ref[sl, :] - _lanes(m_ref[sl, :], tk))
        part = p[:, :LANES]
        for j in range(1, tk // LANES):
            part = part + p[:, j * LANES:(j + 1) * LANES]
        l_ref[sl, :] += part
        p_ref[sl, :] = p.astype(BF16)


def _diff_attn_kernel(lam_ref, g_ref, dbias_ref, q_ref, *refs, n_sub, n_kt):
    k_refs, v_refs = refs[:n_sub], refs[n_sub:2 * n_sub]
    o_ref, qv_sc, s_sc, p_sc, m_sc, l_sc, acc_sc = refs[2 * n_sub:]
    qi = pl.program_id(2)
    ki = pl.program_id(3)
    hw = 2 * HEAD_DIM

    @pl.when(ki == 0)
    def _():
        m_sc[...] = jnp.full(m_sc.shape, -jnp.inf, F32)
        l_sc[...] = jnp.zeros(l_sc.shape, F32)
        acc_sc[...] = jnp.zeros(acc_sc.shape, F32)
        q = q_ref[0]
        zero = jnp.zeros_like(q)
        lane = lax.broadcasted_iota(I32, q.shape, 1)
        is_bias = lane >= hw
        for c in range(2):
            qc = jnp.where(is_bias | ((lane >= c * HEAD_DIM) & (lane < (c + 1) * HEAD_DIM)), q, zero)
            qv_sc[0, c] = qc
            qv_sc[1, c] = jnp.where(is_bias, -qc, qc)
            qv_sc[2, c] = jnp.where(is_bias, zero, qc)

    def body(ends_on_diagonal):
        for j in range(n_sub):
            diagonal = ends_on_diagonal and j == n_sub - 1
            kt = (qi + 1 + ki * n_sub + j) % n_kt
            version = 2 if diagonal else jnp.where(kt > qi, 0, 1)
            for c in range(2):
                s_sc[j, c] = lax.dot_general(qv_sc[version, c], k_refs[j][0], _NT,
                                             preferred_element_type=F32)
        for j in range(n_sub):
            diagonal = ends_on_diagonal and j == n_sub - 1
            for c in range(2):
                _softmax_update(s_sc.at[j, c], p_sc.at[j, c], m_sc.at[c], l_sc.at[c], acc_sc.at[c],
                                dbias_ref.at[0] if diagonal else None)
                acc_sc[c] += jnp.dot(p_sc[j, c], v_refs[j][0], preferred_element_type=F32)

    last = pl.num_programs(3) - 1

    @pl.when(ki != last)
    def _():
        body(False)

    @pl.when(ki == last)
    def _():
        body(True)

    @pl.when(ki == last)
    def _():
        l0 = jnp.sum(l_sc[0], axis=1, keepdims=True)
        l1 = jnp.sum(l_sc[1], axis=1, keepdims=True)
        o = acc_sc[0] / l0 - lam_ref[...] * (acc_sc[1] / l1)
        ms = jnp.mean(o * o, axis=-1, keepdims=True)
        o = o * lax.rsqrt(ms + RMS_EPS) * g_ref[...] * (1.0 - LAM_INIT)
        o_ref[0] = o.astype(o_ref.dtype)


def _diff_attn_call(qa, ka, va, dbias, lam, g, t, n_sub):
    b, s, _ = va.shape
    hw = 2 * HEAD_DIM
    qw = hw + LANES
    n_kt = s // t

    def kv_spec(width, j):
        return pl.BlockSpec(
            (1, t, width), lambda bi, h, qi, ki: (bi, (qi + 1 + ki * n_sub + j) % n_kt, h))

    return pl.pallas_call(
        functools.partial(_diff_attn_kernel, n_sub=n_sub, n_kt=n_kt),
        grid=(b, N_HEADS_A, n_kt, n_kt // n_sub),
        in_specs=[
            pl.BlockSpec((1, LANES), lambda bi, h, qi, ki: (0, 0)),
            pl.BlockSpec((1, hw), lambda bi, h, qi, ki: (0, 0)),
            pl.BlockSpec((1, t, t), lambda bi, h, qi, ki: (h, 0, 0)),
            pl.BlockSpec((1, t, qw), lambda bi, h, qi, ki: (bi, qi, h)),
            *[kv_spec(qw, j) for j in range(n_sub)],
            *[kv_spec(hw, j) for j in range(n_sub)],
        ],
        out_specs=pl.BlockSpec((1, t, hw), lambda bi, h, qi, ki: (bi, qi, h)),
        out_shape=jax.ShapeDtypeStruct((b, s, WIDTH_A), BF16),
        scratch_shapes=[
            pltpu.VMEM((3, 2, t, qw), BF16),
            pltpu.VMEM((n_sub, 2, t, t), F32),
            pltpu.VMEM((n_sub, 2, t, t), BF16),
            pltpu.VMEM((2, t, LANES), F32),
            pltpu.VMEM((2, t, LANES), F32),
            pltpu.VMEM((2, t, hw), F32),
        ],
        compiler_params=pltpu.CompilerParams(
            dimension_semantics=("parallel", "parallel", "parallel", "arbitrary"),
            vmem_limit_bytes=VMEM_LIMIT),
        name="diff_attn",
    )(lam, g, dbias, qa, *([ka] * n_sub), *([va] * n_sub))


def _gqa_kernel(q_ref, k_ref, v_ref, o_ref, s_sc, p_sc, m_sc, l_sc, acc_sc, *, tq, tk):
    ki = pl.program_id(3)
    rows = GROUP_B * tq

    @pl.when(ki == 0)
    def _():
        m_sc[...] = jnp.full(m_sc.shape, -jnp.inf, F32)
        l_sc[...] = jnp.zeros(l_sc.shape, F32)
        acc_sc[...] = jnp.zeros(acc_sc.shape, F32)

    n_sub = k_ref.shape[2] // tk
    q = q_ref[0].reshape(rows, HEAD_DIM)
    for j in range(n_sub):
        s_sc[j] = lax.dot_general(q, k_ref[0, 0, j * tk:(j + 1) * tk, :], _NT,
                                  preferred_element_type=F32)
    for j in range(n_sub):
        _softmax_update(s_sc.at[j], p_sc.at[j], m_sc, l_sc, acc_sc)
        acc_sc[...] += jnp.dot(p_sc[j], v_ref[0, 0, j * tk:(j + 1) * tk, :],
                               preferred_element_type=F32)

    @pl.when(ki == pl.num_programs(3) - 1)
    def _():
        o = acc_sc[...] / jnp.sum(l_sc[...], axis=1, keepdims=True)
        o_ref[0] = o.reshape(GROUP_B, tq, HEAD_DIM).astype(o_ref.dtype)


def _gqa_call(qb, kb, vb, tq, tk, n_sub):
    b, _, s, _ = qb.shape
    rows = GROUP_B * tq
    kv = n_sub * tk
    return pl.pallas_call(
        functools.partial(_gqa_kernel, tq=tq, tk=tk),
        grid=(b, N_KV_B, s // tq, s // kv),
        in_specs=[
            pl.BlockSpec((1, GROUP_B, tq, HEAD_DIM), lambda bi, n, qi, ki: (bi, n, qi, 0)),
            pl.BlockSpec((1, 1, kv, HEAD_DIM), lambda bi, n, qi, ki: (bi, n, ki, 0)),
            pl.BlockSpec((1, 1, kv, HEAD_DIM), lambda bi, n, qi, ki: (bi, n, ki, 0)),
        ],
        out_specs=pl.BlockSpec((1, GROUP_B, tq, HEAD_DIM), lambda bi, n, qi, ki: (bi, n, qi, 0)),
        out_shape=jax.ShapeDtypeStruct((b, N_HEADS_B, s, HEAD_DIM), BF16),
        scratch_shapes=[
            pltpu.VMEM((n_sub, rows, tk), F32),
            pltpu.VMEM((n_sub, rows, tk), BF16),
            pltpu.VMEM((rows, LANES), F32),
            pltpu.VMEM((rows, LANES), F32),
            pltpu.VMEM((rows, HEAD_DIM), F32),
        ],
        compiler_params=pltpu.CompilerParams(
            dimension_semantics=("parallel", "parallel", "parallel", "arbitrary"),
            vmem_limit_bytes=VMEM_LIMIT),
        name="gqa_attn",
    )(qb, kb, vb)


def _post_kernel(oa_ref, ob_ref, x_ref, mod_ref, w_ref, g_ref, b_ref, wrt_ref,
                 x1_ref, h2_ref, aff_ref):
    o = jnp.dot(oa_ref[0], w_ref[:WIDTH_A, :], preferred_element_type=F32)
    for hd in range(N_HEADS_B):
        lo = WIDTH_A + hd * HEAD_DIM
        o = o + jnp.dot(ob_ref[0, hd], w_ref[lo:lo + HEAD_DIM, :], preferred_element_type=F32)
    r = ALPHA * x_ref[0] + mod_ref[0, 2:3, :] * o
    x1 = _layer_norm(r) * g_ref[...] + b_ref[...]
    x1_ref[0] = x1
    h2 = _layer_norm(x1) * (1.0 + mod_ref[0, 4:5, :]) + mod_ref[0, 3:4, :]
    h2_ref[...] = h2.T.astype(BF16)
    logits = _dot3(wrt_ref[...], h2, _NT)
    z = jnp.exp(logits - jnp.max(logits, axis=0, keepdims=True))
    aff_ref[...] = z / jnp.sum(z, axis=0, keepdims=True)


def _post_call(oa, ob, x, mod, w_out, ln_g, ln_b, w_router_t, tm):
    b, s, d = x.shape
    nt = s // tm
    const = lambda shape: pl.BlockSpec(shape, lambda bi, ti: (0,) * len(shape))
    tok = lambda width: pl.BlockSpec((1, tm, width), lambda bi, ti: (bi, ti, 0))
    return pl.pallas_call(
        _post_kernel,
        grid=(b, nt),
        in_specs=[
            tok(WIDTH_A),
            pl.BlockSpec((1, N_HEADS_B, tm, HEAD_DIM), lambda bi, ti: (bi, 0, ti, 0)),
            tok(d),
            pl.BlockSpec((1, 6, d), lambda bi, ti: (bi, 0, 0)),
            const(w_out.shape), const(ln_g.shape), const(ln_b.shape), const(w_router_t.shape),
        ],
        out_specs=[tok(d),
                   pl.BlockSpec((d, tm), lambda bi, ti: (0, bi * nt + ti)),
                   pl.BlockSpec((N_EXPERTS, tm), lambda bi, ti: (0, bi * nt + ti))],
        out_shape=[
            jax.ShapeDtypeStruct((b, s, d), F32),
            jax.ShapeDtypeStruct((d, b * s), BF16),
            jax.ShapeDtypeStruct((N_EXPERTS, b * s), F32),
        ],
        compiler_params=pltpu.CompilerParams(
            dimension_semantics=("parallel", "parallel"), vmem_limit_bytes=VMEM_LIMIT),
        name="post",
    )(oa, ob, x, mod, w_out, ln_g, ln_b, w_router_t)


def _route_kernel(aff_ref, spos_ref, cnt_ref, *, cap, idx_bits):
    aff = aff_ref[...]
    n_e, n_r, _ = aff.shape
    bits = pltpu.bitcast(aff, I32)
    idx = (lax.broadcasted_iota(I32, aff.shape, 1) * LANES
           + lax.broadcasted_iota(I32, aff.shape, 2))

    def count(pred):
        part = jnp.sum(jnp.where(pred, 1.0, 0.0), axis=1, keepdims=True)
        return jnp.sum(part, axis=2, keepdims=True)

    def value_step(t, prefix):
        cand = prefix | (1 << (30 - t))
        return jnp.where(count(bits >= cand) >= cap, cand, prefix)

    thr = lax.fori_loop(0, 31, value_step, jnp.zeros((n_e, 1, 1), I32))
    above = bits > thr
    tie = bits == thr
    need = cap - count(above)

    def index_step(t, prefix):
        cand = prefix | (1 << (idx_bits - 1 - t))
        return jnp.where(count(tie & (idx < cand)) < need, cand, prefix)

    last = lax.fori_loop(0, idx_bits, index_step, jnp.zeros((n_e, 1, 1), I32))
    sel = above | (tie & (idx <= last))

    upper = (lax.broadcasted_iota(I32, (LANES, LANES), 0)
             <= lax.broadcasted_iota(I32, (LANES, LANES), 1)).astype(BF16)
    strict_lower = (lax.broadcasted_iota(I32, (n_r, n_r), 1)
                    < lax.broadcasted_iota(I32, (n_r, n_r), 0)).astype(BF16)
    for e in range(n_e):
        sel_e = sel[e]
        m = sel_e.astype(BF16)
        incl = jnp.dot(m, upper, preferred_element_type=F32)
        row_tot = jnp.broadcast_to(incl[:, LANES - 1:], incl.shape).astype(BF16)
        offs = jnp.dot(strict_lower, row_tot, preferred_element_type=F32)
        cnt = (incl + offs).astype(I32)
        cnt_ref[e] = cnt
        spos_ref[e] = jnp.where(sel_e, cnt - 1, -1)


def _route_call(aff3, cap):
    n_e, n_r, _ = aff3.shape
    idx_bits = max(1, (n_r * LANES - 1).bit_length())
    return pl.pallas_call(
        functools.partial(_route_kernel, cap=cap, idx_bits=idx_bits),
        out_shape=[jax.ShapeDtypeStruct(aff3.shape, I32), jax.ShapeDtypeStruct(aff3.shape, I32)],
        compiler_params=pltpu.CompilerParams(vmem_limit_bytes=VMEM_LIMIT),
        name="route",
    )(aff3)


def _ffn_kernel(tend_ref, spos_ref, g_ref, x_ref, wg_ref, wu_ref, wd_ref, ye_ref,
                stage_sc, gate_sc):
    e = pl.program_id(0)
    i = pl.program_id(1)
    prev_end = jnp.where(i > 0, tend_ref[e, jnp.maximum(i - 1, 0)], 0)
    base = (prev_end // SLOT_BLOCK) * SLOT_BLOCK
    fill = tend_ref[e, i] - base

    @pl.when(i == 0)
    def _():
        stage_sc[...] = jnp.zeros(stage_sc.shape, F32)
        gate_sc[...] = jnp.zeros(gate_sc.shape, F32)

    tt = x_ref.shape[1]
    rel = spos_ref[0, 0] - base
    slot = lax.broadcasted_iota(I32, (SLOT_BLOCK, tt), 0)

    def place(lo, rel_lo):
        hit = slot == rel_lo
        onehot = jnp.where(hit, 1.0, 0.0).astype(BF16)
        cols = slice(lo, lo + SLOT_BLOCK)
        stage_sc[:, cols] += lax.dot_general(x_ref[...], onehot, _NT, preferred_element_type=F32)
        gate_sc[cols, :] += jnp.sum(jnp.where(hit, g_ref[0, 0], 0.0), axis=1, keepdims=True)

    place(0, rel)

    @pl.when(fill > SLOT_BLOCK)
    def _():
        place(SLOT_BLOCK, rel - SLOT_BLOCK)

    @pl.when(fill >= SLOT_BLOCK)
    def _():
        xb = stage_sc[:, :SLOT_BLOCK].T.astype(BF16)
        gate = jnp.dot(xb, wg_ref[0], preferred_element_type=F32)
        up = jnp.dot(xb, wu_ref[0], preferred_element_type=F32)
        hid = (gate * jax.nn.sigmoid(gate) * up).astype(BF16)
        y = jnp.dot(hid, wd_ref[0], preferred_element_type=F32) * gate_sc[:SLOT_BLOCK, :1]
        ye_ref[0, 0] = y.T.astype(ye_ref.dtype)
        stage_sc[:, :SLOT_BLOCK] = stage_sc[:, SLOT_BLOCK:]
        stage_sc[:, SLOT_BLOCK:] = jnp.zeros((stage_sc.shape[0], SLOT_BLOCK), F32)
        gate_sc[:SLOT_BLOCK, :] = gate_sc[SLOT_BLOCK:, :]
        gate_sc[SLOT_BLOCK:, :] = jnp.zeros((SLOT_BLOCK, LANES), F32)


def _ffn_call(tile_end, spos4, aff4, h2t, wg, wu, wd, cap, tt):
    d, n = h2t.shape
    n_t = n // tt
    ff = wg.shape[2]
    n_blk = cap // SLOT_BLOCK

    def out_map(e, i, tend):
        prev_end = jnp.where(i > 0, tend[e, jnp.maximum(i - 1, 0)], 0)
        return (e, jnp.clip(prev_end // SLOT_BLOCK, 0, n_blk - 1), 0, 0)

    grid_spec = pltpu.PrefetchScalarGridSpec(
        num_scalar_prefetch=1,
        grid=(N_EXPERTS, n_t),
        in_specs=[
            pl.BlockSpec((1, 1, 1, tt), lambda e, i, tend: (e, i, 0, 0)),
            pl.BlockSpec((1, 1, 1, tt), lambda e, i, tend: (e, i, 0, 0)),
            pl.BlockSpec((d, tt), lambda e, i, tend: (0, i)),
            pl.BlockSpec((1, d, ff), lambda e, i, tend: (e, 0, 0)),
            pl.BlockSpec((1, d, ff), lambda e, i, tend: (e, 0, 0)),
            pl.BlockSpec((1, ff, d), lambda e, i, tend: (e, 0, 0)),
        ],
        out_specs=pl.BlockSpec((1, 1, d, SLOT_BLOCK), out_map),
        scratch_shapes=[
            pltpu.VMEM((d, 2 * SLOT_BLOCK), F32),
            pltpu.VMEM((2 * SLOT_BLOCK, LANES), F32),
        ],
    )
    return pl.pallas_call(
        _ffn_kernel,
        grid_spec=grid_spec,
        out_shape=jax.ShapeDtypeStruct((N_EXPERTS, n_blk, d, SLOT_BLOCK), BF16),
        compiler_params=pltpu.CompilerParams(
            dimension_semantics=("arbitrary", "arbitrary"), vmem_limit_bytes=VMEM_LIMIT),
        name="ffn",
    )(tile_end, spos4, aff4, h2t, wg, wu, wd)


def _slot_span(tend, e, i):
    lo = jnp.where(i > 0, tend[e, jnp.maximum(i - 1, 0)], 0)
    return lo, tend[e, i]


def _combine_kernel(tend_ref, spos_ref, x1_ref, mod_ref, g_ref, b_ref, *refs):
    first = refs[:N_EXPERTS]
    second = refs[N_EXPERTS:2 * N_EXPERTS]
    o_ref, acc_sc = refs[2 * N_EXPERTS:]
    i = pl.program_id(0)
    tt = x1_ref.shape[0]
    slot = lax.broadcasted_iota(I32, (SLOT_BLOCK, tt), 0)

    def onehot(e, block_offset):
        lo, _ = _slot_span(tend_ref, e, i)
        rel = spos_ref[e, 0] - (lo // SLOT_BLOCK + block_offset) * SLOT_BLOCK
        return jnp.where(slot == rel, 1.0, 0.0).astype(BF16)

    acc_sc[...] = jnp.dot(
        jnp.concatenate([first[e][0, 0] for e in range(N_EXPERTS)], axis=1),
        jnp.concatenate([onehot(e, 0) for e in range(N_EXPERTS)], axis=0),
        preferred_element_type=F32)

    for e in range(N_EXPERTS):
        lo, hi = _slot_span(tend_ref, e, i)

        @pl.when(hi > (lo // SLOT_BLOCK + 1) * SLOT_BLOCK)
        def _(e=e):
            acc_sc[...] += jnp.dot(second[e][0, 0], onehot(e, 1), preferred_element_type=F32)

    r = ALPHA * x1_ref[...] + mod_ref[0, 5:6, :] * acc_sc[...].T
    o_ref[...] = _layer_norm(r) * g_ref[...] + b_ref[...]


def _combine_call(tile_end, spos4, ye_t, x1, mod, ln_g, ln_b, tt, tokens_per_batch):
    n, d = x1.shape
    n_t = n // tt
    n_blk = ye_t.shape[1]
    tiles_per_batch = tokens_per_batch // tt

    def first_spec(e):
        def index(i, tend):
            lo, _ = _slot_span(tend, e, i)
            return (e, jnp.clip(lo // SLOT_BLOCK, 0, n_blk - 1), 0, 0)
        return pl.BlockSpec((1, 1, d, SLOT_BLOCK), index)

    def second_spec(e):
        def index(i, tend):
            lo, hi = _slot_span(tend, e, i)
            nxt = lo // SLOT_BLOCK + 1
            spill = (hi > nxt * SLOT_BLOCK) & (nxt < n_blk)
            return (jnp.where(spill, e, 0), jnp.where(spill, jnp.clip(nxt, 0, n_blk - 1), 0), 0, 0)
        return pl.BlockSpec((1, 1, d, SLOT_BLOCK), index)

    grid_spec = pltpu.PrefetchScalarGridSpec(
        num_scalar_prefetch=1,
        grid=(n_t,),
        in_specs=[
            pl.BlockSpec((N_EXPERTS, 1, 1, tt), lambda i, tend: (0, i, 0, 0)),
            pl.BlockSpec((tt, d), lambda i, tend: (i, 0)),
            pl.BlockSpec((1, 6, d), lambda i, tend: (i // tiles_per_batch, 0, 0)),
            pl.BlockSpec((1, d), lambda i, tend: (0, 0)),
            pl.BlockSpec((1, d), lambda i, tend: (0, 0)),
            *[first_spec(e) for e in range(N_EXPERTS)],
            *[second_spec(e) for e in range(N_EXPERTS)],
        ],
        out_specs=pl.BlockSpec((tt, d), lambda i, tend: (i, 0)),
        scratch_shapes=[pltpu.VMEM((d, tt), F32)],
    )
    return pl.pallas_call(
        _combine_kernel,
        grid_spec=grid_spec,
        out_shape=jax.ShapeDtypeStruct((n, d), F32),
        compiler_params=pltpu.CompilerParams(
            dimension_semantics=("parallel",), vmem_limit_bytes=VMEM_LIMIT),
        name="combine",
    )(tile_end, spos4, x1, mod, ln_g, ln_b, *([ye_t] * (2 * N_EXPERTS)))


def _rope_tables(s):
    pos = jnp.arange(s)
    half = HEAD_DIM // 2
    inv_freq = ROPE_THETA ** (-jnp.arange(0, half, 2, dtype=F32) / half)
    ang_r = (pos // GRID_W).astype(F32)[:, None] * inv_freq[None, :]
    ang_c = (pos % GRID_W).astype(F32)[:, None] * inv_freq[None, :]
    zero = jnp.zeros_like(ang_r)
    cos = jnp.concatenate([jnp.cos(ang_r)] * 2 + [jnp.cos(ang_c)] * 2, axis=1)
    slo = jnp.concatenate([-jnp.sin(ang_r), zero, -jnp.sin(ang_c), zero], axis=1)
    shi = jnp.concatenate([zero, jnp.sin(ang_r), zero, jnp.sin(ang_c)], axis=1)
    rep = LANES // HEAD_DIM
    return tuple(jnp.concatenate([t] * rep, axis=1) for t in (cos, slo, shi))


def _bf16_terms(value, n=3):
    terms, rem = [], np.float32(value)
    for _ in range(n):
        part = np.float32(np.asarray(rem, dtype=jnp.bfloat16))
        terms.append(float(part))
        rem = np.float32(rem - part)
    return terms


def _alibi_constants(t):
    cq = np.zeros((N_HEADS_A, LANES), np.float32)
    ck = np.zeros((N_HEADS_A, LANES), np.float32)
    d = jnp.arange(t)
    dist = jnp.abs(d[:, None] - d[None, :]).astype(F32)
    db = []
    for h in range(N_HEADS_A):
        terms = _bf16_terms(2.0 ** (-8.0 * (h + 1) / N_HEADS_A) * LOG2E)
        cq[h, 0:6] = terms + terms
        ck[h, 6:12] = terms + terms
        db.append(-np.float32(sum(terms)) * dist)
    return jnp.asarray(cq), jnp.asarray(ck), jnp.stack(db)


def _tile(dim, target):
    t = min(dim, target)
    assert dim % t == 0, (dim, t)
    return t


def _trunk(x, mod, lam, consts, w):
    b, s, d = x.shape
    n = b * s
    cap = CAPACITY_FACTOR * n // N_EXPERTS
    assert cap % SLOT_BLOCK == 0 and n % LANES == 0
    tm = _tile(s, 512)
    tt = _tile(n, SLOT_BLOCK)

    ta = _tile(s, 512)
    cq, ck, dbias = _alibi_constants(ta)
    qa, ka, va, qb, kb, vb = _proj_call(
        x, mod, w["w_in"], w["gq"], w["gk"], *consts["rope"], consts["bd"], cq, ck, tm)
    n_sub = min(ATTN_SUBTILES, s // ta)
    assert (s // ta) % n_sub == 0, (s, ta, n_sub)
    oa = _diff_attn_call(qa, ka, va, dbias, lam, w["subln_g"], ta, n_sub)
    ob = _gqa_call(qb, kb, vb, _tile(s, 256), ta, n_sub)
    x1, h2t, aff_t = _post_call(oa, ob, x, mod, w["w_out"], w["ln1_g"], w["ln1_b"],
                                w["w_router_t"], tm)

    spos, cnt = _route_call(aff_t.reshape(N_EXPERTS, n // LANES, LANES), cap)
    n_t = n // tt
    spos4 = spos.reshape(N_EXPERTS, n_t, 1, tt)
    incl_end = cnt.reshape(N_EXPERTS, n)[:, tt - 1::tt]

    ye_t = _ffn_call(incl_end, spos4, aff_t.reshape(N_EXPERTS, n_t, 1, tt), h2t,
                     w["w_gate"], w["w_up"], w["w_down"], cap, tt)
    out = _combine_call(incl_end, spos4, ye_t, x1.reshape(n, d), mod,
                        w["ln2_g"], w["ln2_b"], tt, s)
    return out.reshape(b, s, d)


def kernel(x_prompt, x_sample, c_prompt, c_sample, w_ada, b_ada, w_in, lam_q1, lam_k1, lam_q2,
           lam_k2, subln_g, q_norm_g, k_norm_g, w_out, ln1_g, ln1_b, w_router, w_gate, w_up,
           w_down, ln2_g, ln2_b):
    l = 0
    d = x_prompt.shape[-1]
    c_all = jnp.concatenate([c_prompt, c_sample], axis=0)
    mod, lam = _mod_call(c_all, w_ada[l], b_ada[l][None, :], lam_q1[l][None, :], lam_k1[l][None, :],
                         lam_q2[l][None, :], lam_k2[l][None, :])
    mod = mod.reshape(c_all.shape[0], 6, d)

    w = {
        "w_in": w_in[l].astype(BF16),
        "gq": jnp.tile(q_norm_g[l], N_HEADS_B)[None, :],
        "gk": jnp.tile(k_norm_g[l], N_KV_B)[None, :],
        "subln_g": subln_g[l][None, :],
        "w_out": w_out[l].astype(BF16),
        "ln1_g": ln1_g[l][None, :], "ln1_b": ln1_b[l][None, :],
        "w_router_t": w_router[l].T,
        "w_gate": w_gate[l].astype(BF16), "w_up": w_up[l].astype(BF16),
        "w_down": w_down[l].astype(BF16),
        "ln2_g": ln2_g[l][None, :], "ln2_b": ln2_b[l][None, :],
    }
    head_of = jnp.arange(WIDTH_B) // HEAD_DIM
    shared = {"bd": (head_of[:, None] == head_of[None, :]).astype(BF16)}

    outs = []
    nb = x_prompt.shape[0]
    for x, m in ((x_prompt, mod[:nb]), (x_sample, mod[nb:])):
        consts = dict(shared, rope=_rope_tables(x.shape[1]))
        outs.append(_trunk(x, m, lam, consts, w))
    return tuple(outs)
```

```python
import functools
import math

import jax
import jax.numpy as jnp
import numpy as np
from jax import lax
from jax.experimental import pallas as pl
from jax.experimental.pallas import tpu as pltpu

F32 = jnp.float32
BF16 = jnp.bfloat16
I32 = jnp.int32

HEAD_DIM = 64
N_HEADS_A = 4
WIDTH_A = N_HEADS_A * 2 * HEAD_DIM
N_HEADS_B = 8
N_KV_B = 2
GROUP_B = N_HEADS_B // N_KV_B
WIDTH_B = N_HEADS_B * HEAD_DIM
KV_B = N_KV_B * HEAD_DIM
GRID_W = 64
ROPE_THETA = 10000.0
N_EXPERTS = 16
CAPACITY_FACTOR = 2
DEPTH = 1
ALPHA = (2.0 * DEPTH) ** 0.25
LN_EPS = 1e-5
RMS_EPS = 1e-6
LAM_INIT = 0.8 - 0.6 * math.exp(-0.3 * 0)

LOG2E = 1.4426950408889634
SOFTMAX_CHUNK = 32
N_BIAS_COLS = 12
ATTN_SUBTILES = 8

LANES = 128
SLOT_BLOCK = 256
FFN_BLOCK = 512
VMEM_LIMIT = 56 * 1024 * 1024

_NT = (((1,), (1,)), ((), ()))


def _split_bf16(a):
    hi = a.astype(BF16)
    lo = (a - hi.astype(F32)).astype(BF16)
    return hi, lo


def _dot3(a, b, dims=(((1,), (0,)), ((), ()))):
    ah, al = _split_bf16(a)
    bh, bl = _split_bf16(b)
    d = functools.partial(lax.dot_general, dimension_numbers=dims, preferred_element_type=F32)
    return d(ah, bh) + (d(ah, bl) + d(al, bh))


def _layer_norm(x):
    mu = jnp.mean(x, axis=-1, keepdims=True)
    xc = x - mu
    var = jnp.mean(xc * xc, axis=-1, keepdims=True)
    return xc * lax.rsqrt(var + LN_EPS)


def _lanes(x, n):
    if n == LANES:
        return x
    if n < LANES:
        return x[:, :n]
    return jnp.concatenate([x] * (n // LANES), axis=1)


def _mod_kernel(c_ref, w_ref, b_ref, lq1_ref, lk1_ref, lq2_ref, lk2_ref, mod_ref, lam_ref):
    c = c_ref[...]
    a = c * jax.nn.sigmoid(c)
    mod_ref[...] = _dot3(a, w_ref[...]) + b_ref[...]
    s1 = jnp.sum(lq1_ref[...] * lk1_ref[...], axis=-1, keepdims=True)
    s2 = jnp.sum(lq2_ref[...] * lk2_ref[...], axis=-1, keepdims=True)
    lam = jnp.exp(s1) - jnp.exp(s2) + LAM_INIT
    lam_ref[...] = jnp.broadcast_to(lam, lam_ref.shape)


def _mod_call(c, w_ada, b_ada, lq1, lk1, lq2, lk2):
    bt, d = c.shape
    n_chunks = w_ada.shape[1] // d
    vec = pl.BlockSpec((1, HEAD_DIM), lambda j: (0, 0))
    return pl.pallas_call(
        _mod_kernel,
        grid=(n_chunks,),
        in_specs=[
            pl.BlockSpec((bt, d), lambda j: (0, 0)),
            pl.BlockSpec((d, d), lambda j: (0, j)),
            pl.BlockSpec((1, d), lambda j: (0, j)),
            vec, vec, vec, vec,
        ],
        out_specs=[
            pl.BlockSpec((bt, d), lambda j: (0, j)),
            pl.BlockSpec((1, LANES), lambda j: (0, 0)),
        ],
        out_shape=[
            jax.ShapeDtypeStruct((bt, n_chunks * d), F32),
            jax.ShapeDtypeStruct((1, LANES), F32),
        ],
        compiler_params=pltpu.CompilerParams(vmem_limit_bytes=VMEM_LIMIT),
        name="mod",
    )(c, w_ada, b_ada, lq1, lk1, lq2, lk2)


def _head_rms(x, bd, g):
    sq = x * x
    hi, lo = _split_bf16(sq)
    ss = jnp.dot(hi, bd, preferred_element_type=F32) + jnp.dot(lo, bd, preferred_element_type=F32)
    return x * lax.rsqrt(ss * (1.0 / HEAD_DIM) + RMS_EPS) * g


def _rope(x, cos, sin_lo, sin_hi):
    w = x.shape[1]
    nxt = pltpu.roll(x, w - 16, axis=1)
    prv = pltpu.roll(x, 16, axis=1)
    return x * cos + nxt * sin_lo + prv * sin_hi


def _proj_kernel(x_ref, mod_ref, w_ref, gq_ref, gk_ref, cos_ref, slo_ref, shi_ref, bd_ref, cq_ref, ck_ref,
                 qa_ref, ka_ref, va_ref, qb_ref, kb_ref, vb_ref):
    x = x_ref[0]
    h = _layer_norm(x) * (1.0 + mod_ref[0, 1:2, :]) + mod_ref[0, 0:1, :]
    hb = h.astype(BF16)

    def proj(lo, width):
        return jnp.dot(hb, w_ref[:, lo:lo + width], preferred_element_type=F32)

    scale = HEAD_DIM ** -0.5 * LOG2E
    qa = (proj(0, WIDTH_A) * scale).astype(BF16)
    ka = proj(WIDTH_A, WIDTH_A).astype(BF16)
    hw = 2 * HEAD_DIM
    tm = x.shape[0]
    pos = pl.program_id(1) * tm + lax.broadcasted_iota(I32, (tm, LANES), 0)
    pos_hi = ((pos // LANES) * LANES).astype(F32)
    pos_lo = (pos % LANES).astype(F32)
    lane = lax.broadcasted_iota(I32, (tm, LANES), 1)
    zero = jnp.zeros((tm, LANES), F32)
    for hd in range(N_HEADS_A):
        lo = hd * (hw + LANES)
        bias_q = jnp.where(lane < 6, cq_ref[hd:hd + 1, :],
                           jnp.where(lane < 9, pos_hi, jnp.where(lane < N_BIAS_COLS, pos_lo, zero)))
        bias_k = jnp.where(lane < 3, -pos_hi,
                           jnp.where(lane < 6, -pos_lo, jnp.where(lane < N_BIAS_COLS, ck_ref[hd:hd + 1, :], zero)))
        qa_ref[0, :, lo:lo + hw] = qa[:, hd * hw:(hd + 1) * hw]
        qa_ref[0, :, lo + hw:lo + hw + LANES] = bias_q.astype(BF16)
        ka_ref[0, :, lo:lo + hw] = ka[:, hd * hw:(hd + 1) * hw]
        ka_ref[0, :, lo + hw:lo + hw + LANES] = bias_k.astype(BF16)
    va_ref[0] = proj(2 * WIDTH_A, WIDTH_A).astype(BF16)

    cos, slo, shi = cos_ref[...], slo_ref[...], shi_ref[...]
    bd = bd_ref[...]
    qb = _head_rms(proj(3 * WIDTH_A, WIDTH_B), bd, gq_ref[...])
    qb = _rope(qb, _lanes(cos, WIDTH_B), _lanes(slo, WIDTH_B), _lanes(shi, WIDTH_B)) * scale
    for hd in range(N_HEADS_B):
        qb_ref[0, hd] = qb[:, hd * HEAD_DIM:(hd + 1) * HEAD_DIM].astype(BF16)
    kb = _head_rms(proj(3 * WIDTH_A + WIDTH_B, KV_B), bd[:KV_B, :KV_B], gk_ref[...])
    kb = _rope(kb, cos, slo, shi)
    vb = proj(3 * WIDTH_A + WIDTH_B + KV_B, KV_B)
    for hd in range(N_KV_B):
        kb_ref[0, hd] = kb[:, hd * HEAD_DIM:(hd + 1) * HEAD_DIM].astype(BF16)
        vb_ref[0, hd] = vb[:, hd * HEAD_DIM:(hd + 1) * HEAD_DIM].astype(BF16)


def _proj_call(x, mod, w_in, gq, gk, cos, slo, shi, bd, cq, ck, tm):
    b, s, d = x.shape
    nt = s // tm
    wide_a = N_HEADS_A * (2 * HEAD_DIM + LANES)
    bias_tab = pl.BlockSpec((N_HEADS_A, LANES), lambda bi, ti: (0, 0))
    const = lambda shape: pl.BlockSpec(shape, lambda bi, ti: (0,) * len(shape))
    tok = lambda width: pl.BlockSpec((1, tm, width), lambda bi, ti: (bi, ti, 0))
    heads = lambda n: pl.BlockSpec((1, n, tm, HEAD_DIM), lambda bi, ti: (bi, 0, ti, 0))
    tab = pl.BlockSpec((tm, LANES), lambda bi, ti: (ti, 0))
    return pl.pallas_call(
        _proj_kernel,
        grid=(b, nt),
        in_specs=[
            tok(d),
            pl.BlockSpec((1, 6, d), lambda bi, ti: (bi, 0, 0)),
            const(w_in.shape),
            const(gq.shape), const(gk.shape),
            tab, tab, tab,
            const(bd.shape),
            bias_tab, bias_tab,
        ],
        out_specs=[tok(wide_a), tok(wide_a), tok(WIDTH_A),
                   heads(N_HEADS_B), heads(N_KV_B), heads(N_KV_B)],
        out_shape=[
            jax.ShapeDtypeStruct((b, s, wide_a), BF16),
            jax.ShapeDtypeStruct((b, s, wide_a), BF16),
            jax.ShapeDtypeStruct((b, s, WIDTH_A), BF16),
            jax.ShapeDtypeStruct((b, N_HEADS_B, s, HEAD_DIM), BF16),
            jax.ShapeDtypeStruct((b, N_KV_B, s, HEAD_DIM), BF16),
            jax.ShapeDtypeStruct((b, N_KV_B, s, HEAD_DIM), BF16),
        ],
        compiler_params=pltpu.CompilerParams(
            dimension_semantics=("parallel", "parallel"), vmem_limit_bytes=VMEM_LIMIT),
        name="proj",
    )(x, mod, w_in, gq, gk, cos, slo, shi, bd, cq, ck)


def _softmax_update(s_ref, p_ref, m_ref, l_ref, acc_ref, bias_ref=None):
    rows, tk = s_ref.shape
    width = acc_ref.shape[-1]
    for r0 in range(0, rows, SOFTMAX_CHUNK):
        sl = slice(r0, r0 + SOFTMAX_CHUNK)
        s = s_ref[sl, :]
        if bias_ref is not None:
            s = s + bias_ref[sl, :]
            s_ref[sl, :] = s
        m_prev = m_ref[sl, :]
        m_new = jnp.maximum(m_prev, jnp.max(s, axis=1, keepdims=True))
        alpha = jnp.exp2(m_prev - m_new)
        m_ref[sl, :] = m_new
        l_ref[sl, :] = alpha * l_ref[sl, :]
        acc_ref[sl, :] = _lanes(alpha, width) * acc_ref[sl, :]
    for r0 in range(0, rows, SOFTMAX_CHUNK):
        sl = slice(r0, r0 + SOFTMAX_CHUNK)
        p = jnp.exp2(s_ref[sl, :] - _lanes(m_ref[sl, :], tk))
        part = p[:, :LANES]
        for j in range(1, tk // LANES):
            part = part + p[:, j * LANES:(j + 1) * LANES]
        l_ref[sl, :] += part
        p_ref[sl, :] = p.astype(BF16)


def _diff_attn_kernel(lam_ref, g_ref, dbias_ref, q_ref, *refs, n_sub, n_kt):
    k_refs, v_refs = refs[:n_sub], refs[n_sub:2 * n_sub]
    o_ref, qv_sc, s_sc, p_sc, m_sc, l_sc, acc_sc = refs[2 * n_sub:]
    qi = pl.program_id(2)
    ki = pl.program_id(3)
    hw = 2 * HEAD_DIM

    @pl.when(ki == 0)
    def _():
        m_sc[...] = jnp.full(m_sc.shape, -jnp.inf, F32)
        l_sc[...] = jnp.zeros(l_sc.shape, F32)
        acc_sc[...] = jnp.zeros(acc_sc.shape, F32)
        q = q_ref[0]
        zero = jnp.zeros_like(q)
        lane = lax.broadcasted_iota(I32, q.shape, 1)
        is_bias = lane >= hw
        for c in range(2):
            qc = jnp.where(is_bias | ((lane >= c * HEAD_DIM) & (lane < (c + 1) * HEAD_DIM)), q, zero)
            qv_sc[0, c] = qc
            qv_sc[1, c] = jnp.where(is_bias, -qc, qc)
            qv_sc[2, c] = jnp.where(is_bias, zero, qc)

    def body(ends_on_diagonal):
        for j in range(n_sub):
            diagonal = ends_on_diagonal and j == n_sub - 1
            kt = (qi + 1 + ki * n_sub + j) % n_kt
            version = 2 if diagonal else jnp.where(kt > qi, 0, 1)
            for c in range(2):
                s_sc[j, c] = lax.dot_general(qv_sc[version, c], k_refs[j][0], _NT,
                                             preferred_element_type=F32)
        for j in range(n_sub):
            diagonal = ends_on_diagonal and j == n_sub - 1
            for c in range(2):
                _softmax_update(s_sc.at[j, c], p_sc.at[j, c], m_sc.at[c], l_sc.at[c], acc_sc.at[c],
                                dbias_ref.at[0] if diagonal else None)
                acc_sc[c] += jnp.dot(p_sc[j, c], v_refs[j][0], preferred_element_type=F32)

    last = pl.num_programs(3) - 1

    @pl.when(ki != last)
    def _():
        body(False)

    @pl.when(ki == last)
    def _():
        body(True)

    @pl.when(ki == last)
    def _():
        l0 = jnp.sum(l_sc[0], axis=1, keepdims=True)
        l1 = jnp.sum(l_sc[1], axis=1, keepdims=True)
        o = acc_sc[0] / l0 - lam_ref[...] * (acc_sc[1] / l1)
        ms = jnp.mean(o * o, axis=-1, keepdims=True)
        o = o * lax.rsqrt(ms + RMS_EPS) * g_ref[...] * (1.0 - LAM_INIT)
        o_ref[0] = o.astype(o_ref.dtype)


def _diff_attn_call(qa, ka, va, dbias, lam, g, t, n_sub):
    b, s, _ = va.shape
    hw = 2 * HEAD_DIM
    qw = hw + LANES
    n_kt = s // t

    def kv_spec(width, j):
        return pl.BlockSpec(
            (1, t, width), lambda bi, h, qi, ki: (bi, (qi + 1 + ki * n_sub + j) % n_kt, h))

    return pl.pallas_call(
        functools.partial(_diff_attn_kernel, n_sub=n_sub, n_kt=n_kt),
        grid=(b, N_HEADS_A, n_kt, n_kt // n_sub),
        in_specs=[
            pl.BlockSpec((1, LANES), lambda bi, h, qi, ki: (0, 0)),
            pl.BlockSpec((1, hw), lambda bi, h, qi, ki: (0, 0)),
            pl.BlockSpec((1, t, t), lambda bi, h, qi, ki: (h, 0, 0)),
            pl.BlockSpec((1, t, qw), lambda bi, h, qi, ki: (bi, qi, h)),
            *[kv_spec(qw, j) for j in range(n_sub)],
            *[kv_spec(hw, j) for j in range(n_sub)],
        ],
        out_specs=pl.BlockSpec((1, t, hw), lambda bi, h, qi, ki: (bi, qi, h)),
        out_shape=jax.ShapeDtypeStruct((b, s, WIDTH_A), BF16),
        scratch_shapes=[
            pltpu.VMEM((3, 2, t, qw), BF16),
            pltpu.VMEM((n_sub, 2, t, t), F32),
            pltpu.VMEM((n_sub, 2, t, t), BF16),
            pltpu.VMEM((2, t, LANES), F32),
            pltpu.VMEM((2, t, LANES), F32),
            pltpu.VMEM((2, t, hw), F32),
        ],
        compiler_params=pltpu.CompilerParams(
            dimension_semantics=("parallel", "parallel", "parallel", "arbitrary"),
            vmem_limit_bytes=VMEM_LIMIT),
        name="diff_attn",
    )(lam, g, dbias, qa, *([ka] * n_sub), *([va] * n_sub))


def _gqa_kernel(q_ref, k_ref, v_ref, o_ref, s_sc, p_sc, m_sc, l_sc, acc_sc, *, tq, tk):
    ki = pl.program_id(3)
    rows = GROUP_B * tq

    @pl.when(ki == 0)
    def _():
        m_sc[...] = jnp.full(m_sc.shape, -jnp.inf, F32)
        l_sc[...] = jnp.zeros(l_sc.shape, F32)
        acc_sc[...] = jnp.zeros(acc_sc.shape, F32)

    n_sub = k_ref.shape[2] // tk
    q = q_ref[0].reshape(rows, HEAD_DIM)
    for j in range(n_sub):
        s_sc[j] = lax.dot_general(q, k_ref[0, 0, j * tk:(j + 1) * tk, :], _NT,
                                  preferred_element_type=F32)
    for j in range(n_sub):
        _softmax_update(s_sc.at[j], p_sc.at[j], m_sc, l_sc, acc_sc)
        acc_sc[...] += jnp.dot(p_sc[j], v_ref[0, 0, j * tk:(j + 1) * tk, :],
                               preferred_element_type=F32)

    @pl.when(ki == pl.num_programs(3) - 1)
    def _():
        o = acc_sc[...] / jnp.sum(l_sc[...], axis=1, keepdims=True)
        o_ref[0] = o.reshape(GROUP_B, tq, HEAD_DIM).astype(o_ref.dtype)


def _gqa_call(qb, kb, vb, tq, tk, n_sub):
    b, _, s, _ = qb.shape
    rows = GROUP_B * tq
    kv = n_sub * tk
    return pl.pallas_call(
        functools.partial(_gqa_kernel, tq=tq, tk=tk),
        grid=(b, N_KV_B, s // tq, s // kv),
        in_specs=[
            pl.BlockSpec((1, GROUP_B, tq, HEAD_DIM), lambda bi, n, qi, ki: (bi, n, qi, 0)),
            pl.BlockSpec((1, 1, kv, HEAD_DIM), lambda bi, n, qi, ki: (bi, n, ki, 0)),
            pl.BlockSpec((1, 1, kv, HEAD_DIM), lambda bi, n, qi, ki: (bi, n, ki, 0)),
        ],
        out_specs=pl.BlockSpec((1, GROUP_B, tq, HEAD_DIM), lambda bi, n, qi, ki: (bi, n, qi, 0)),
        out_shape=jax.ShapeDtypeStruct((b, N_HEADS_B, s, HEAD_DIM), BF16),
        scratch_shapes=[
            pltpu.VMEM((n_sub, rows, tk), F32),
            pltpu.VMEM((n_sub, rows, tk), BF16),
            pltpu.VMEM((rows, LANES), F32),
            pltpu.VMEM((rows, LANES), F32),
            pltpu.VMEM((rows, HEAD_DIM), F32),
        ],
        compiler_params=pltpu.CompilerParams(
            dimension_semantics=("parallel", "parallel", "parallel", "arbitrary"),
            vmem_limit_bytes=VMEM_LIMIT),
        name="gqa_attn",
    )(qb, kb, vb)


def _post_kernel(oa_ref, ob_ref, x_ref, mod_ref, w_ref, g_ref, b_ref, wrt_ref,
                 x1_ref, h2_ref, aff_ref):
    o = jnp.dot(oa_ref[0], w_ref[:WIDTH_A, :], preferred_element_type=F32)
    for hd in range(N_HEADS_B):
        lo = WIDTH_A + hd * HEAD_DIM
        o = o + jnp.dot(ob_ref[0, hd], w_ref[lo:lo + HEAD_DIM, :], preferred_element_type=F32)
    r = ALPHA * x_ref[0] + mod_ref[0, 2:3, :] * o
    x1 = _layer_norm(r) * g_ref[...] + b_ref[...]
    x1_ref[0] = x1
    h2 = _layer_norm(x1) * (1.0 + mod_ref[0, 4:5, :]) + mod_ref[0, 3:4, :]
    h2_ref[...] = h2.T.astype(BF16)
    logits = _dot3(wrt_ref[...], h2, _NT)
    z = jnp.exp(logits - jnp.max(logits, axis=0, keepdims=True))
    aff_ref[...] = z / jnp.sum(z, axis=0, keepdims=True)


def _post_call(oa, ob, x, mod, w_out, ln_g, ln_b, w_router_t, tm):
    b, s, d = x.shape
    nt = s // tm
    const = lambda shape: pl.BlockSpec(shape, lambda bi, ti: (0,) * len(shape))
    tok = lambda width: pl.BlockSpec((1, tm, width), lambda bi, ti: (bi, ti, 0))
    return pl.pallas_call(
        _post_kernel,
        grid=(b, nt),
        in_specs=[
            tok(WIDTH_A),
            pl.BlockSpec((1, N_HEADS_B, tm, HEAD_DIM), lambda bi, ti: (bi, 0, ti, 0)),
            tok(d),
            pl.BlockSpec((1, 6, d), lambda bi, ti: (bi, 0, 0)),
            const(w_out.shape), const(ln_g.shape), const(ln_b.shape), const(w_router_t.shape),
        ],
        out_specs=[tok(d),
                   pl.BlockSpec((d, tm), lambda bi, ti: (0, bi * nt + ti)),
                   pl.BlockSpec((N_EXPERTS, tm), lambda bi, ti: (0, bi * nt + ti))],
        out_shape=[
            jax.ShapeDtypeStruct((b, s, d), F32),
            jax.ShapeDtypeStruct((d, b * s), BF16),
            jax.ShapeDtypeStruct((N_EXPERTS, b * s), F32),
        ],
        compiler_params=pltpu.CompilerParams(
            dimension_semantics=("parallel", "parallel"), vmem_limit_bytes=VMEM_LIMIT),
        name="post",
    )(oa, ob, x, mod, w_out, ln_g, ln_b, w_router_t)


def _route_kernel(aff_ref, spos_ref, cnt_ref, *, cap, idx_bits):
    aff = aff_ref[...]
    n_e, n_r, _ = aff.shape
    bits = pltpu.bitcast(aff, I32)
    idx = (lax.broadcasted_iota(I32, aff.shape, 1) * LANES
           + lax.broadcasted_iota(I32, aff.shape, 2))

    def count(pred):
        part = jnp.sum(jnp.where(pred, 1.0, 0.0), axis=1, keepdims=True)
        return jnp.sum(part, axis=2, keepdims=True)

    def value_step(t, prefix):
        cand = prefix | (1 << (30 - t))
        return jnp.where(count(bits >= cand) >= cap, cand, prefix)

    thr = lax.fori_loop(0, 31, value_step, jnp.zeros((n_e, 1, 1), I32))
    above = bits > thr
    tie = bits == thr
    need = cap - count(above)

    def index_step(t, prefix):
        cand = prefix | (1 << (idx_bits - 1 - t))
        return jnp.where(count(tie & (idx < cand)) < need, cand, prefix)

    last = lax.fori_loop(0, idx_bits, index_step, jnp.zeros((n_e, 1, 1), I32))
    sel = above | (tie & (idx <= last))

    upper = (lax.broadcasted_iota(I32, (LANES, LANES), 0)
             <= lax.broadcasted_iota(I32, (LANES, LANES), 1)).astype(BF16)
    strict_lower = (lax.broadcasted_iota(I32, (n_r, n_r), 1)
                    < lax.broadcasted_iota(I32, (n_r, n_r), 0)).astype(BF16)
    for e in range(n_e):
        sel_e = sel[e]
        m = sel_e.astype(BF16)
        incl = jnp.dot(m, upper, preferred_element_type=F32)
        row_tot = jnp.broadcast_to(incl[:, LANES - 1:], incl.shape).astype(BF16)
        offs = jnp.dot(strict_lower, row_tot, preferred_element_type=F32)
        cnt = (incl + offs).astype(I32)
        cnt_ref[e] = cnt
        spos_ref[e] = jnp.where(sel_e, cnt - 1, -1)


def _route_call(aff3, cap):
    n_e, n_r, _ = aff3.shape
    idx_bits = max(1, (n_r * LANES - 1).bit_length())
    return pl.pallas_call(
        functools.partial(_route_kernel, cap=cap, idx_bits=idx_bits),
        out_shape=[jax.ShapeDtypeStruct(aff3.shape, I32), jax.ShapeDtypeStruct(aff3.shape, I32)],
        compiler_params=pltpu.CompilerParams(vmem_limit_bytes=VMEM_LIMIT),
        name="route",
    )(aff3)


def _ffn_kernel(tend_ref, spos_ref, g_ref, x_ref, wg_ref, wu_ref, wd_ref, ye_ref,
                stage_sc, gate_sc, *, blk):
    e = pl.program_id(0)
    i = pl.program_id(1)
    prev_end = jnp.where(i > 0, tend_ref[e, jnp.maximum(i - 1, 0)], 0)
    base = (prev_end // blk) * blk
    fill = tend_ref[e, i] - base

    @pl.when(i == 0)
    def _():
        stage_sc[...] = jnp.zeros(stage_sc.shape, F32)
        gate_sc[...] = jnp.zeros(gate_sc.shape, F32)

    rel = spos_ref[0, 0] - base
    slot = lax.broadcasted_iota(I32, (blk, blk), 0)

    def place(lo, rel_lo):
        hit = slot == rel_lo
        onehot = jnp.where(hit, 1.0, 0.0).astype(BF16)
        cols = slice(lo, lo + blk)
        stage_sc[:, cols] += lax.dot_general(x_ref[...], onehot, _NT, preferred_element_type=F32)
        gate_sc[cols, :] += jnp.sum(jnp.where(hit, g_ref[0, 0], 0.0), axis=1, keepdims=True)

    place(0, rel)

    @pl.when(fill > blk)
    def _():
        place(blk, rel - blk)

    @pl.when(fill >= blk)
    def _():
        xb = stage_sc[:, :blk].T.astype(BF16)
        gate = jnp.dot(xb, wg_ref[0], preferred_element_type=F32)
        up = jnp.dot(xb, wu_ref[0], preferred_element_type=F32)
        hid = (gate * jax.nn.sigmoid(gate) * up).astype(BF16)
        y = jnp.dot(hid, wd_ref[0], preferred_element_type=F32) * gate_sc[:blk, :1]
        for w in range(blk // SLOT_BLOCK):
            ye_ref[0, w] = y[w * SLOT_BLOCK:(w + 1) * SLOT_BLOCK, :].T.astype(ye_ref.dtype)
        stage_sc[:, :blk] = stage_sc[:, blk:]
        stage_sc[:, blk:] = jnp.zeros((stage_sc.shape[0], blk), F32)
        gate_sc[:blk, :] = gate_sc[blk:, :]
        gate_sc[blk:, :] = jnp.zeros((blk, LANES), F32)


def _ffn_call(tile_end, spos4, aff4, h2t, wg, wu, wd, cap, blk):
    d, n = h2t.shape
    n_t = n // blk
    ff = wg.shape[2]
    n_blk = cap // blk
    sub = blk // SLOT_BLOCK
    tt = blk

    def out_map(e, i, tend):
        prev_end = jnp.where(i > 0, tend[e, jnp.maximum(i - 1, 0)], 0)
        return (e, jnp.clip(prev_end // blk, 0, n_blk - 1), 0, 0)

    grid_spec = pltpu.PrefetchScalarGridSpec(
        num_scalar_prefetch=1,
        grid=(N_EXPERTS, n_t),
        in_specs=[
            pl.BlockSpec((1, 1, 1, tt), lambda e, i, tend: (e, i, 0, 0)),
            pl.BlockSpec((1, 1, 1, tt), lambda e, i, tend: (e, i, 0, 0)),
            pl.BlockSpec((d, tt), lambda e, i, tend: (0, i)),
            pl.BlockSpec((1, d, ff), lambda e, i, tend: (e, 0, 0)),
            pl.BlockSpec((1, d, ff), lambda e, i, tend: (e, 0, 0)),
            pl.BlockSpec((1, ff, d), lambda e, i, tend: (e, 0, 0)),
        ],
        out_specs=pl.BlockSpec((1, sub, d, SLOT_BLOCK), out_map),
        scratch_shapes=[
            pltpu.VMEM((d, 2 * blk), F32),
            pltpu.VMEM((2 * blk, LANES), F32),
        ],
    )
    return pl.pallas_call(
        functools.partial(_ffn_kernel, blk=blk),
        grid_spec=grid_spec,
        out_shape=jax.ShapeDtypeStruct((N_EXPERTS, n_blk * sub, d, SLOT_BLOCK), BF16),
        compiler_params=pltpu.CompilerParams(
            dimension_semantics=("arbitrary", "arbitrary"), vmem_limit_bytes=VMEM_LIMIT),
        name="ffn",
    )(tile_end, spos4, aff4, h2t, wg, wu, wd)


def _slot_span(tend, e, i):
    lo = jnp.where(i > 0, tend[e, jnp.maximum(i - 1, 0)], 0)
    return lo, tend[e, i]


def _combine_kernel(tend_ref, spos_ref, x1_ref, mod_ref, g_ref, b_ref, *refs):
    first = refs[:N_EXPERTS]
    second = refs[N_EXPERTS:2 * N_EXPERTS]
    o_ref, acc_sc = refs[2 * N_EXPERTS:]
    i = pl.program_id(0)
    tt = x1_ref.shape[0]
    slot = lax.broadcasted_iota(I32, (SLOT_BLOCK, tt), 0)

    def onehot(e, block_offset):
        lo, _ = _slot_span(tend_ref, e, i)
        rel = spos_ref[e, 0] - (lo // SLOT_BLOCK + block_offset) * SLOT_BLOCK
        return jnp.where(slot == rel, 1.0, 0.0).astype(BF16)

    acc_sc[...] = jnp.dot(
        jnp.concatenate([first[e][0, 0] for e in range(N_EXPERTS)], axis=1),
        jnp.concatenate([onehot(e, 0) for e in range(N_EXPERTS)], axis=0),
        preferred_element_type=F32)

    for e in range(N_EXPERTS):
        lo, hi = _slot_span(tend_ref, e, i)

        @pl.when(hi > (lo // SLOT_BLOCK + 1) * SLOT_BLOCK)
        def _(e=e):
            acc_sc[...] += jnp.dot(second[e][0, 0], onehot(e, 1), preferred_element_type=F32)

    r = ALPHA * x1_ref[...] + mod_ref[0, 5:6, :] * acc_sc[...].T
    o_ref[...] = _layer_norm(r) * g_ref[...] + b_ref[...]


def _combine_call(tile_end, spos4, ye_t, x1, mod, ln_g, ln_b, tt, tokens_per_batch):
    n, d = x1.shape
    n_t = n // tt
    n_blk = ye_t.shape[1]
    tiles_per_batch = tokens_per_batch // tt

    def first_spec(e):
        def index(i, tend):
            lo, _ = _slot_span(tend, e, i)
            return (e, jnp.clip(lo // SLOT_BLOCK, 0, n_blk - 1), 0, 0)
        return pl.BlockSpec((1, 1, d, SLOT_BLOCK), index)

    def second_spec(e):
        def index(i, tend):
            lo, hi = _slot_span(tend, e, i)
            nxt = lo // SLOT_BLOCK + 1
            spill = (hi > nxt * SLOT_BLOCK) & (nxt < n_blk)
            return (jnp.where(spill, e, 0), jnp.where(spill, jnp.clip(nxt, 0, n_blk - 1), 0), 0, 0)
        return pl.BlockSpec((1, 1, d, SLOT_BLOCK), index)

    grid_spec = pltpu.PrefetchScalarGridSpec(
        num_scalar_prefetch=1,
        grid=(n_t,),
        in_specs=[
            pl.BlockSpec((N_EXPERTS, 1, 1, tt), lambda i, tend: (0, i, 0, 0)),
            pl.BlockSpec((tt, d), lambda i, tend: (i, 0)),
            pl.BlockSpec((1, 6, d), lambda i, tend: (i // tiles_per_batch, 0, 0)),
            pl.BlockSpec((1, d), lambda i, tend: (0, 0)),
            pl.BlockSpec((1, d), lambda i, tend: (0, 0)),
            *[first_spec(e) for e in range(N_EXPERTS)],
            *[second_spec(e) for e in range(N_EXPERTS)],
        ],
        out_specs=pl.BlockSpec((tt, d), lambda i, tend: (i, 0)),
        scratch_shapes=[pltpu.VMEM((d, tt), F32)],
    )
    return pl.pallas_call(
        _combine_kernel,
        grid_spec=grid_spec,
        out_shape=jax.ShapeDtypeStruct((n, d), F32),
        compiler_params=pltpu.CompilerParams(
            dimension_semantics=("parallel",), vmem_limit_bytes=VMEM_LIMIT),
        name="combine",
    )(tile_end, spos4, x1, mod, ln_g, ln_b, *([ye_t] * (2 * N_EXPERTS)))


def _rope_tables(s):
    pos = jnp.arange(s)
    half = HEAD_DIM // 2
    inv_freq = ROPE_THETA ** (-jnp.arange(0, half, 2, dtype=F32) / half)
    ang_r = (pos // GRID_W).astype(F32)[:, None] * inv_freq[None, :]
    ang_c = (pos % GRID_W).astype(F32)[:, None] * inv_freq[None, :]
    zero = jnp.zeros_like(ang_r)
    cos = jnp.concatenate([jnp.cos(ang_r)] * 2 + [jnp.cos(ang_c)] * 2, axis=1)
    slo = jnp.concatenate([-jnp.sin(ang_r), zero, -jnp.sin(ang_c), zero], axis=1)
    shi = jnp.concatenate([zero, jnp.sin(ang_r), zero, jnp.sin(ang_c)], axis=1)
    rep = LANES // HEAD_DIM
    return tuple(jnp.concatenate([t] * rep, axis=1) for t in (cos, slo, shi))


def _bf16_terms(value, n=3):
    terms, rem = [], np.float32(value)
    for _ in range(n):
        part = np.float32(np.asarray(rem, dtype=jnp.bfloat16))
        terms.append(float(part))
        rem = np.float32(rem - part)
    return terms


def _alibi_constants(t):
    cq = np.zeros((N_HEADS_A, LANES), np.float32)
    ck = np.zeros((N_HEADS_A, LANES), np.float32)
    d = jnp.arange(t)
    dist = jnp.abs(d[:, None] - d[None, :]).astype(F32)
    db = []
    for h in range(N_HEADS_A):
        terms = _bf16_terms(2.0 ** (-8.0 * (h + 1) / N_HEADS_A) * LOG2E)
        cq[h, 0:6] = terms + terms
        ck[h, 6:12] = terms + terms
        db.append(-np.float32(sum(terms)) * dist)
    return jnp.asarray(cq), jnp.asarray(ck), jnp.stack(db)


def _tile(dim, target):
    t = min(dim, target)
    assert dim % t == 0, (dim, t)
    return t


def _trunk(x, mod, lam, consts, w):
    b, s, d = x.shape
    n = b * s
    cap = CAPACITY_FACTOR * n // N_EXPERTS
    assert cap % SLOT_BLOCK == 0 and n % LANES == 0
    tm = _tile(s, 512)
    tt = _tile(n, SLOT_BLOCK)

    ta = _tile(s, 512)
    cq, ck, dbias = _alibi_constants(ta)
    qa, ka, va, qb, kb, vb = _proj_call(
        x, mod, w["w_in"], w["gq"], w["gk"], *consts["rope"], consts["bd"], cq, ck, tm)
    n_sub = min(ATTN_SUBTILES, s // ta)
    assert (s // ta) % n_sub == 0, (s, ta, n_sub)
    oa = _diff_attn_call(qa, ka, va, dbias, lam, w["subln_g"], ta, n_sub)
    ob = _gqa_call(qb, kb, vb, _tile(s, 256), ta, n_sub)
    x1, h2t, aff_t = _post_call(oa, ob, x, mod, w["w_out"], w["ln1_g"], w["ln1_b"],
                                w["w_router_t"], tm)

    spos, cnt = _route_call(aff_t.reshape(N_EXPERTS, n // LANES, LANES), cap)
    cnt = cnt.reshape(N_EXPERTS, n)

    def tiles(a, width):
        return a.reshape(N_EXPERTS, n // width, 1, width)

    def tile_ends(width):
        return cnt[:, width - 1::width]

    blk = math.gcd(FFN_BLOCK, cap)
    ye_t = _ffn_call(tile_ends(blk), tiles(spos, blk), tiles(aff_t, blk), h2t,
                     w["w_gate"], w["w_up"], w["w_down"], cap, blk)
    out = _combine_call(tile_ends(tt), tiles(spos, tt), ye_t, x1.reshape(n, d), mod,
                        w["ln2_g"], w["ln2_b"], tt, s)
    return out.reshape(b, s, d)


def kernel(x_prompt, x_sample, c_prompt, c_sample, w_ada, b_ada, w_in, lam_q1, lam_k1, lam_q2,
           lam_k2, subln_g, q_norm_g, k_norm_g, w_out, ln1_g, ln1_b, w_router, w_gate, w_up,
           w_down, ln2_g, ln2_b):
    l = 0
    d = x_prompt.shape[-1]
    c_all = jnp.concatenate([c_prompt, c_sample], axis=0)
    mod, lam = _mod_call(c_all, w_ada[l], b_ada[l][None, :], lam_q1[l][None, :], lam_k1[l][None, :],
                         lam_q2[l][None, :], lam_k2[l][None, :])
    mod = mod.reshape(c_all.shape[0], 6, d)

    w = {
        "w_in": w_in[l].astype(BF16),
        "gq": jnp.tile(q_norm_g[l], N_HEADS_B)[None, :],
        "gk": jnp.tile(k_norm_g[l], N_KV_B)[None, :],
        "subln_g": subln_g[l][None, :],
        "w_out": w_out[l].astype(BF16),
        "ln1_g": ln1_g[l][None, :], "ln1_b": ln1_b[l][None, :],
        "w_router_t": w_router[l].T,
        "w_gate": w_gate[l].astype(BF16), "w_up": w_up[l].astype(BF16),
        "w_down": w_down[l].astype(BF16),
        "ln2_g": ln2_g[l][None, :], "ln2_b": ln2_b[l][None, :],
    }
    head_of = jnp.arange(WIDTH_B) // HEAD_DIM
    shared = {"bd": (head_of[:, None] == head_of[None, :]).astype(BF16)}

    outs = []
    nb = x_prompt.shape[0]
    for x, m in ((x_prompt, mod[:nb]), (x_sample, mod[nb:])):
        consts = dict(shared, rope=_rope_tables(x.shape[1]))
        outs.append(_trunk(x, m, lam, consts, w))
    return tuple(outs)
```

```python
import functools
import math

import jax
import jax.numpy as jnp
import numpy as np
from jax import lax
from jax.experimental import pallas as pl
from jax.experimental.pallas import tpu as pltpu

F32 = jnp.float32
BF16 = jnp.bfloat16
I32 = jnp.int32

HEAD_DIM = 64
N_HEADS_A = 4
WIDTH_A = N_HEADS_A * 2 * HEAD_DIM
N_HEADS_B = 8
N_KV_B = 2
GROUP_B = N_HEADS_B // N_KV_B
WIDTH_B = N_HEADS_B * HEAD_DIM
KV_B = N_KV_B * HEAD_DIM
GRID_W = 64
ROPE_THETA = 10000.0
N_EXPERTS = 16
CAPACITY_FACTOR = 2
DEPTH = 1
ALPHA = (2.0 * DEPTH) ** 0.25
LN_EPS = 1e-5
RMS_EPS = 1e-6
LAM_INIT = 0.8 - 0.6 * math.exp(-0.3 * 0)

LOG2E = 1.4426950408889634
SOFTMAX_CHUNK = 32
N_BIAS_COLS = 12
ATTN_SUBTILES = 8

LANES = 128
SLOT_BLOCK = 256
FFN_BLOCK = 512
VMEM_LIMIT = 56 * 1024 * 1024

_NT = (((1,), (1,)), ((), ()))


def _split_bf16(a):
    hi = a.astype(BF16)
    lo = (a - hi.astype(F32)).astype(BF16)
    return hi, lo


def _dot3(a, b, dims=(((1,), (0,)), ((), ()))):
    ah, al = _split_bf16(a)
    bh, bl = _split_bf16(b)
    d = functools.partial(lax.dot_general, dimension_numbers=dims, preferred_element_type=F32)
    return d(ah, bh) + (d(ah, bl) + d(al, bh))


def _layer_norm(x):
    mu = jnp.mean(x, axis=-1, keepdims=True)
    xc = x - mu
    var = jnp.mean(xc * xc, axis=-1, keepdims=True)
    return xc * lax.rsqrt(var + LN_EPS)


def _lanes(x, n):
    if n == LANES:
        return x
    if n < LANES:
        return x[:, :n]
    return jnp.concatenate([x] * (n // LANES), axis=1)


def _mod_kernel(c_ref, w_ref, b_ref, lq1_ref, lk1_ref, lq2_ref, lk2_ref, mod_ref, lam_ref):
    c = c_ref[...]
    a = c * jax.nn.sigmoid(c)
    mod_ref[...] = _dot3(a, w_ref[...]) + b_ref[...]
    s1 = jnp.sum(lq1_ref[...] * lk1_ref[...], axis=-1, keepdims=True)
    s2 = jnp.sum(lq2_ref[...] * lk2_ref[...], axis=-1, keepdims=True)
    lam = jnp.exp(s1) - jnp.exp(s2) + LAM_INIT
    lam_ref[...] = jnp.broadcast_to(lam, lam_ref.shape)


def _mod_call(c, w_ada, b_ada, lq1, lk1, lq2, lk2):
    bt, d = c.shape
    n_chunks = w_ada.shape[1] // d
    vec = pl.BlockSpec((1, HEAD_DIM), lambda j: (0, 0))
    return pl.pallas_call(
        _mod_kernel,
        grid=(n_chunks,),
        in_specs=[
            pl.BlockSpec((bt, d), lambda j: (0, 0)),
            pl.BlockSpec((d, d), lambda j: (0, j)),
            pl.BlockSpec((1, d), lambda j: (0, j)),
            vec, vec, vec, vec,
        ],
        out_specs=[
            pl.BlockSpec((bt, d), lambda j: (0, j)),
            pl.BlockSpec((1, LANES), lambda j: (0, 0)),
        ],
        out_shape=[
            jax.ShapeDtypeStruct((bt, n_chunks * d), F32),
            jax.ShapeDtypeStruct((1, LANES), F32),
        ],
        compiler_params=pltpu.CompilerParams(vmem_limit_bytes=VMEM_LIMIT),
        name="mod",
    )(c, w_ada, b_ada, lq1, lk1, lq2, lk2)


def _head_rms(x, bd, g):
    sq = x * x
    hi, lo = _split_bf16(sq)
    ss = jnp.dot(hi, bd, preferred_element_type=F32) + jnp.dot(lo, bd, preferred_element_type=F32)
    return x * lax.rsqrt(ss * (1.0 / HEAD_DIM) + RMS_EPS) * g


def _rope(x, cos, sin_lo, sin_hi):
    w = x.shape[1]
    nxt = pltpu.roll(x, w - 16, axis=1)
    prv = pltpu.roll(x, 16, axis=1)
    return x * cos + nxt * sin_lo + prv * sin_hi


def _proj_kernel(x_ref, mod_ref, w_ref, gq_ref, gk_ref, cos_ref, slo_ref, shi_ref, bd_ref, cq_ref, ck_ref,
                 qa_ref, ka_ref, va_ref, qb_ref, kb_ref, vb_ref):
    x = x_ref[0]
    h = _layer_norm(x) * (1.0 + mod_ref[0, 1:2, :]) + mod_ref[0, 0:1, :]
    hb = h.astype(BF16)

    def proj(lo, width):
        return jnp.dot(hb, w_ref[:, lo:lo + width], preferred_element_type=F32)

    scale = HEAD_DIM ** -0.5 * LOG2E
    qa = (proj(0, WIDTH_A) * scale).astype(BF16)
    ka = proj(WIDTH_A, WIDTH_A).astype(BF16)
    hw = 2 * HEAD_DIM
    tm = x.shape[0]
    pos = pl.program_id(1) * tm + lax.broadcasted_iota(I32, (tm, LANES), 0)
    pos_hi = ((pos // LANES) * LANES).astype(F32)
    pos_lo = (pos % LANES).astype(F32)
    lane = lax.broadcasted_iota(I32, (tm, LANES), 1)
    zero = jnp.zeros((tm, LANES), F32)
    for hd in range(N_HEADS_A):
        lo = hd * (hw + LANES)
        bias_q = jnp.where(lane < 6, cq_ref[hd:hd + 1, :],
                           jnp.where(lane < 9, pos_hi, jnp.where(lane < N_BIAS_COLS, pos_lo, zero)))
        bias_k = jnp.where(lane < 3, -pos_hi,
                           jnp.where(lane < 6, -pos_lo, jnp.where(lane < N_BIAS_COLS, ck_ref[hd:hd + 1, :], zero)))
        qa_ref[0, :, lo:lo + hw] = qa[:, hd * hw:(hd + 1) * hw]
        qa_ref[0, :, lo + hw:lo + hw + LANES] = bias_q.astype(BF16)
        ka_ref[0, :, lo:lo + hw] = ka[:, hd * hw:(hd + 1) * hw]
        ka_ref[0, :, lo + hw:lo + hw + LANES] = bias_k.astype(BF16)
    va_ref[0] = proj(2 * WIDTH_A, WIDTH_A).astype(BF16)

    cos, slo, shi = cos_ref[...], slo_ref[...], shi_ref[...]
    bd = bd_ref[...]
    qb = _head_rms(proj(3 * WIDTH_A, WIDTH_B), bd, gq_ref[...])
    qb = _rope(qb, _lanes(cos, WIDTH_B), _lanes(slo, WIDTH_B), _lanes(shi, WIDTH_B)) * scale
    for hd in range(N_HEADS_B):
        qb_ref[0, hd] = qb[:, hd * HEAD_DIM:(hd + 1) * HEAD_DIM].astype(BF16)
    kb = _head_rms(proj(3 * WIDTH_A + WIDTH_B, KV_B), bd[:KV_B, :KV_B], gk_ref[...])
    kb = _rope(kb, cos, slo, shi)
    vb = proj(3 * WIDTH_A + WIDTH_B + KV_B, KV_B)
    for hd in range(N_KV_B):
        kb_ref[0, hd] = kb[:, hd * HEAD_DIM:(hd + 1) * HEAD_DIM].astype(BF16)
        vb_ref[0, hd] = vb[:, hd * HEAD_DIM:(hd + 1) * HEAD_DIM].astype(BF16)


def _proj_call(x, mod, w_in, gq, gk, cos, slo, shi, bd, cq, ck, tm):
    b, s, d = x.shape
    nt = s // tm
    wide_a = N_HEADS_A * (2 * HEAD_DIM + LANES)
    bias_tab = pl.BlockSpec((N_HEADS_A, LANES), lambda bi, ti: (0, 0))
    const = lambda shape: pl.BlockSpec(shape, lambda bi, ti: (0,) * len(shape))
    tok = lambda width: pl.BlockSpec((1, tm, width), lambda bi, ti: (bi, ti, 0))
    heads = lambda n: pl.BlockSpec((1, n, tm, HEAD_DIM), lambda bi, ti: (bi, 0, ti, 0))
    tab = pl.BlockSpec((tm, LANES), lambda bi, ti: (ti, 0))
    return pl.pallas_call(
        _proj_kernel,
        grid=(b, nt),
        in_specs=[
            tok(d),
            pl.BlockSpec((1, 6, d), lambda bi, ti: (bi, 0, 0)),
            const(w_in.shape),
            const(gq.shape), const(gk.shape),
            tab, tab, tab,
            const(bd.shape),
            bias_tab, bias_tab,
        ],
        out_specs=[tok(wide_a), tok(wide_a), tok(WIDTH_A),
                   heads(N_HEADS_B), heads(N_KV_B), heads(N_KV_B)],
        out_shape=[
            jax.ShapeDtypeStruct((b, s, wide_a), BF16),
            jax.ShapeDtypeStruct((b, s, wide_a), BF16),
            jax.ShapeDtypeStruct((b, s, WIDTH_A), BF16),
            jax.ShapeDtypeStruct((b, N_HEADS_B, s, HEAD_DIM), BF16),
            jax.ShapeDtypeStruct((b, N_KV_B, s, HEAD_DIM), BF16),
            jax.ShapeDtypeStruct((b, N_KV_B, s, HEAD_DIM), BF16),
        ],
        compiler_params=pltpu.CompilerParams(
            dimension_semantics=("parallel", "parallel"), vmem_limit_bytes=VMEM_LIMIT),
        name="proj",
    )(x, mod, w_in, gq, gk, cos, slo, shi, bd, cq, ck)


def _lane_blocks(x, op):
    out = x[:, :LANES]
    for j in range(1, x.shape[1] // LANES):
        out = op(out, x[:, j * LANES:(j + 1) * LANES])
    return out


def _fold_row_max(s_ref, mt_ref, first):
    rows = s_ref.shape[0]
    for r0 in range(0, rows, SOFTMAX_CHUNK):
        sl = slice(r0, r0 + SOFTMAX_CHUNK)
        part = _lane_blocks(s_ref[sl, :], jnp.maximum)
        mt_ref[sl, :] = part if first else jnp.maximum(mt_ref[sl, :], part)


def _raise_max(mt_ref, m_ref, l_ref, acc_ref):
    rows = m_ref.shape[0]
    width = acc_ref.shape[-1]
    for r0 in range(0, rows, 4 * SOFTMAX_CHUNK):
        sl = slice(r0, r0 + 4 * SOFTMAX_CHUNK)
        m_prev = m_ref[sl, :]
        m_new = jnp.maximum(m_prev, jnp.max(mt_ref[sl, :], axis=1, keepdims=True))
        alpha = jnp.exp2(m_prev - m_new)
        m_ref[sl, :] = m_new
        l_ref[sl, :] = alpha * l_ref[sl, :]
        acc_ref[sl, :] = _lanes(alpha, width) * acc_ref[sl, :]


def _write_probs(s_ref, p_ref, col0, m_ref, l_ref):
    rows, tk = s_ref.shape
    for r0 in range(0, rows, SOFTMAX_CHUNK):
        sl = slice(r0, r0 + SOFTMAX_CHUNK)
        p = jnp.exp2(s_ref[sl, :] - _lanes(m_ref[sl, :], tk))
        l_ref[sl, :] += _lane_blocks(p, jnp.add)
        p_ref[sl, col0:col0 + tk] = p.astype(BF16)


def _softmax_update(s_ref, p_ref, m_ref, l_ref, acc_ref, bias_ref=None):
    rows, tk = s_ref.shape
    width = acc_ref.shape[-1]
    for r0 in range(0, rows, SOFTMAX_CHUNK):
        sl = slice(r0, r0 + SOFTMAX_CHUNK)
        s = s_ref[sl, :]
        if bias_ref is not None:
            s = s + bias_ref[sl, :]
            s_ref[sl, :] = s
        m_prev = m_ref[sl, :]
        m_new = jnp.maximum(m_prev, jnp.max(s, axis=1, keepdims=True))
        alpha = jnp.exp2(m_prev - m_new)
        m_ref[sl, :] = m_new
        l_ref[sl, :] = alpha * l_ref[sl, :]
        acc_ref[sl, :] = _lanes(alpha, width) * acc_ref[sl, :]
    _write_probs(s_ref, p_ref, 0, m_ref, l_ref)


def _diff_attn_kernel(lam_ref, g_ref, dbias_ref, q_ref, *refs, n_sub, n_kt):
    k_refs, v_refs = refs[:n_sub], refs[n_sub:2 * n_sub]
    o_ref, qv_sc, s_sc, p_sc, m_sc, l_sc, acc_sc = refs[2 * n_sub:]
    qi = pl.program_id(2)
    ki = pl.program_id(3)
    hw = 2 * HEAD_DIM

    @pl.when(ki == 0)
    def _():
        m_sc[...] = jnp.full(m_sc.shape, -jnp.inf, F32)
        l_sc[...] = jnp.zeros(l_sc.shape, F32)
        acc_sc[...] = jnp.zeros(acc_sc.shape, F32)
        q = q_ref[0]
        zero = jnp.zeros_like(q)
        lane = lax.broadcasted_iota(I32, q.shape, 1)
        is_bias = lane >= hw
        for c in range(2):
            qc = jnp.where(is_bias | ((lane >= c * HEAD_DIM) & (lane < (c + 1) * HEAD_DIM)), q, zero)
            qv_sc[0, c] = qc
            qv_sc[1, c] = jnp.where(is_bias, -qc, qc)
            qv_sc[2, c] = jnp.where(is_bias, zero, qc)

    def body(ends_on_diagonal):
        for j in range(n_sub):
            diagonal = ends_on_diagonal and j == n_sub - 1
            kt = (qi + 1 + ki * n_sub + j) % n_kt
            version = 2 if diagonal else jnp.where(kt > qi, 0, 1)
            for c in range(2):
                s_sc[j, c] = lax.dot_general(qv_sc[version, c], k_refs[j][0], _NT,
                                             preferred_element_type=F32)
        for j in range(n_sub):
            diagonal = ends_on_diagonal and j == n_sub - 1
            for c in range(2):
                _softmax_update(s_sc.at[j, c], p_sc.at[j, c], m_sc.at[c], l_sc.at[c], acc_sc.at[c],
                                dbias_ref.at[0] if diagonal else None)
                acc_sc[c] += jnp.dot(p_sc[j, c], v_refs[j][0], preferred_element_type=F32)

    last = pl.num_programs(3) - 1

    @pl.when(ki != last)
    def _():
        body(False)

    @pl.when(ki == last)
    def _():
        body(True)

    @pl.when(ki == last)
    def _():
        l0 = jnp.sum(l_sc[0], axis=1, keepdims=True)
        l1 = jnp.sum(l_sc[1], axis=1, keepdims=True)
        o = acc_sc[0] / l0 - lam_ref[...] * (acc_sc[1] / l1)
        ms = jnp.mean(o * o, axis=-1, keepdims=True)
        o = o * lax.rsqrt(ms + RMS_EPS) * g_ref[...] * (1.0 - LAM_INIT)
        o_ref[0] = o.astype(o_ref.dtype)


def _diff_attn_call(qa, ka, va, dbias, lam, g, t, n_sub):
    b, s, _ = va.shape
    hw = 2 * HEAD_DIM
    qw = hw + LANES
    n_kt = s // t

    def kv_spec(width, j):
        return pl.BlockSpec(
            (1, t, width), lambda bi, h, qi, ki: (bi, (qi + 1 + ki * n_sub + j) % n_kt, h))

    return pl.pallas_call(
        functools.partial(_diff_attn_kernel, n_sub=n_sub, n_kt=n_kt),
        grid=(b, N_HEADS_A, n_kt, n_kt // n_sub),
        in_specs=[
            pl.BlockSpec((1, LANES), lambda bi, h, qi, ki: (0, 0)),
            pl.BlockSpec((1, hw), lambda bi, h, qi, ki: (0, 0)),
            pl.BlockSpec((1, t, t), lambda bi, h, qi, ki: (h, 0, 0)),
            pl.BlockSpec((1, t, qw), lambda bi, h, qi, ki: (bi, qi, h)),
            *[kv_spec(qw, j) for j in range(n_sub)],
            *[kv_spec(hw, j) for j in range(n_sub)],
        ],
        out_specs=pl.BlockSpec((1, t, hw), lambda bi, h, qi, ki: (bi, qi, h)),
        out_shape=jax.ShapeDtypeStruct((b, s, WIDTH_A), BF16),
        scratch_shapes=[
            pltpu.VMEM((3, 2, t, qw), BF16),
            pltpu.VMEM((n_sub, 2, t, t), F32),
            pltpu.VMEM((n_sub, 2, t, t), BF16),
            pltpu.VMEM((2, t, LANES), F32),
            pltpu.VMEM((2, t, LANES), F32),
            pltpu.VMEM((2, t, hw), F32),
        ],
        compiler_params=pltpu.CompilerParams(
            dimension_semantics=("parallel", "parallel", "parallel", "arbitrary"),
            vmem_limit_bytes=VMEM_LIMIT),
        name="diff_attn",
    )(lam, g, dbias, qa, *([ka] * n_sub), *([va] * n_sub))


def _gqa_kernel(q_ref, k_ref, v_ref, o_ref, s_sc, p_sc, mt_sc, m_sc, l_sc, acc_sc, *, tq, tk):
    ki = pl.program_id(3)
    rows = GROUP_B * tq

    @pl.when(ki == 0)
    def _():
        m_sc[...] = jnp.full(m_sc.shape, -jnp.inf, F32)
        l_sc[...] = jnp.zeros(l_sc.shape, F32)
        acc_sc[...] = jnp.zeros(acc_sc.shape, F32)

    n_sub = k_ref.shape[2] // tk
    q = q_ref[0].reshape(rows, HEAD_DIM)
    for j in range(n_sub):
        s_sc[j] = lax.dot_general(q, k_ref[0, 0, j * tk:(j + 1) * tk, :], _NT,
                                  preferred_element_type=F32)
    for j in range(n_sub):
        _fold_row_max(s_sc.at[j], mt_sc, j == 0)
    _raise_max(mt_sc, m_sc, l_sc, acc_sc)
    for j in range(n_sub):
        _write_probs(s_sc.at[j], p_sc, j * tk, m_sc, l_sc)
    acc_sc[...] += jnp.dot(p_sc[...], v_ref[0, 0], preferred_element_type=F32)

    @pl.when(ki == pl.num_programs(3) - 1)
    def _():
        o = acc_sc[...] / jnp.sum(l_sc[...], axis=1, keepdims=True)
        o_ref[0] = o.reshape(GROUP_B, tq, HEAD_DIM).astype(o_ref.dtype)


def _gqa_call(qb, kb, vb, tq, tk, n_sub):
    b, _, s, _ = qb.shape
    rows = GROUP_B * tq
    kv = n_sub * tk
    return pl.pallas_call(
        functools.partial(_gqa_kernel, tq=tq, tk=tk),
        grid=(b, N_KV_B, s // tq, s // kv),
        in_specs=[
            pl.BlockSpec((1, GROUP_B, tq, HEAD_DIM), lambda bi, n, qi, ki: (bi, n, qi, 0)),
            pl.BlockSpec((1, 1, kv, HEAD_DIM), lambda bi, n, qi, ki: (bi, n, ki, 0)),
            pl.BlockSpec((1, 1, kv, HEAD_DIM), lambda bi, n, qi, ki: (bi, n, ki, 0)),
        ],
        out_specs=pl.BlockSpec((1, GROUP_B, tq, HEAD_DIM), lambda bi, n, qi, ki: (bi, n, qi, 0)),
        out_shape=jax.ShapeDtypeStruct((b, N_HEADS_B, s, HEAD_DIM), BF16),
        scratch_shapes=[
            pltpu.VMEM((n_sub, rows, tk), F32),
            pltpu.VMEM((rows, n_sub * tk), BF16),
            pltpu.VMEM((rows, LANES), F32),
            pltpu.VMEM((rows, LANES), F32),
            pltpu.VMEM((rows, LANES), F32),
            pltpu.VMEM((rows, HEAD_DIM), F32),
        ],
        compiler_params=pltpu.CompilerParams(
            dimension_semantics=("parallel", "parallel", "parallel", "arbitrary"),
            vmem_limit_bytes=VMEM_LIMIT),
        name="gqa_attn",
    )(qb, kb, vb)


def _post_kernel(oa_ref, ob_ref, x_ref, mod_ref, w_ref, g_ref, b_ref, wrt_ref,
                 x1_ref, h2_ref, aff_ref):
    o = jnp.dot(oa_ref[0], w_ref[:WIDTH_A, :], preferred_element_type=F32)
    for hd in range(N_HEADS_B):
        lo = WIDTH_A + hd * HEAD_DIM
        o = o + jnp.dot(ob_ref[0, hd], w_ref[lo:lo + HEAD_DIM, :], preferred_element_type=F32)
    r = ALPHA * x_ref[0] + mod_ref[0, 2:3, :] * o
    x1 = _layer_norm(r) * g_ref[...] + b_ref[...]
    x1_ref[0] = x1
    h2 = _layer_norm(x1) * (1.0 + mod_ref[0, 4:5, :]) + mod_ref[0, 3:4, :]
    h2_ref[...] = h2.T.astype(BF16)
    logits = _dot3(wrt_ref[...], h2, _NT)
    z = jnp.exp(logits - jnp.max(logits, axis=0, keepdims=True))
    aff_ref[...] = z / jnp.sum(z, axis=0, keepdims=True)


def _post_call(oa, ob, x, mod, w_out, ln_g, ln_b, w_router_t, tm):
    b, s, d = x.shape
    nt = s // tm
    const = lambda shape: pl.BlockSpec(shape, lambda bi, ti: (0,) * len(shape))
    tok = lambda width: pl.BlockSpec((1, tm, width), lambda bi, ti: (bi, ti, 0))
    return pl.pallas_call(
        _post_kernel,
        grid=(b, nt),
        in_specs=[
            tok(WIDTH_A),
            pl.BlockSpec((1, N_HEADS_B, tm, HEAD_DIM), lambda bi, ti: (bi, 0, ti, 0)),
            tok(d),
            pl.BlockSpec((1, 6, d), lambda bi, ti: (bi, 0, 0)),
            const(w_out.shape), const(ln_g.shape), const(ln_b.shape), const(w_router_t.shape),
        ],
        out_specs=[tok(d),
                   pl.BlockSpec((d, tm), lambda bi, ti: (0, bi * nt + ti)),
                   pl.BlockSpec((N_EXPERTS, tm), lambda bi, ti: (0, bi * nt + ti))],
        out_shape=[
            jax.ShapeDtypeStruct((b, s, d), F32),
            jax.ShapeDtypeStruct((d, b * s), BF16),
            jax.ShapeDtypeStruct((N_EXPERTS, b * s), F32),
        ],
        compiler_params=pltpu.CompilerParams(
            dimension_semantics=("parallel", "parallel"), vmem_limit_bytes=VMEM_LIMIT),
        name="post",
    )(oa, ob, x, mod, w_out, ln_g, ln_b, w_router_t)


def _route_kernel(aff_ref, spos_ref, cnt_ref, *, cap, idx_bits):
    aff = aff_ref[...]
    n_e, n_r, _ = aff.shape
    bits = pltpu.bitcast(aff, I32)
    idx = (lax.broadcasted_iota(I32, aff.shape, 1) * LANES
           + lax.broadcasted_iota(I32, aff.shape, 2))

    def count(pred):
        part = jnp.sum(jnp.where(pred, 1.0, 0.0), axis=1, keepdims=True)
        return jnp.sum(part, axis=2, keepdims=True)

    def value_step(t, prefix):
        cand = prefix | (1 << (30 - t))
        return jnp.where(count(bits >= cand) >= cap, cand, prefix)

    thr = lax.fori_loop(0, 31, value_step, jnp.zeros((n_e, 1, 1), I32))
    above = bits > thr
    tie = bits == thr
    need = cap - count(above)

    def index_step(t, prefix):
        cand = prefix | (1 << (idx_bits - 1 - t))
        return jnp.where(count(tie & (idx < cand)) < need, cand, prefix)

    last = lax.fori_loop(0, idx_bits, index_step, jnp.zeros((n_e, 1, 1), I32))
    sel = above | (tie & (idx <= last))

    upper = (lax.broadcasted_iota(I32, (LANES, LANES), 0)
             <= lax.broadcasted_iota(I32, (LANES, LANES), 1)).astype(BF16)
    strict_lower = (lax.broadcasted_iota(I32, (n_r, n_r), 1)
                    < lax.broadcasted_iota(I32, (n_r, n_r), 0)).astype(BF16)
    for e in range(n_e):
        sel_e = sel[e]
        m = sel_e.astype(BF16)
        incl = jnp.dot(m, upper, preferred_element_type=F32)
        row_tot = jnp.broadcast_to(incl[:, LANES - 1:], incl.shape).astype(BF16)
        offs = jnp.dot(strict_lower, row_tot, preferred_element_type=F32)
        cnt = (incl + offs).astype(I32)
        cnt_ref[e] = cnt
        spos_ref[e] = jnp.where(sel_e, cnt - 1, -1)


def _route_call(aff3, cap):
    n_e, n_r, _ = aff3.shape
    idx_bits = max(1, (n_r * LANES - 1).bit_length())
    return pl.pallas_call(
        functools.partial(_route_kernel, cap=cap, idx_bits=idx_bits),
        out_shape=[jax.ShapeDtypeStruct(aff3.shape, I32), jax.ShapeDtypeStruct(aff3.shape, I32)],
        compiler_params=pltpu.CompilerParams(vmem_limit_bytes=VMEM_LIMIT),
        name="route",
    )(aff3)


def _ffn_kernel(tend_ref, spos_ref, g_ref, x_ref, wg_ref, wu_ref, wd_ref, ye_ref,
                stage_sc, gate_sc, *, blk):
    e = pl.program_id(0)
    i = pl.program_id(1)
    prev_end = jnp.where(i > 0, tend_ref[e, jnp.maximum(i - 1, 0)], 0)
    base = (prev_end // blk) * blk
    fill = tend_ref[e, i] - base

    @pl.when(i == 0)
    def _():
        stage_sc[...] = jnp.zeros(stage_sc.shape, F32)
        gate_sc[...] = jnp.zeros(gate_sc.shape, F32)

    rel = spos_ref[0, 0] - base
    slot = lax.broadcasted_iota(I32, (blk, blk), 0)

    def place(lo, rel_lo):
        hit = slot == rel_lo
        onehot = jnp.where(hit, 1.0, 0.0).astype(BF16)
        cols = slice(lo, lo + blk)
        stage_sc[:, cols] += lax.dot_general(x_ref[...], onehot, _NT, preferred_element_type=F32)
        gate_sc[cols, :] += jnp.sum(jnp.where(hit, g_ref[0, 0], 0.0), axis=1, keepdims=True)

    place(0, rel)

    @pl.when(fill > blk)
    def _():
        place(blk, rel - blk)

    @pl.when(fill >= blk)
    def _():
        xb = stage_sc[:, :blk].T.astype(BF16)
        gate = jnp.dot(xb, wg_ref[0], preferred_element_type=F32)
        up = jnp.dot(xb, wu_ref[0], preferred_element_type=F32)
        hid = (gate * jax.nn.sigmoid(gate) * up).astype(BF16)
        y = jnp.dot(hid, wd_ref[0], preferred_element_type=F32) * gate_sc[:blk, :1]
        for w in range(blk // SLOT_BLOCK):
            ye_ref[0, w] = y[w * SLOT_BLOCK:(w + 1) * SLOT_BLOCK, :].T.astype(ye_ref.dtype)
        stage_sc[:, :blk] = stage_sc[:, blk:]
        stage_sc[:, blk:] = jnp.zeros((stage_sc.shape[0], blk), F32)
        gate_sc[:blk, :] = gate_sc[blk:, :]
        gate_sc[blk:, :] = jnp.zeros((blk, LANES), F32)


def _ffn_call(tile_end, spos4, aff4, h2t, wg, wu, wd, cap, blk):
    d, n = h2t.shape
    n_t = n // blk
    ff = wg.shape[2]
    n_blk = cap // blk
    sub = blk // SLOT_BLOCK
    tt = blk

    def out_map(e, i, tend):
        prev_end = jnp.where(i > 0, tend[e, jnp.maximum(i - 1, 0)], 0)
        return (e, jnp.clip(prev_end // blk, 0, n_blk - 1), 0, 0)

    grid_spec = pltpu.PrefetchScalarGridSpec(
        num_scalar_prefetch=1,
        grid=(N_EXPERTS, n_t),
        in_specs=[
            pl.BlockSpec((1, 1, 1, tt), lambda e, i, tend: (e, i, 0, 0)),
            pl.BlockSpec((1, 1, 1, tt), lambda e, i, tend: (e, i, 0, 0)),
            pl.BlockSpec((d, tt), lambda e, i, tend: (0, i)),
            pl.BlockSpec((1, d, ff), lambda e, i, tend: (e, 0, 0)),
            pl.BlockSpec((1, d, ff), lambda e, i, tend: (e, 0, 0)),
            pl.BlockSpec((1, ff, d), lambda e, i, tend: (e, 0, 0)),
        ],
        out_specs=pl.BlockSpec((1, sub, d, SLOT_BLOCK), out_map),
        scratch_shapes=[
            pltpu.VMEM((d, 2 * blk), F32),
            pltpu.VMEM((2 * blk, LANES), F32),
        ],
    )
    return pl.pallas_call(
        functools.partial(_ffn_kernel, blk=blk),
        grid_spec=grid_spec,
        out_shape=jax.ShapeDtypeStruct((N_EXPERTS, n_blk * sub, d, SLOT_BLOCK), BF16),
        compiler_params=pltpu.CompilerParams(
            dimension_semantics=("arbitrary", "arbitrary"), vmem_limit_bytes=VMEM_LIMIT),
        name="ffn",
    )(tile_end, spos4, aff4, h2t, wg, wu, wd)


def _slot_span(tend, e, i):
    lo = jnp.where(i > 0, tend[e, jnp.maximum(i - 1, 0)], 0)
    return lo, tend[e, i]


def _combine_kernel(tend_ref, spos_ref, x1_ref, mod_ref, g_ref, b_ref, *refs):
    first = refs[:N_EXPERTS]
    second = refs[N_EXPERTS:2 * N_EXPERTS]
    o_ref, acc_sc = refs[2 * N_EXPERTS:]
    i = pl.program_id(0)
    tt = x1_ref.shape[0]
    slot = lax.broadcasted_iota(I32, (SLOT_BLOCK, tt), 0)

    def onehot(e, block_offset):
        lo, _ = _slot_span(tend_ref, e, i)
        rel = spos_ref[e, 0] - (lo // SLOT_BLOCK + block_offset) * SLOT_BLOCK
        return jnp.where(slot == rel, 1.0, 0.0).astype(BF16)

    acc_sc[...] = jnp.dot(
        jnp.concatenate([first[e][0, 0] for e in range(N_EXPERTS)], axis=1),
        jnp.concatenate([onehot(e, 0) for e in range(N_EXPERTS)], axis=0),
        preferred_element_type=F32)

    for e in range(N_EXPERTS):
        lo, hi = _slot_span(tend_ref, e, i)

        @pl.when(hi > (lo // SLOT_BLOCK + 1) * SLOT_BLOCK)
        def _(e=e):
            acc_sc[...] += jnp.dot(second[e][0, 0], onehot(e, 1), preferred_element_type=F32)

    r = ALPHA * x1_ref[...] + mod_ref[0, 5:6, :] * acc_sc[...].T
    o_ref[...] = _layer_norm(r) * g_ref[...] + b_ref[...]


def _combine_call(tile_end, spos4, ye_t, x1, mod, ln_g, ln_b, tt, tokens_per_batch):
    n, d = x1.shape
    n_t = n // tt
    n_blk = ye_t.shape[1]
    tiles_per_batch = tokens_per_batch // tt

    def first_spec(e):
        def index(i, tend):
            lo, _ = _slot_span(tend, e, i)
            return (e, jnp.clip(lo // SLOT_BLOCK, 0, n_blk - 1), 0, 0)
        return pl.BlockSpec((1, 1, d, SLOT_BLOCK), index)

    def second_spec(e):
        def index(i, tend):
            lo, hi = _slot_span(tend, e, i)
            nxt = lo // SLOT_BLOCK + 1
            spill = (hi > nxt * SLOT_BLOCK) & (nxt < n_blk)
            return (jnp.where(spill, e, 0), jnp.where(spill, jnp.clip(nxt, 0, n_blk - 1), 0), 0, 0)
        return pl.BlockSpec((1, 1, d, SLOT_BLOCK), index)

    grid_spec = pltpu.PrefetchScalarGridSpec(
        num_scalar_prefetch=1,
        grid=(n_t,),
        in_specs=[
            pl.BlockSpec((N_EXPERTS, 1, 1, tt), lambda i, tend: (0, i, 0, 0)),
            pl.BlockSpec((tt, d), lambda i, tend: (i, 0)),
            pl.BlockSpec((1, 6, d), lambda i, tend: (i // tiles_per_batch, 0, 0)),
            pl.BlockSpec((1, d), lambda i, tend: (0, 0)),
            pl.BlockSpec((1, d), lambda i, tend: (0, 0)),
            *[first_spec(e) for e in range(N_EXPERTS)],
            *[second_spec(e) for e in range(N_EXPERTS)],
        ],
        out_specs=pl.BlockSpec((tt, d), lambda i, tend: (i, 0)),
        scratch_shapes=[pltpu.VMEM((d, tt), F32)],
    )
    return pl.pallas_call(
        _combine_kernel,
        grid_spec=grid_spec,
        out_shape=jax.ShapeDtypeStruct((n, d), F32),
        compiler_params=pltpu.CompilerParams(
            dimension_semantics=("parallel",), vmem_limit_bytes=VMEM_LIMIT),
        name="combine",
    )(tile_end, spos4, x1, mod, ln_g, ln_b, *([ye_t] * (2 * N_EXPERTS)))


def _rope_tables(s):
    pos = jnp.arange(s)
    half = HEAD_DIM // 2
    inv_freq = ROPE_THETA ** (-jnp.arange(0, half, 2, dtype=F32) / half)
    ang_r = (pos // GRID_W).astype(F32)[:, None] * inv_freq[None, :]
    ang_c = (pos % GRID_W).astype(F32)[:, None] * inv_freq[None, :]
    zero = jnp.zeros_like(ang_r)
    cos = jnp.concatenate([jnp.cos(ang_r)] * 2 + [jnp.cos(ang_c)] * 2, axis=1)
    slo = jnp.concatenate([-jnp.sin(ang_r), zero, -jnp.sin(ang_c), zero], axis=1)
    shi = jnp.concatenate([zero, jnp.sin(ang_r), zero, jnp.sin(ang_c)], axis=1)
    rep = LANES // HEAD_DIM
    return tuple(jnp.concatenate([t] * rep, axis=1) for t in (cos, slo, shi))


def _bf16_terms(value, n=3):
    terms, rem = [], np.float32(value)
    for _ in range(n):
        part = np.float32(np.asarray(rem, dtype=jnp.bfloat16))
        terms.append(float(part))
        rem = np.float32(rem - part)
    return terms


def _alibi_constants(t):
    cq = np.zeros((N_HEADS_A, LANES), np.float32)
    ck = np.zeros((N_HEADS_A, LANES), np.float32)
    d = jnp.arange(t)
    dist = jnp.abs(d[:, None] - d[None, :]).astype(F32)
    db = []
    for h in range(N_HEADS_A):
        terms = _bf16_terms(2.0 ** (-8.0 * (h + 1) / N_HEADS_A) * LOG2E)
        cq[h, 0:6] = terms + terms
        ck[h, 6:12] = terms + terms
        db.append(-np.float32(sum(terms)) * dist)
    return jnp.asarray(cq), jnp.asarray(ck), jnp.stack(db)


def _tile(dim, target):
    t = min(dim, target)
    assert dim % t == 0, (dim, t)
    return t


def _trunk(x, mod, lam, consts, w):
    b, s, d = x.shape
    n = b * s
    cap = CAPACITY_FACTOR * n // N_EXPERTS
    assert cap % SLOT_BLOCK == 0 and n % LANES == 0
    tm = _tile(s, 512)
    tt = _tile(n, SLOT_BLOCK)

    ta = _tile(s, 512)
    cq, ck, dbias = _alibi_constants(ta)
    qa, ka, va, qb, kb, vb = _proj_call(
        x, mod, w["w_in"], w["gq"], w["gk"], *consts["rope"], consts["bd"], cq, ck, tm)
    n_sub = min(ATTN_SUBTILES, s // ta)
    assert (s // ta) % n_sub == 0, (s, ta, n_sub)
    oa = _diff_attn_call(qa, ka, va, dbias, lam, w["subln_g"], ta, n_sub)
    ob = _gqa_call(qb, kb, vb, _tile(s, 256), ta, n_sub)
    x1, h2t, aff_t = _post_call(oa, ob, x, mod, w["w_out"], w["ln1_g"], w["ln1_b"],
                                w["w_router_t"], tm)

    spos, cnt = _route_call(aff_t.reshape(N_EXPERTS, n // LANES, LANES), cap)
    cnt = cnt.reshape(N_EXPERTS, n)

    def tiles(a, width):
        return a.reshape(N_EXPERTS, n // width, 1, width)

    def tile_ends(width):
        return cnt[:, width - 1::width]

    blk = math.gcd(FFN_BLOCK, cap)
    ye_t = _ffn_call(tile_ends(blk), tiles(spos, blk), tiles(aff_t, blk), h2t,
                     w["w_gate"], w["w_up"], w["w_down"], cap, blk)
    out = _combine_call(tile_ends(tt), tiles(spos, tt), ye_t, x1.reshape(n, d), mod,
                        w["ln2_g"], w["ln2_b"], tt, s)
    return out.reshape(b, s, d)


def kernel(x_prompt, x_sample, c_prompt, c_sample, w_ada, b_ada, w_in, lam_q1, lam_k1, lam_q2,
           lam_k2, subln_g, q_norm_g, k_norm_g, w_out, ln1_g, ln1_b, w_router, w_gate, w_up,
           w_down, ln2_g, ln2_b):
    l = 0
    d = x_prompt.shape[-1]
    c_all = jnp.concatenate([c_prompt, c_sample], axis=0)
    mod, lam = _mod_call(c_all, w_ada[l], b_ada[l][None, :], lam_q1[l][None, :], lam_k1[l][None, :],
                         lam_q2[l][None, :], lam_k2[l][None, :])
    mod = mod.reshape(c_all.shape[0], 6, d)

    w = {
        "w_in": w_in[l].astype(BF16),
        "gq": jnp.tile(q_norm_g[l], N_HEADS_B)[None, :],
        "gk": jnp.tile(k_norm_g[l], N_KV_B)[None, :],
        "subln_g": subln_g[l][None, :],
        "w_out": w_out[l].astype(BF16),
        "ln1_g": ln1_g[l][None, :], "ln1_b": ln1_b[l][None, :],
        "w_router_t": w_router[l].T,
        "w_gate": w_gate[l].astype(BF16), "w_up": w_up[l].astype(BF16),
        "w_down": w_down[l].astype(BF16),
        "ln2_g": ln2_g[l][None, :], "ln2_b": ln2_b[l][None, :],
    }
    head_of = jnp.arange(WIDTH_B) // HEAD_DIM
    shared = {"bd": (head_of[:, None] == head_of[None, :]).astype(BF16)}

    outs = []
    nb = x_prompt.shape[0]
    for x, m in ((x_prompt, mod[:nb]), (x_sample, mod[nb:])):
        consts = dict(shared, rope=_rope_tables(x.shape[1]))
        outs.append(_trunk(x, m, lam, consts, w))
    return tuple(outs)
```

```python
import functools
import math

import jax
import jax.numpy as jnp
import numpy as np
from jax import lax
from jax.experimental import pallas as pl
from jax.experimental.pallas import tpu as pltpu

F32 = jnp.float32
BF16 = jnp.bfloat16
I32 = jnp.int32

HEAD_DIM = 64
N_HEADS_A = 4
WIDTH_A = N_HEADS_A * 2 * HEAD_DIM
N_HEADS_B = 8
N_KV_B = 2
GROUP_B = N_HEADS_B // N_KV_B
WIDTH_B = N_HEADS_B * HEAD_DIM
KV_B = N_KV_B * HEAD_DIM
GRID_W = 64
ROPE_THETA = 10000.0
N_EXPERTS = 16
CAPACITY_FACTOR = 2
DEPTH = 1
ALPHA = (2.0 * DEPTH) ** 0.25
LN_EPS = 1e-5
RMS_EPS = 1e-6
LAM_INIT = 0.8 - 0.6 * math.exp(-0.3 * 0)

LOG2E = 1.4426950408889634
SOFTMAX_CHUNK = 32
N_BIAS_COLS = 12
ATTN_SUBTILES = 8

LANES = 128
SLOT_BLOCK = 256
FFN_BLOCK = 512
VMEM_LIMIT = 56 * 1024 * 1024

_NT = (((1,), (1,)), ((), ()))


def _split_bf16(a):
    hi = a.astype(BF16)
    lo = (a - hi.astype(F32)).astype(BF16)
    return hi, lo


def _dot3(a, b, dims=(((1,), (0,)), ((), ()))):
    ah, al = _split_bf16(a)
    bh, bl = _split_bf16(b)
    d = functools.partial(lax.dot_general, dimension_numbers=dims, preferred_element_type=F32)
    return d(ah, bh) + (d(ah, bl) + d(al, bh))


def _layer_norm(x):
    mu = jnp.mean(x, axis=-1, keepdims=True)
    xc = x - mu
    var = jnp.mean(xc * xc, axis=-1, keepdims=True)
    return xc * lax.rsqrt(var + LN_EPS)


def _lanes(x, n):
    if n == LANES:
        return x
    if n < LANES:
        return x[:, :n]
    return jnp.concatenate([x] * (n // LANES), axis=1)


def _mod_kernel(c_ref, w_ref, b_ref, lq1_ref, lk1_ref, lq2_ref, lk2_ref, mod_ref, lam_ref):
    c = c_ref[...]
    a = c * jax.nn.sigmoid(c)
    mod_ref[...] = _dot3(a, w_ref[...]) + b_ref[...]
    s1 = jnp.sum(lq1_ref[...] * lk1_ref[...], axis=-1, keepdims=True)
    s2 = jnp.sum(lq2_ref[...] * lk2_ref[...], axis=-1, keepdims=True)
    lam = jnp.exp(s1) - jnp.exp(s2) + LAM_INIT
    lam_ref[...] = jnp.broadcast_to(lam, lam_ref.shape)


def _mod_call(c, w_ada, b_ada, lq1, lk1, lq2, lk2):
    bt, d = c.shape
    n_chunks = w_ada.shape[1] // d
    vec = pl.BlockSpec((1, HEAD_DIM), lambda j: (0, 0))
    return pl.pallas_call(
        _mod_kernel,
        grid=(n_chunks,),
        in_specs=[
            pl.BlockSpec((bt, d), lambda j: (0, 0)),
            pl.BlockSpec((d, d), lambda j: (0, j)),
            pl.BlockSpec((1, d), lambda j: (0, j)),
            vec, vec, vec, vec,
        ],
        out_specs=[
            pl.BlockSpec((bt, d), lambda j: (0, j)),
            pl.BlockSpec((1, LANES), lambda j: (0, 0)),
        ],
        out_shape=[
            jax.ShapeDtypeStruct((bt, n_chunks * d), F32),
            jax.ShapeDtypeStruct((1, LANES), F32),
        ],
        compiler_params=pltpu.CompilerParams(vmem_limit_bytes=VMEM_LIMIT),
        name="mod",
    )(c, w_ada, b_ada, lq1, lk1, lq2, lk2)


def _head_rms(x, bd, g):
    sq = x * x
    hi, lo = _split_bf16(sq)
    ss = jnp.dot(hi, bd, preferred_element_type=F32) + jnp.dot(lo, bd, preferred_element_type=F32)
    return x * lax.rsqrt(ss * (1.0 / HEAD_DIM) + RMS_EPS) * g


def _rope(x, cos, sin_lo, sin_hi):
    w = x.shape[1]
    nxt = pltpu.roll(x, w - 16, axis=1)
    prv = pltpu.roll(x, 16, axis=1)
    return x * cos + nxt * sin_lo + prv * sin_hi


def _proj_kernel(x_ref, mod_ref, w_ref, gq_ref, gk_ref, cos_ref, slo_ref, shi_ref, bd_ref, cq_ref, ck_ref,
                 qa_ref, ka_ref, va_ref, qb_ref, kb_ref, vb_ref):
    x = x_ref[0]
    h = _layer_norm(x) * (1.0 + mod_ref[0, 1:2, :]) + mod_ref[0, 0:1, :]
    hb = h.astype(BF16)

    def proj(lo, width):
        return jnp.dot(hb, w_ref[:, lo:lo + width], preferred_element_type=F32)

    scale = HEAD_DIM ** -0.5 * LOG2E
    qa = (proj(0, WIDTH_A) * scale).astype(BF16)
    ka = proj(WIDTH_A, WIDTH_A).astype(BF16)
    hw = 2 * HEAD_DIM
    tm = x.shape[0]
    pos = pl.program_id(1) * tm + lax.broadcasted_iota(I32, (tm, LANES), 0)
    pos_hi = ((pos // LANES) * LANES).astype(F32)
    pos_lo = (pos % LANES).astype(F32)
    lane = lax.broadcasted_iota(I32, (tm, LANES), 1)
    zero = jnp.zeros((tm, LANES), F32)
    for hd in range(N_HEADS_A):
        lo = hd * (hw + LANES)
        bias_q = jnp.where(lane < 6, cq_ref[hd:hd + 1, :],
                           jnp.where(lane < 9, pos_hi, jnp.where(lane < N_BIAS_COLS, pos_lo, zero)))
        bias_k = jnp.where(lane < 3, -pos_hi,
                           jnp.where(lane < 6, -pos_lo, jnp.where(lane < N_BIAS_COLS, ck_ref[hd:hd + 1, :], zero)))
        qa_ref[0, :, lo:lo + hw] = qa[:, hd * hw:(hd + 1) * hw]
        qa_ref[0, :, lo + hw:lo + hw + LANES] = bias_q.astype(BF16)
        ka_ref[0, :, lo:lo + hw] = ka[:, hd * hw:(hd + 1) * hw]
        ka_ref[0, :, lo + hw:lo + hw + LANES] = bias_k.astype(BF16)
    va_ref[0] = proj(2 * WIDTH_A, WIDTH_A).astype(BF16)

    cos, slo, shi = cos_ref[...], slo_ref[...], shi_ref[...]
    bd = bd_ref[...]
    qb = _head_rms(proj(3 * WIDTH_A, WIDTH_B), bd, gq_ref[...])
    qb = _rope(qb, _lanes(cos, WIDTH_B), _lanes(slo, WIDTH_B), _lanes(shi, WIDTH_B)) * scale
    for hd in range(N_HEADS_B):
        qb_ref[0, hd] = qb[:, hd * HEAD_DIM:(hd + 1) * HEAD_DIM].astype(BF16)
    kb = _head_rms(proj(3 * WIDTH_A + WIDTH_B, KV_B), bd[:KV_B, :KV_B], gk_ref[...])
    kb = _rope(kb, cos, slo, shi)
    vb = proj(3 * WIDTH_A + WIDTH_B + KV_B, KV_B)
    ones_col = jnp.where(lane == HEAD_DIM, 1.0, 0.0)
    for hd in range(N_KV_B):
        kb_ref[0, hd] = kb[:, hd * HEAD_DIM:(hd + 1) * HEAD_DIM].astype(BF16)
        v_first = vb if hd == 0 else pltpu.roll(vb, KV_B - hd * HEAD_DIM, axis=1)
        vb_ref[0, hd] = jnp.where(lane < HEAD_DIM, v_first, ones_col).astype(BF16)


def _proj_call(x, mod, w_in, gq, gk, cos, slo, shi, bd, cq, ck, tm):
    b, s, d = x.shape
    nt = s // tm
    wide_a = N_HEADS_A * (2 * HEAD_DIM + LANES)
    bias_tab = pl.BlockSpec((N_HEADS_A, LANES), lambda bi, ti: (0, 0))
    const = lambda shape: pl.BlockSpec(shape, lambda bi, ti: (0,) * len(shape))
    tok = lambda width: pl.BlockSpec((1, tm, width), lambda bi, ti: (bi, ti, 0))
    heads = lambda n, width=HEAD_DIM: pl.BlockSpec((1, n, tm, width), lambda bi, ti: (bi, 0, ti, 0))
    tab = pl.BlockSpec((tm, LANES), lambda bi, ti: (ti, 0))
    return pl.pallas_call(
        _proj_kernel,
        grid=(b, nt),
        in_specs=[
            tok(d),
            pl.BlockSpec((1, 6, d), lambda bi, ti: (bi, 0, 0)),
            const(w_in.shape),
            const(gq.shape), const(gk.shape),
            tab, tab, tab,
            const(bd.shape),
            bias_tab, bias_tab,
        ],
        out_specs=[tok(wide_a), tok(wide_a), tok(WIDTH_A),
                   heads(N_HEADS_B), heads(N_KV_B), heads(N_KV_B, LANES)],
        out_shape=[
            jax.ShapeDtypeStruct((b, s, wide_a), BF16),
            jax.ShapeDtypeStruct((b, s, wide_a), BF16),
            jax.ShapeDtypeStruct((b, s, WIDTH_A), BF16),
            jax.ShapeDtypeStruct((b, N_HEADS_B, s, HEAD_DIM), BF16),
            jax.ShapeDtypeStruct((b, N_KV_B, s, HEAD_DIM), BF16),
            jax.ShapeDtypeStruct((b, N_KV_B, s, LANES), BF16),
        ],
        compiler_params=pltpu.CompilerParams(
            dimension_semantics=("parallel", "parallel"), vmem_limit_bytes=VMEM_LIMIT),
        name="proj",
    )(x, mod, w_in, gq, gk, cos, slo, shi, bd, cq, ck)


def _lane_blocks(x, op):
    out = x[:, :LANES]
    for j in range(1, x.shape[1] // LANES):
        out = op(out, x[:, j * LANES:(j + 1) * LANES])
    return out


def _fold_row_max(s_ref, mt_ref, first):
    rows = s_ref.shape[0]
    for r0 in range(0, rows, SOFTMAX_CHUNK):
        sl = slice(r0, r0 + SOFTMAX_CHUNK)
        part = _lane_blocks(s_ref[sl, :], jnp.maximum)
        mt_ref[sl, :] = part if first else jnp.maximum(mt_ref[sl, :], part)


def _raise_max(mt_ref, m_ref, l_ref, acc_ref):
    rows = m_ref.shape[0]
    width = acc_ref.shape[-1]
    for r0 in range(0, rows, 4 * SOFTMAX_CHUNK):
        sl = slice(r0, r0 + 4 * SOFTMAX_CHUNK)
        m_prev = m_ref[sl, :]
        m_new = jnp.maximum(m_prev, jnp.max(mt_ref[sl, :], axis=1, keepdims=True))
        alpha = jnp.exp2(m_prev - m_new)
        m_ref[sl, :] = m_new
        if l_ref is not None:
            l_ref[sl, :] = alpha * l_ref[sl, :]
        acc_ref[sl, :] = _lanes(alpha, width) * acc_ref[sl, :]


def _write_probs(s_ref, p_ref, col0, m_ref, l_ref):
    rows, tk = s_ref.shape
    for r0 in range(0, rows, SOFTMAX_CHUNK):
        sl = slice(r0, r0 + SOFTMAX_CHUNK)
        p = jnp.exp2(s_ref[sl, :] - _lanes(m_ref[sl, :], tk))
        if l_ref is not None:
            l_ref[sl, :] += _lane_blocks(p, jnp.add)
        p_ref[sl, col0:col0 + tk] = p.astype(BF16)


def _softmax_update(s_ref, p_ref, m_ref, l_ref, acc_ref, bias_ref=None):
    rows, tk = s_ref.shape
    width = acc_ref.shape[-1]
    for r0 in range(0, rows, SOFTMAX_CHUNK):
        sl = slice(r0, r0 + SOFTMAX_CHUNK)
        s = s_ref[sl, :]
        if bias_ref is not None:
            s = s + bias_ref[sl, :]
            s_ref[sl, :] = s
        m_prev = m_ref[sl, :]
        m_new = jnp.maximum(m_prev, jnp.max(s, axis=1, keepdims=True))
        alpha = jnp.exp2(m_prev - m_new)
        m_ref[sl, :] = m_new
        l_ref[sl, :] = alpha * l_ref[sl, :]
        acc_ref[sl, :] = _lanes(alpha, width) * acc_ref[sl, :]
    _write_probs(s_ref, p_ref, 0, m_ref, l_ref)


def _diff_attn_kernel(lam_ref, g_ref, dbias_ref, q_ref, *refs, n_sub, n_kt):
    k_refs, v_refs = refs[:n_sub], refs[n_sub:2 * n_sub]
    o_ref, qv_sc, s_sc, p_sc, m_sc, l_sc, acc_sc = refs[2 * n_sub:]
    qi = pl.program_id(2)
    ki = pl.program_id(3)
    hw = 2 * HEAD_DIM

    @pl.when(ki == 0)
    def _():
        m_sc[...] = jnp.full(m_sc.shape, -jnp.inf, F32)
        l_sc[...] = jnp.zeros(l_sc.shape, F32)
        acc_sc[...] = jnp.zeros(acc_sc.shape, F32)
        q = q_ref[0]
        zero = jnp.zeros_like(q)
        lane = lax.broadcasted_iota(I32, q.shape, 1)
        is_bias = lane >= hw
        for c in range(2):
            qc = jnp.where(is_bias | ((lane >= c * HEAD_DIM) & (lane < (c + 1) * HEAD_DIM)), q, zero)
            qv_sc[0, c] = qc
            qv_sc[1, c] = jnp.where(is_bias, -qc, qc)
            qv_sc[2, c] = jnp.where(is_bias, zero, qc)

    def body(ends_on_diagonal):
        for j in range(n_sub):
            diagonal = ends_on_diagonal and j == n_sub - 1
            kt = (qi + 1 + ki * n_sub + j) % n_kt
            version = 2 if diagonal else jnp.where(kt > qi, 0, 1)
            for c in range(2):
                s_sc[j, c] = lax.dot_general(qv_sc[version, c], k_refs[j][0], _NT,
                                             preferred_element_type=F32)
        for j in range(n_sub):
            diagonal = ends_on_diagonal and j == n_sub - 1
            for c in range(2):
                _softmax_update(s_sc.at[j, c], p_sc.at[j, c], m_sc.at[c], l_sc.at[c], acc_sc.at[c],
                                dbias_ref.at[0] if diagonal else None)
                acc_sc[c] += jnp.dot(p_sc[j, c], v_refs[j][0], preferred_element_type=F32)

    last = pl.num_programs(3) - 1

    @pl.when(ki != last)
    def _():
        body(False)

    @pl.when(ki == last)
    def _():
        body(True)

    @pl.when(ki == last)
    def _():
        l0 = jnp.sum(l_sc[0], axis=1, keepdims=True)
        l1 = jnp.sum(l_sc[1], axis=1, keepdims=True)
        o = acc_sc[0] / l0 - lam_ref[...] * (acc_sc[1] / l1)
        ms = jnp.mean(o * o, axis=-1, keepdims=True)
        o = o * lax.rsqrt(ms + RMS_EPS) * g_ref[...] * (1.0 - LAM_INIT)
        o_ref[0] = o.astype(o_ref.dtype)


def _diff_attn_call(qa, ka, va, dbias, lam, g, t, n_sub):
    b, s, _ = va.shape
    hw = 2 * HEAD_DIM
    qw = hw + LANES
    n_kt = s // t

    def kv_spec(width, j):
        return pl.BlockSpec(
            (1, t, width), lambda bi, h, qi, ki: (bi, (qi + 1 + ki * n_sub + j) % n_kt, h))

    return pl.pallas_call(
        functools.partial(_diff_attn_kernel, n_sub=n_sub, n_kt=n_kt),
        grid=(b, N_HEADS_A, n_kt, n_kt // n_sub),
        in_specs=[
            pl.BlockSpec((1, LANES), lambda bi, h, qi, ki: (0, 0)),
            pl.BlockSpec((1, hw), lambda bi, h, qi, ki: (0, 0)),
            pl.BlockSpec((1, t, t), lambda bi, h, qi, ki: (h, 0, 0)),
            pl.BlockSpec((1, t, qw), lambda bi, h, qi, ki: (bi, qi, h)),
            *[kv_spec(qw, j) for j in range(n_sub)],
            *[kv_spec(hw, j) for j in range(n_sub)],
        ],
        out_specs=pl.BlockSpec((1, t, hw), lambda bi, h, qi, ki: (bi, qi, h)),
        out_shape=jax.ShapeDtypeStruct((b, s, WIDTH_A), BF16),
        scratch_shapes=[
            pltpu.VMEM((3, 2, t, qw), BF16),
            pltpu.VMEM((n_sub, 2, t, t), F32),
            pltpu.VMEM((n_sub, 2, t, t), BF16),
            pltpu.VMEM((2, t, LANES), F32),
            pltpu.VMEM((2, t, LANES), F32),
            pltpu.VMEM((2, t, hw), F32),
        ],
        compiler_params=pltpu.CompilerParams(
            dimension_semantics=("parallel", "parallel", "parallel", "arbitrary"),
            vmem_limit_bytes=VMEM_LIMIT),
        name="diff_attn",
    )(lam, g, dbias, qa, *([ka] * n_sub), *([va] * n_sub))


def _gqa_kernel(q_ref, k_ref, v_ref, o_ref, s_sc, p_sc, mt_sc, m_sc, acc_sc, *, tq, tk):
    ki = pl.program_id(3)
    rows = GROUP_B * tq

    @pl.when(ki == 0)
    def _():
        m_sc[...] = jnp.full(m_sc.shape, -jnp.inf, F32)
        acc_sc[...] = jnp.zeros(acc_sc.shape, F32)

    n_sub = k_ref.shape[2] // tk
    q = q_ref[0].reshape(rows, HEAD_DIM)
    for j in range(n_sub):
        s_sc[j] = lax.dot_general(q, k_ref[0, 0, j * tk:(j + 1) * tk, :], _NT,
                                  preferred_element_type=F32)
    for j in range(n_sub):
        _fold_row_max(s_sc.at[j], mt_sc, j == 0)
    _raise_max(mt_sc, m_sc, None, acc_sc)
    for j in range(n_sub):
        _write_probs(s_sc.at[j], p_sc, j * tk, m_sc, None)
    acc_sc[...] += jnp.dot(p_sc[...], v_ref[0, 0], preferred_element_type=F32)

    @pl.when(ki == pl.num_programs(3) - 1)
    def _():
        acc = acc_sc[...]
        o = acc[:, :HEAD_DIM] / acc[:, HEAD_DIM:HEAD_DIM + 1]
        o_ref[0] = o.reshape(GROUP_B, tq, HEAD_DIM).astype(o_ref.dtype)


def _gqa_call(qb, kb, vb, tq, tk, n_sub):
    b, _, s, _ = qb.shape
    rows = GROUP_B * tq
    kv = n_sub * tk
    return pl.pallas_call(
        functools.partial(_gqa_kernel, tq=tq, tk=tk),
        grid=(b, N_KV_B, s // tq, s // kv),
        in_specs=[
            pl.BlockSpec((1, GROUP_B, tq, HEAD_DIM), lambda bi, n, qi, ki: (bi, n, qi, 0)),
            pl.BlockSpec((1, 1, kv, HEAD_DIM), lambda bi, n, qi, ki: (bi, n, ki, 0)),
            pl.BlockSpec((1, 1, kv, LANES), lambda bi, n, qi, ki: (bi, n, ki, 0)),
        ],
        out_specs=pl.BlockSpec((1, GROUP_B, tq, HEAD_DIM), lambda bi, n, qi, ki: (bi, n, qi, 0)),
        out_shape=jax.ShapeDtypeStruct((b, N_HEADS_B, s, HEAD_DIM), BF16),
        scratch_shapes=[
            pltpu.VMEM((n_sub, rows, tk), F32),
            pltpu.VMEM((rows, n_sub * tk), BF16),
            pltpu.VMEM((rows, LANES), F32),
            pltpu.VMEM((rows, LANES), F32),
            pltpu.VMEM((rows, LANES), F32),
        ],
        compiler_params=pltpu.CompilerParams(
            dimension_semantics=("parallel", "parallel", "parallel", "arbitrary"),
            vmem_limit_bytes=VMEM_LIMIT),
        name="gqa_attn",
    )(qb, kb, vb)


def _post_kernel(oa_ref, ob_ref, x_ref, mod_ref, w_ref, g_ref, b_ref, wrt_ref,
                 x1_ref, h2_ref, aff_ref):
    o = jnp.dot(oa_ref[0], w_ref[:WIDTH_A, :], preferred_element_type=F32)
    for hd in range(N_HEADS_B):
        lo = WIDTH_A + hd * HEAD_DIM
        o = o + jnp.dot(ob_ref[0, hd], w_ref[lo:lo + HEAD_DIM, :], preferred_element_type=F32)
    r = ALPHA * x_ref[0] + mod_ref[0, 2:3, :] * o
    x1 = _layer_norm(r) * g_ref[...] + b_ref[...]
    x1_ref[0] = x1
    h2 = _layer_norm(x1) * (1.0 + mod_ref[0, 4:5, :]) + mod_ref[0, 3:4, :]
    h2_ref[...] = h2.T.astype(BF16)
    logits = _dot3(wrt_ref[...], h2, _NT)
    z = jnp.exp(logits - jnp.max(logits, axis=0, keepdims=True))
    aff_ref[...] = z / jnp.sum(z, axis=0, keepdims=True)


def _post_call(oa, ob, x, mod, w_out, ln_g, ln_b, w_router_t, tm):
    b, s, d = x.shape
    nt = s // tm
    const = lambda shape: pl.BlockSpec(shape, lambda bi, ti: (0,) * len(shape))
    tok = lambda width: pl.BlockSpec((1, tm, width), lambda bi, ti: (bi, ti, 0))
    return pl.pallas_call(
        _post_kernel,
        grid=(b, nt),
        in_specs=[
            tok(WIDTH_A),
            pl.BlockSpec((1, N_HEADS_B, tm, HEAD_DIM), lambda bi, ti: (bi, 0, ti, 0)),
            tok(d),
            pl.BlockSpec((1, 6, d), lambda bi, ti: (bi, 0, 0)),
            const(w_out.shape), const(ln_g.shape), const(ln_b.shape), const(w_router_t.shape),
        ],
        out_specs=[tok(d),
                   pl.BlockSpec((d, tm), lambda bi, ti: (0, bi * nt + ti)),
                   pl.BlockSpec((N_EXPERTS, tm), lambda bi, ti: (0, bi * nt + ti))],
        out_shape=[
            jax.ShapeDtypeStruct((b, s, d), F32),
            jax.ShapeDtypeStruct((d, b * s), BF16),
            jax.ShapeDtypeStruct((N_EXPERTS, b * s), F32),
        ],
        compiler_params=pltpu.CompilerParams(
            dimension_semantics=("parallel", "parallel"), vmem_limit_bytes=VMEM_LIMIT),
        name="post",
    )(oa, ob, x, mod, w_out, ln_g, ln_b, w_router_t)


def _route_kernel(aff_ref, spos_ref, cnt_ref, *, cap, idx_bits):
    aff = aff_ref[...]
    n_e, n_r, _ = aff.shape
    bits = pltpu.bitcast(aff, I32)
    idx = (lax.broadcasted_iota(I32, aff.shape, 1) * LANES
           + lax.broadcasted_iota(I32, aff.shape, 2))

    def count(pred):
        part = jnp.sum(jnp.where(pred, 1.0, 0.0), axis=1, keepdims=True)
        return jnp.sum(part, axis=2, keepdims=True)

    def value_step(t, prefix):
        cand = prefix | (1 << (30 - t))
        return jnp.where(count(bits >= cand) >= cap, cand, prefix)

    thr = lax.fori_loop(0, 31, value_step, jnp.zeros((n_e, 1, 1), I32))
    above = bits > thr
    tie = bits == thr
    need = cap - count(above)

    def index_step(t, prefix):
        cand = prefix | (1 << (idx_bits - 1 - t))
        return jnp.where(count(tie & (idx < cand)) < need, cand, prefix)

    last = lax.fori_loop(0, idx_bits, index_step, jnp.zeros((n_e, 1, 1), I32))
    sel = above | (tie & (idx <= last))

    upper = (lax.broadcasted_iota(I32, (LANES, LANES), 0)
             <= lax.broadcasted_iota(I32, (LANES, LANES), 1)).astype(BF16)
    strict_lower = (lax.broadcasted_iota(I32, (n_r, n_r), 1)
                    < lax.broadcasted_iota(I32, (n_r, n_r), 0)).astype(BF16)
    for e in range(n_e):
        sel_e = sel[e]
        m = sel_e.astype(BF16)
        incl = jnp.dot(m, upper, preferred_element_type=F32)
        row_tot = jnp.broadcast_to(incl[:, LANES - 1:], incl.shape).astype(BF16)
        offs = jnp.dot(strict_lower, row_tot, preferred_element_type=F32)
        cnt = (incl + offs).astype(I32)
        cnt_ref[e] = cnt
        spos_ref[e] = jnp.where(sel_e, cnt - 1, -1)


def _route_call(aff3, cap):
    n_e, n_r, _ = aff3.shape
    idx_bits = max(1, (n_r * LANES - 1).bit_length())
    return pl.pallas_call(
        functools.partial(_route_kernel, cap=cap, idx_bits=idx_bits),
        out_shape=[jax.ShapeDtypeStruct(aff3.shape, I32), jax.ShapeDtypeStruct(aff3.shape, I32)],
        compiler_params=pltpu.CompilerParams(vmem_limit_bytes=VMEM_LIMIT),
        name="route",
    )(aff3)


def _ffn_kernel(tend_ref, spos_ref, g_ref, x_ref, wg_ref, wu_ref, wd_ref, ye_ref,
                stage_sc, gate_sc, *, blk):
    e = pl.program_id(0)
    i = pl.program_id(1)
    prev_end = jnp.where(i > 0, tend_ref[e, jnp.maximum(i - 1, 0)], 0)
    base = (prev_end // blk) * blk
    fill = tend_ref[e, i] - base

    @pl.when(i == 0)
    def _():
        stage_sc[...] = jnp.zeros(stage_sc.shape, F32)
        gate_sc[...] = jnp.zeros(gate_sc.shape, F32)

    rel = spos_ref[0, 0] - base
    slot = lax.broadcasted_iota(I32, (blk, blk), 0)

    def place(lo, rel_lo):
        hit = slot == rel_lo
        onehot = jnp.where(hit, 1.0, 0.0).astype(BF16)
        cols = slice(lo, lo + blk)
        stage_sc[:, cols] += lax.dot_general(x_ref[...], onehot, _NT, preferred_element_type=F32)
        gate_sc[cols, :] += jnp.sum(jnp.where(hit, g_ref[0, 0], 0.0), axis=1, keepdims=True)

    place(0, rel)

    @pl.when(fill > blk)
    def _():
        place(blk, rel - blk)

    @pl.when(fill >= blk)
    def _():
        xb = stage_sc[:, :blk].T.astype(BF16)
        gate = jnp.dot(xb, wg_ref[0], preferred_element_type=F32)
        up = jnp.dot(xb, wu_ref[0], preferred_element_type=F32)
        hid = (gate * jax.nn.sigmoid(gate) * up).astype(BF16)
        y = jnp.dot(hid, wd_ref[0], preferred_element_type=F32) * gate_sc[:blk, :1]
        for w in range(blk // SLOT_BLOCK):
            ye_ref[0, w] = y[w * SLOT_BLOCK:(w + 1) * SLOT_BLOCK, :].T.astype(ye_ref.dtype)
        stage_sc[:, :blk] = stage_sc[:, blk:]
        stage_sc[:, blk:] = jnp.zeros((stage_sc.shape[0], blk), F32)
        gate_sc[:blk, :] = gate_sc[blk:, :]
        gate_sc[blk:, :] = jnp.zeros((blk, LANES), F32)


def _ffn_call(tile_end, spos4, aff4, h2t, wg, wu, wd, cap, blk):
    d, n = h2t.shape
    n_t = n // blk
    ff = wg.shape[2]
    n_blk = cap // blk
    sub = blk // SLOT_BLOCK
    tt = blk

    def out_map(e, i, tend):
        prev_end = jnp.where(i > 0, tend[e, jnp.maximum(i - 1, 0)], 0)
        return (e, jnp.clip(prev_end // blk, 0, n_blk - 1), 0, 0)

    grid_spec = pltpu.PrefetchScalarGridSpec(
        num_scalar_prefetch=1,
        grid=(N_EXPERTS, n_t),
        in_specs=[
            pl.BlockSpec((1, 1, 1, tt), lambda e, i, tend: (e, i, 0, 0)),
            pl.BlockSpec((1, 1, 1, tt), lambda e, i, tend: (e, i, 0, 0)),
            pl.BlockSpec((d, tt), lambda e, i, tend: (0, i)),
            pl.BlockSpec((1, d, ff), lambda e, i, tend: (e, 0, 0)),
            pl.BlockSpec((1, d, ff), lambda e, i, tend: (e, 0, 0)),
            pl.BlockSpec((1, ff, d), lambda e, i, tend: (e, 0, 0)),
        ],
        out_specs=pl.BlockSpec((1, sub, d, SLOT_BLOCK), out_map),
        scratch_shapes=[
            pltpu.VMEM((d, 2 * blk), F32),
            pltpu.VMEM((2 * blk, LANES), F32),
        ],
    )
    return pl.pallas_call(
        functools.partial(_ffn_kernel, blk=blk),
        grid_spec=grid_spec,
        out_shape=jax.ShapeDtypeStruct((N_EXPERTS, n_blk * sub, d, SLOT_BLOCK), BF16),
        compiler_params=pltpu.CompilerParams(
            dimension_semantics=("arbitrary", "arbitrary"), vmem_limit_bytes=VMEM_LIMIT),
        name="ffn",
    )(tile_end, spos4, aff4, h2t, wg, wu, wd)


def _slot_span(tend, e, i):
    lo = jnp.where(i > 0, tend[e, jnp.maximum(i - 1, 0)], 0)
    return lo, tend[e, i]


def _combine_kernel(tend_ref, spos_ref, x1_ref, mod_ref, g_ref, b_ref, *refs):
    first = refs[:N_EXPERTS]
    second = refs[N_EXPERTS:2 * N_EXPERTS]
    o_ref, acc_sc = refs[2 * N_EXPERTS:]
    i = pl.program_id(0)
    tt = x1_ref.shape[0]
    slot = lax.broadcasted_iota(I32, (SLOT_BLOCK, tt), 0)

    def onehot(e, block_offset):
        lo, _ = _slot_span(tend_ref, e, i)
        rel = spos_ref[e, 0] - (lo // SLOT_BLOCK + block_offset) * SLOT_BLOCK
        return jnp.where(slot == rel, 1.0, 0.0).astype(BF16)

    acc_sc[...] = jnp.dot(
        jnp.concatenate([first[e][0, 0] for e in range(N_EXPERTS)], axis=1),
        jnp.concatenate([onehot(e, 0) for e in range(N_EXPERTS)], axis=0),
        preferred_element_type=F32)

    for e in range(N_EXPERTS):
        lo, hi = _slot_span(tend_ref, e, i)

        @pl.when(hi > (lo // SLOT_BLOCK + 1) * SLOT_BLOCK)
        def _(e=e):
            acc_sc[...] += jnp.dot(second[e][0, 0], onehot(e, 1), preferred_element_type=F32)

    r = ALPHA * x1_ref[...] + mod_ref[0, 5:6, :] * acc_sc[...].T
    o_ref[...] = _layer_norm(r) * g_ref[...] + b_ref[...]


def _combine_call(tile_end, spos4, ye_t, x1, mod, ln_g, ln_b, tt, tokens_per_batch):
    n, d = x1.shape
    n_t = n // tt
    n_blk = ye_t.shape[1]
    tiles_per_batch = tokens_per_batch // tt

    def first_spec(e):
        def index(i, tend):
            lo, _ = _slot_span(tend, e, i)
            return (e, jnp.clip(lo // SLOT_BLOCK, 0, n_blk - 1), 0, 0)
        return pl.BlockSpec((1, 1, d, SLOT_BLOCK), index)

    def second_spec(e):
        def index(i, tend):
            lo, hi = _slot_span(tend, e, i)
            nxt = lo // SLOT_BLOCK + 1
            spill = (hi > nxt * SLOT_BLOCK) & (nxt < n_blk)
            return (jnp.where(spill, e, 0), jnp.where(spill, jnp.clip(nxt, 0, n_blk - 1), 0), 0, 0)
        return pl.BlockSpec((1, 1, d, SLOT_BLOCK), index)

    grid_spec = pltpu.PrefetchScalarGridSpec(
        num_scalar_prefetch=1,
        grid=(n_t,),
        in_specs=[
            pl.BlockSpec((N_EXPERTS, 1, 1, tt), lambda i, tend: (0, i, 0, 0)),
            pl.BlockSpec((tt, d), lambda i, tend: (i, 0)),
            pl.BlockSpec((1, 6, d), lambda i, tend: (i // tiles_per_batch, 0, 0)),
            pl.BlockSpec((1, d), lambda i, tend: (0, 0)),
            pl.BlockSpec((1, d), lambda i, tend: (0, 0)),
            *[first_spec(e) for e in range(N_EXPERTS)],
            *[second_spec(e) for e in range(N_EXPERTS)],
        ],
        out_specs=pl.BlockSpec((tt, d), lambda i, tend: (i, 0)),
        scratch_shapes=[pltpu.VMEM((d, tt), F32)],
    )
    return pl.pallas_call(
        _combine_kernel,
        grid_spec=grid_spec,
        out_shape=jax.ShapeDtypeStruct((n, d), F32),
        compiler_params=pltpu.CompilerParams(
            dimension_semantics=("parallel",), vmem_limit_bytes=VMEM_LIMIT),
        name="combine",
    )(tile_end, spos4, x1, mod, ln_g, ln_b, *([ye_t] * (2 * N_EXPERTS)))


def _rope_tables(s):
    pos = jnp.arange(s)
    half = HEAD_DIM // 2
    inv_freq = ROPE_THETA ** (-jnp.arange(0, half, 2, dtype=F32) / half)
    ang_r = (pos // GRID_W).astype(F32)[:, None] * inv_freq[None, :]
    ang_c = (pos % GRID_W).astype(F32)[:, None] * inv_freq[None, :]
    zero = jnp.zeros_like(ang_r)
    cos = jnp.concatenate([jnp.cos(ang_r)] * 2 + [jnp.cos(ang_c)] * 2, axis=1)
    slo = jnp.concatenate([-jnp.sin(ang_r), zero, -jnp.sin(ang_c), zero], axis=1)
    shi = jnp.concatenate([zero, jnp.sin(ang_r), zero, jnp.sin(ang_c)], axis=1)
    rep = LANES // HEAD_DIM
    return tuple(jnp.concatenate([t] * rep, axis=1) for t in (cos, slo, shi))


def _bf16_terms(value, n=3):
    terms, rem = [], np.float32(value)
    for _ in range(n):
        part = np.float32(np.asarray(rem, dtype=jnp.bfloat16))
        terms.append(float(part))
        rem = np.float32(rem - part)
    return terms


def _alibi_constants(t):
    cq = np.zeros((N_HEADS_A, LANES), np.float32)
    ck = np.zeros((N_HEADS_A, LANES), np.float32)
    d = jnp.arange(t)
    dist = jnp.abs(d[:, None] - d[None, :]).astype(F32)
    db = []
    for h in range(N_HEADS_A):
        terms = _bf16_terms(2.0 ** (-8.0 * (h + 1) / N_HEADS_A) * LOG2E)
        cq[h, 0:6] = terms + terms
        ck[h, 6:12] = terms + terms
        db.append(-np.float32(sum(terms)) * dist)
    return jnp.asarray(cq), jnp.asarray(ck), jnp.stack(db)


def _tile(dim, target):
    t = min(dim, target)
    assert dim % t == 0, (dim, t)
    return t


def _trunk(x, mod, lam, consts, w):
    b, s, d = x.shape
    n = b * s
    cap = CAPACITY_FACTOR * n // N_EXPERTS
    assert cap % SLOT_BLOCK == 0 and n % LANES == 0
    tm = _tile(s, 512)
    tt = _tile(n, SLOT_BLOCK)

    ta = _tile(s, 512)
    cq, ck, dbias = _alibi_constants(ta)
    qa, ka, va, qb, kb, vb = _proj_call(
        x, mod, w["w_in"], w["gq"], w["gk"], *consts["rope"], consts["bd"], cq, ck, tm)
    n_sub = min(ATTN_SUBTILES, s // ta)
    assert (s // ta) % n_sub == 0, (s, ta, n_sub)
    oa = _diff_attn_call(qa, ka, va, dbias, lam, w["subln_g"], ta, n_sub)
    ob = _gqa_call(qb, kb, vb, _tile(s, 256), ta, n_sub)
    x1, h2t, aff_t = _post_call(oa, ob, x, mod, w["w_out"], w["ln1_g"], w["ln1_b"],
                                w["w_router_t"], tm)

    spos, cnt = _route_call(aff_t.reshape(N_EXPERTS, n // LANES, LANES), cap)
    cnt = cnt.reshape(N_EXPERTS, n)

    def tiles(a, width):
        return a.reshape(N_EXPERTS, n // width, 1, width)

    def tile_ends(width):
        return cnt[:, width - 1::width]

    blk = math.gcd(FFN_BLOCK, cap)
    ye_t = _ffn_call(tile_ends(blk), tiles(spos, blk), tiles(aff_t, blk), h2t,
                     w["w_gate"], w["w_up"], w["w_down"], cap, blk)
    out = _combine_call(tile_ends(tt), tiles(spos, tt), ye_t, x1.reshape(n, d), mod,
                        w["ln2_g"], w["ln2_b"], tt, s)
    return out.reshape(b, s, d)


def kernel(x_prompt, x_sample, c_prompt, c_sample, w_ada, b_ada, w_in, lam_q1, lam_k1, lam_q2,
           lam_k2, subln_g, q_norm_g, k_norm_g, w_out, ln1_g, ln1_b, w_router, w_gate, w_up,
           w_down, ln2_g, ln2_b):
    l = 0
    d = x_prompt.shape[-1]
    c_all = jnp.concatenate([c_prompt, c_sample], axis=0)
    mod, lam = _mod_call(c_all, w_ada[l], b_ada[l][None, :], lam_q1[l][None, :], lam_k1[l][None, :],
                         lam_q2[l][None, :], lam_k2[l][None, :])
    mod = mod.reshape(c_all.shape[0], 6, d)

    w = {
        "w_in": w_in[l].astype(BF16),
        "gq": jnp.tile(q_norm_g[l], N_HEADS_B)[None, :],
        "gk": jnp.tile(k_norm_g[l], N_KV_B)[None, :],
        "subln_g": subln_g[l][None, :],
        "w_out": w_out[l].astype(BF16),
        "ln1_g": ln1_g[l][None, :], "ln1_b": ln1_b[l][None, :],
        "w_router_t": w_router[l].T,
        "w_gate": w_gate[l].astype(BF16), "w_up": w_up[l].astype(BF16),
        "w_down": w_down[l].astype(BF16),
        "ln2_g": ln2_g[l][None, :], "ln2_b": ln2_b[l][None, :],
    }
    head_of = jnp.arange(WIDTH_B) // HEAD_DIM
    shared = {"bd": (head_of[:, None] == head_of[None, :]).astype(BF16)}

    outs = []
    nb = x_prompt.shape[0]
    for x, m in ((x_prompt, mod[:nb]), (x_sample, mod[nb:])):
        consts = dict(shared, rope=_rope_tables(x.shape[1]))
        outs.append(_trunk(x, m, lam, consts, w))
    return tuple(outs)
```

```python
import functools
import math

import jax
import jax.numpy as jnp
import numpy as np
from jax import lax
from jax.experimental import pallas as pl
from jax.experimental.pallas import tpu as pltpu

F32 = jnp.float32
BF16 = jnp.bfloat16
I32 = jnp.int32

HEAD_DIM = 64
N_HEADS_A = 4
WIDTH_A = N_HEADS_A * 2 * HEAD_DIM
N_HEADS_B = 8
N_KV_B = 2
GROUP_B = N_HEADS_B // N_KV_B
WIDTH_B = N_HEADS_B * HEAD_DIM
KV_B = N_KV_B * HEAD_DIM
GRID_W = 64
ROPE_THETA = 10000.0
N_EXPERTS = 16
CAPACITY_FACTOR = 2
DEPTH = 1
ALPHA = (2.0 * DEPTH) ** 0.25
LN_EPS = 1e-5
RMS_EPS = 1e-6
LAM_INIT = 0.8 - 0.6 * math.exp(-0.3 * 0)

LOG2E = 1.4426950408889634
SOFTMAX_CHUNK = 32
SKIP_EXPONENT = 170.0
N_BIAS_COLS = 12
ATTN_SUBTILES = 8

LANES = 128
SLOT_BLOCK = 256
FFN_BLOCK = 512
VMEM_LIMIT = 56 * 1024 * 1024

_NT = (((1,), (1,)), ((), ()))


def _split_bf16(a):
    hi = a.astype(BF16)
    lo = (a - hi.astype(F32)).astype(BF16)
    return hi, lo


def _dot3(a, b, dims=(((1,), (0,)), ((), ()))):
    ah, al = _split_bf16(a)
    bh, bl = _split_bf16(b)
    d = functools.partial(lax.dot_general, dimension_numbers=dims, preferred_element_type=F32)
    return d(ah, bh) + (d(ah, bl) + d(al, bh))


def _layer_norm(x):
    mu = jnp.mean(x, axis=-1, keepdims=True)
    xc = x - mu
    var = jnp.mean(xc * xc, axis=-1, keepdims=True)
    return xc * lax.rsqrt(var + LN_EPS)


def _lanes(x, n):
    if n == LANES:
        return x
    if n < LANES:
        return x[:, :n]
    return jnp.concatenate([x] * (n // LANES), axis=1)


def _mod_kernel(c_ref, w_ref, b_ref, lq1_ref, lk1_ref, lq2_ref, lk2_ref, mod_ref, lam_ref):
    c = c_ref[...]
    a = c * jax.nn.sigmoid(c)
    mod_ref[...] = _dot3(a, w_ref[...]) + b_ref[...]
    s1 = jnp.sum(lq1_ref[...] * lk1_ref[...], axis=-1, keepdims=True)
    s2 = jnp.sum(lq2_ref[...] * lk2_ref[...], axis=-1, keepdims=True)
    lam = jnp.exp(s1) - jnp.exp(s2) + LAM_INIT
    lam_ref[...] = jnp.broadcast_to(lam, lam_ref.shape)


def _mod_call(c, w_ada, b_ada, lq1, lk1, lq2, lk2):
    bt, d = c.shape
    n_chunks = w_ada.shape[1] // d
    vec = pl.BlockSpec((1, HEAD_DIM), lambda j: (0, 0))
    return pl.pallas_call(
        _mod_kernel,
        grid=(n_chunks,),
        in_specs=[
            pl.BlockSpec((bt, d), lambda j: (0, 0)),
            pl.BlockSpec((d, d), lambda j: (0, j)),
            pl.BlockSpec((1, d), lambda j: (0, j)),
            vec, vec, vec, vec,
        ],
        out_specs=[
            pl.BlockSpec((bt, d), lambda j: (0, j)),
            pl.BlockSpec((1, LANES), lambda j: (0, 0)),
        ],
        out_shape=[
            jax.ShapeDtypeStruct((bt, n_chunks * d), F32),
            jax.ShapeDtypeStruct((1, LANES), F32),
        ],
        compiler_params=pltpu.CompilerParams(vmem_limit_bytes=VMEM_LIMIT),
        name="mod",
    )(c, w_ada, b_ada, lq1, lk1, lq2, lk2)


def _head_rms(x, bd, g):
    sq = x * x
    hi, lo = _split_bf16(sq)
    ss = jnp.dot(hi, bd, preferred_element_type=F32) + jnp.dot(lo, bd, preferred_element_type=F32)
    return x * lax.rsqrt(ss * (1.0 / HEAD_DIM) + RMS_EPS) * g


def _rope(x, cos, sin_lo, sin_hi):
    w = x.shape[1]
    nxt = pltpu.roll(x, w - 16, axis=1)
    prv = pltpu.roll(x, 16, axis=1)
    return x * cos + nxt * sin_lo + prv * sin_hi


def _proj_kernel(x_ref, mod_ref, w_ref, gq_ref, gk_ref, cos_ref, slo_ref, shi_ref, bd_ref, cq_ref, ck_ref,
                 qa_ref, ka_ref, va_ref, qb_ref, kb_ref, vb_ref, norm_ref):
    x = x_ref[0]
    h = _layer_norm(x) * (1.0 + mod_ref[0, 1:2, :]) + mod_ref[0, 0:1, :]
    hb = h.astype(BF16)

    def proj(lo, width):
        return jnp.dot(hb, w_ref[:, lo:lo + width], preferred_element_type=F32)

    scale = HEAD_DIM ** -0.5 * LOG2E
    qa = (proj(0, WIDTH_A) * scale).astype(BF16)
    ka = proj(WIDTH_A, WIDTH_A).astype(BF16)
    hw = 2 * HEAD_DIM
    tm = x.shape[0]
    pos = pl.program_id(1) * tm + lax.broadcasted_iota(I32, (tm, LANES), 0)
    pos_hi = ((pos // LANES) * LANES).astype(F32)
    pos_lo = (pos % LANES).astype(F32)
    lane = lax.broadcasted_iota(I32, (tm, LANES), 1)
    zero = jnp.zeros((tm, LANES), F32)
    for hd in range(N_HEADS_A):
        lo = hd * (hw + LANES)
        bias_q = jnp.where(lane < 6, cq_ref[hd:hd + 1, :],
                           jnp.where(lane < 9, pos_hi, jnp.where(lane < N_BIAS_COLS, pos_lo, zero)))
        bias_k = jnp.where(lane < 3, -pos_hi,
                           jnp.where(lane < 6, -pos_lo, jnp.where(lane < N_BIAS_COLS, ck_ref[hd:hd + 1, :], zero)))
        qa_ref[0, :, lo:lo + hw] = qa[:, hd * hw:(hd + 1) * hw]
        qa_ref[0, :, lo + hw:lo + hw + LANES] = bias_q.astype(BF16)
        ka_ref[0, :, lo:lo + hw] = ka[:, hd * hw:(hd + 1) * hw]
        ka_ref[0, :, lo + hw:lo + hw + LANES] = bias_k.astype(BF16)
        for row, val in ((hd, qa), (N_HEADS_A + hd, ka)):
            v32 = val[:, hd * hw:(hd + 1) * hw].astype(F32)
            n2 = jnp.max(jnp.sum(v32 * v32, axis=1, keepdims=True), axis=0, keepdims=True)
            norm_ref[0, 0, row:row + 1, :] = jnp.broadcast_to(n2, (1, LANES))
    va_ref[0] = proj(2 * WIDTH_A, WIDTH_A).astype(BF16)

    cos, slo, shi = cos_ref[...], slo_ref[...], shi_ref[...]
    bd = bd_ref[...]
    qb = _head_rms(proj(3 * WIDTH_A, WIDTH_B), bd, gq_ref[...])
    qb = _rope(qb, _lanes(cos, WIDTH_B), _lanes(slo, WIDTH_B), _lanes(shi, WIDTH_B)) * scale
    for hd in range(N_HEADS_B):
        qb_ref[0, hd] = qb[:, hd * HEAD_DIM:(hd + 1) * HEAD_DIM].astype(BF16)
    kb = _head_rms(proj(3 * WIDTH_A + WIDTH_B, KV_B), bd[:KV_B, :KV_B], gk_ref[...])
    kb = _rope(kb, cos, slo, shi)
    vb = proj(3 * WIDTH_A + WIDTH_B + KV_B, KV_B)
    ones_col = jnp.where(lane == HEAD_DIM, 1.0, 0.0)
    for hd in range(N_KV_B):
        kb_ref[0, hd] = kb[:, hd * HEAD_DIM:(hd + 1) * HEAD_DIM].astype(BF16)
        v_first = vb if hd == 0 else pltpu.roll(vb, KV_B - hd * HEAD_DIM, axis=1)
        vb_ref[0, hd] = jnp.where(lane < HEAD_DIM, v_first, ones_col).astype(BF16)


def _proj_call(x, mod, w_in, gq, gk, cos, slo, shi, bd, cq, ck, tm):
    b, s, d = x.shape
    nt = s // tm
    wide_a = N_HEADS_A * (2 * HEAD_DIM + LANES)
    bias_tab = pl.BlockSpec((N_HEADS_A, LANES), lambda bi, ti: (0, 0))
    const = lambda shape: pl.BlockSpec(shape, lambda bi, ti: (0,) * len(shape))
    tok = lambda width: pl.BlockSpec((1, tm, width), lambda bi, ti: (bi, ti, 0))
    heads = lambda n, width=HEAD_DIM: pl.BlockSpec((1, n, tm, width), lambda bi, ti: (bi, 0, ti, 0))
    tab = pl.BlockSpec((tm, LANES), lambda bi, ti: (ti, 0))
    return pl.pallas_call(
        _proj_kernel,
        grid=(b, nt),
        in_specs=[
            tok(d),
            pl.BlockSpec((1, 6, d), lambda bi, ti: (bi, 0, 0)),
            const(w_in.shape),
            const(gq.shape), const(gk.shape),
            tab, tab, tab,
            const(bd.shape),
            bias_tab, bias_tab,
        ],
        out_specs=[tok(wide_a), tok(wide_a), tok(WIDTH_A),
                   heads(N_HEADS_B), heads(N_KV_B), heads(N_KV_B, LANES),
                   pl.BlockSpec((1, 1, 2 * N_HEADS_A, LANES), lambda bi, ti: (bi, ti, 0, 0))],
        out_shape=[
            jax.ShapeDtypeStruct((b, s, wide_a), BF16),
            jax.ShapeDtypeStruct((b, s, wide_a), BF16),
            jax.ShapeDtypeStruct((b, s, WIDTH_A), BF16),
            jax.ShapeDtypeStruct((b, N_HEADS_B, s, HEAD_DIM), BF16),
            jax.ShapeDtypeStruct((b, N_KV_B, s, HEAD_DIM), BF16),
            jax.ShapeDtypeStruct((b, N_KV_B, s, LANES), BF16),
            jax.ShapeDtypeStruct((b, nt, 2 * N_HEADS_A, LANES), F32),
        ],
        compiler_params=pltpu.CompilerParams(
            dimension_semantics=("parallel", "parallel"), vmem_limit_bytes=VMEM_LIMIT),
        name="proj",
    )(x, mod, w_in, gq, gk, cos, slo, shi, bd, cq, ck)


def _lane_blocks(x, op):
    out = x[:, :LANES]
    for j in range(1, x.shape[1] // LANES):
        out = op(out, x[:, j * LANES:(j + 1) * LANES])
    return out


def _fold_row_max(s_ref, mt_ref, first):
    rows = s_ref.shape[0]
    for r0 in range(0, rows, SOFTMAX_CHUNK):
        sl = slice(r0, r0 + SOFTMAX_CHUNK)
        part = _lane_blocks(s_ref[sl, :], jnp.maximum)
        mt_ref[sl, :] = part if first else jnp.maximum(mt_ref[sl, :], part)


def _raise_max(mt_ref, m_ref, l_ref, acc_ref):
    rows = m_ref.shape[0]
    width = acc_ref.shape[-1]
    for r0 in range(0, rows, 4 * SOFTMAX_CHUNK):
        sl = slice(r0, r0 + 4 * SOFTMAX_CHUNK)
        m_prev = m_ref[sl, :]
        m_new = jnp.maximum(m_prev, jnp.max(mt_ref[sl, :], axis=1, keepdims=True))
        alpha = jnp.exp2(m_prev - m_new)
        m_ref[sl, :] = m_new
        if l_ref is not None:
            l_ref[sl, :] = alpha * l_ref[sl, :]
        acc_ref[sl, :] = _lanes(alpha, width) * acc_ref[sl, :]


def _write_probs(s_ref, p_ref, col0, m_ref, l_ref):
    rows, tk = s_ref.shape
    for r0 in range(0, rows, SOFTMAX_CHUNK):
        sl = slice(r0, r0 + SOFTMAX_CHUNK)
        p = jnp.exp2(s_ref[sl, :] - _lanes(m_ref[sl, :], tk))
        if l_ref is not None:
            l_ref[sl, :] += _lane_blocks(p, jnp.add)
        p_ref[sl, col0:col0 + tk] = p.astype(BF16)


def _softmax_update(s_ref, p_ref, m_ref, l_ref, acc_ref, bias_ref=None):
    rows, tk = s_ref.shape
    width = acc_ref.shape[-1]
    for r0 in range(0, rows, SOFTMAX_CHUNK):
        sl = slice(r0, r0 + SOFTMAX_CHUNK)
        s = s_ref[sl, :]
        if bias_ref is not None:
            s = s + bias_ref[sl, :]
            s_ref[sl, :] = s
        m_prev = m_ref[sl, :]
        m_new = jnp.maximum(m_prev, jnp.max(s, axis=1, keepdims=True))
        alpha = jnp.exp2(m_prev - m_new)
        m_ref[sl, :] = m_new
        l_ref[sl, :] = alpha * l_ref[sl, :]
        acc_ref[sl, :] = _lanes(alpha, width) * acc_ref[sl, :]
    _write_probs(s_ref, p_ref, 0, m_ref, l_ref)


def _diff_attn_kernel(order_ref, skip_ref, lam_ref, g_ref, dbias_ref, q_ref, *refs, n_sub):
    k_refs, v_refs = refs[:n_sub], refs[n_sub:2 * n_sub]
    o_ref, qv_sc, s_sc, p_sc, m_sc, l_sc, acc_sc = refs[2 * n_sub:]
    bi = pl.program_id(0)
    h = pl.program_id(1)
    qi = pl.program_id(2)
    ki = pl.program_id(3)
    n_q = pl.num_programs(2)
    n_k = pl.num_programs(3)
    hw = 2 * HEAD_DIM

    @pl.when(ki == 0)
    def _():
        m_sc[...] = jnp.full(m_sc.shape, -jnp.inf, F32)
        l_sc[...] = jnp.zeros(l_sc.shape, F32)
        acc_sc[...] = jnp.zeros(acc_sc.shape, F32)
        q = q_ref[0]
        zero = jnp.zeros_like(q)
        lane = lax.broadcasted_iota(I32, q.shape, 1)
        is_bias = lane >= hw
        for c in range(2):
            qc = jnp.where(is_bias | ((lane >= c * HEAD_DIM) & (lane < (c + 1) * HEAD_DIM)), q, zero)
            qv_sc[0, c] = qc
            qv_sc[1, c] = jnp.where(is_bias, -qc, qc)
            qv_sc[2, c] = jnp.where(is_bias, zero, qc)

    def body(starts_on_diagonal):
        for j in range(n_sub):
            diagonal = starts_on_diagonal and j == 0
            kt = order_ref[qi, ki * n_sub + j]
            version = 2 if diagonal else jnp.where(kt > qi, 0, 1)
            for c in range(2):
                s_sc[j, c] = lax.dot_general(qv_sc[version, c], k_refs[j][0], _NT,
                                             preferred_element_type=F32)
        for j in range(n_sub):
            diagonal = starts_on_diagonal and j == 0
            for c in range(2):
                _softmax_update(s_sc.at[j, c], p_sc.at[j, c], m_sc.at[c], l_sc.at[c], acc_sc.at[c],
                                dbias_ref.at[0] if diagonal else None)
                acc_sc[c] += jnp.dot(p_sc[j, c], v_refs[j][0], preferred_element_type=F32)

    @pl.when(ki == 0)
    def _():
        body(True)

    @pl.when((ki != 0) & (skip_ref[((bi * N_HEADS_A + h) * n_q + qi) * n_k + ki] == 0))
    def _():
        body(False)

    @pl.when(ki == n_k - 1)
    def _():
        l0 = jnp.sum(l_sc[0], axis=1, keepdims=True)
        l1 = jnp.sum(l_sc[1], axis=1, keepdims=True)
        o = acc_sc[0] / l0 - lam_ref[...] * (acc_sc[1] / l1)
        ms = jnp.mean(o * o, axis=-1, keepdims=True)
        o = o * lax.rsqrt(ms + RMS_EPS) * g_ref[...] * (1.0 - LAM_INIT)
        o_ref[0] = o.astype(o_ref.dtype)


def _kv_visit_order(n_kt):
    order = np.zeros((n_kt, n_kt), np.int32)
    for qi in range(n_kt):
        order[qi] = sorted(range(n_kt), key=lambda kt: (abs(kt - qi), qi - kt))
    return order


def _skip_flags(norms, order, slopes_c, t, n_sub):
    n_kt = order.shape[0]
    qn = jnp.sqrt(norms[:, :, :N_HEADS_A, 0])
    kn = jnp.sqrt(norms[:, :, N_HEADS_A:, 0])
    gap = (np.maximum(np.abs(order - np.arange(n_kt)[:, None]) - 1, 0) * t + 1).astype(np.float32)
    kn_seen = kn[:, order, :]
    bound = (qn[:, :, None, :] * (kn_seen + kn[:, :, None, :]) * 1.001
             - jnp.asarray(slopes_c, F32)[None, None, None, :] * gap[None, :, :, None])
    dead = (bound < -SKIP_EXPONENT).reshape(norms.shape[0], n_kt, n_kt // n_sub, n_sub, N_HEADS_A)
    flag = jnp.all(dead, axis=3).at[:, :, 0, :].set(False)
    return flag.transpose(0, 3, 1, 2).astype(I32).reshape(-1)


def _diff_attn_call(qa, ka, va, dbias, lam, g, order, skip, t, n_sub):
    b, s, _ = va.shape
    hw = 2 * HEAD_DIM
    qw = hw + LANES
    n_kt = s // t
    n_k = n_kt // n_sub

    def kv_spec(width, j):
        def index(bi, h, qi, ki, order_ref, skip_ref):
            dead = skip_ref[((bi * N_HEADS_A + h) * n_kt + qi) * n_k + ki] != 0
            return (bi, order_ref[qi, jnp.where(dead, 0, ki) * n_sub + j], h)
        return pl.BlockSpec((1, t, width), index)

    grid_spec = pltpu.PrefetchScalarGridSpec(
        num_scalar_prefetch=2,
        grid=(b, N_HEADS_A, n_kt, n_k),
        in_specs=[
            pl.BlockSpec((1, LANES), lambda bi, h, qi, ki, *_: (0, 0)),
            pl.BlockSpec((1, hw), lambda bi, h, qi, ki, *_: (0, 0)),
            pl.BlockSpec((1, t, t), lambda bi, h, qi, ki, *_: (h, 0, 0)),
            pl.BlockSpec((1, t, qw), lambda bi, h, qi, ki, *_: (bi, qi, h)),
            *[kv_spec(qw, j) for j in range(n_sub)],
            *[kv_spec(hw, j) for j in range(n_sub)],
        ],
        out_specs=pl.BlockSpec((1, t, hw), lambda bi, h, qi, ki, *_: (bi, qi, h)),
        scratch_shapes=[
            pltpu.VMEM((3, 2, t, qw), BF16),
            pltpu.VMEM((n_sub, 2, t, t), F32),
            pltpu.VMEM((n_sub, 2, t, t), BF16),
            pltpu.VMEM((2, t, LANES), F32),
            pltpu.VMEM((2, t, LANES), F32),
            pltpu.VMEM((2, t, hw), F32),
        ],
    )
    return pl.pallas_call(
        functools.partial(_diff_attn_kernel, n_sub=n_sub),
        grid_spec=grid_spec,
        out_shape=jax.ShapeDtypeStruct((b, s, WIDTH_A), BF16),
        compiler_params=pltpu.CompilerParams(
            dimension_semantics=("parallel", "parallel", "parallel", "arbitrary"),
            vmem_limit_bytes=VMEM_LIMIT),
        name="diff_attn",
    )(order, skip, lam, g, dbias, qa, *([ka] * n_sub), *([va] * n_sub))


def _gqa_kernel(q_ref, k_ref, v_ref, o_ref, s_sc, p_sc, mt_sc, m_sc, acc_sc, *, tq, tk):
    ki = pl.program_id(3)
    rows = GROUP_B * tq

    @pl.when(ki == 0)
    def _():
        m_sc[...] = jnp.full(m_sc.shape, -jnp.inf, F32)
        acc_sc[...] = jnp.zeros(acc_sc.shape, F32)

    n_sub = k_ref.shape[2] // tk
    q = q_ref[0].reshape(rows, HEAD_DIM)
    for j in range(n_sub):
        s_sc[j] = lax.dot_general(q, k_ref[0, 0, j * tk:(j + 1) * tk, :], _NT,
                                  preferred_element_type=F32)
    for j in range(n_sub):
        _fold_row_max(s_sc.at[j], mt_sc, j == 0)
    _raise_max(mt_sc, m_sc, None, acc_sc)
    for j in range(n_sub):
        _write_probs(s_sc.at[j], p_sc, j * tk, m_sc, None)
    acc_sc[...] += jnp.dot(p_sc[...], v_ref[0, 0], preferred_element_type=F32)

    @pl.when(ki == pl.num_programs(3) - 1)
    def _():
        acc = acc_sc[...]
        o = acc[:, :HEAD_DIM] / acc[:, HEAD_DIM:HEAD_DIM + 1]
        o_ref[0] = o.reshape(GROUP_B, tq, HEAD_DIM).astype(o_ref.dtype)


def _gqa_call(qb, kb, vb, tq, tk, n_sub):
    b, _, s, _ = qb.shape
    rows = GROUP_B * tq
    kv = n_sub * tk
    return pl.pallas_call(
        functools.partial(_gqa_kernel, tq=tq, tk=tk),
        grid=(b, N_KV_B, s // tq, s // kv),
        in_specs=[
            pl.BlockSpec((1, GROUP_B, tq, HEAD_DIM), lambda bi, n, qi, ki: (bi, n, qi, 0)),
            pl.BlockSpec((1, 1, kv, HEAD_DIM), lambda bi, n, qi, ki: (bi, n, ki, 0)),
            pl.BlockSpec((1, 1, kv, LANES), lambda bi, n, qi, ki: (bi, n, ki, 0)),
        ],
        out_specs=pl.BlockSpec((1, GROUP_B, tq, HEAD_DIM), lambda bi, n, qi, ki: (bi, n, qi, 0)),
        out_shape=jax.ShapeDtypeStruct((b, N_HEADS_B, s, HEAD_DIM), BF16),
        scratch_shapes=[
            pltpu.VMEM((n_sub, rows, tk), F32),
            pltpu.VMEM((rows, n_sub * tk), BF16),
            pltpu.VMEM((rows, LANES), F32),
            pltpu.VMEM((rows, LANES), F32),
            pltpu.VMEM((rows, LANES), F32),
        ],
        compiler_params=pltpu.CompilerParams(
            dimension_semantics=("parallel", "parallel", "parallel", "arbitrary"),
            vmem_limit_bytes=VMEM_LIMIT),
        name="gqa_attn",
    )(qb, kb, vb)


def _post_kernel(oa_ref, ob_ref, x_ref, mod_ref, w_ref, g_ref, b_ref, wrt_ref,
                 x1_ref, h2_ref, aff_ref):
    o = jnp.dot(oa_ref[0], w_ref[:WIDTH_A, :], preferred_element_type=F32)
    for hd in range(N_HEADS_B):
        lo = WIDTH_A + hd * HEAD_DIM
        o = o + jnp.dot(ob_ref[0, hd], w_ref[lo:lo + HEAD_DIM, :], preferred_element_type=F32)
    r = ALPHA * x_ref[0] + mod_ref[0, 2:3, :] * o
    x1 = _layer_norm(r) * g_ref[...] + b_ref[...]
    x1_ref[0] = x1
    h2 = _layer_norm(x1) * (1.0 + mod_ref[0, 4:5, :]) + mod_ref[0, 3:4, :]
    h2_ref[...] = h2.T.astype(BF16)
    logits = _dot3(wrt_ref[...], h2, _NT)
    z = jnp.exp(logits - jnp.max(logits, axis=0, keepdims=True))
    aff_ref[...] = z / jnp.sum(z, axis=0, keepdims=True)


def _post_call(oa, ob, x, mod, w_out, ln_g, ln_b, w_router_t, tm):
    b, s, d = x.shape
    nt = s // tm
    const = lambda shape: pl.BlockSpec(shape, lambda bi, ti: (0,) * len(shape))
    tok = lambda width: pl.BlockSpec((1, tm, width), lambda bi, ti: (bi, ti, 0))
    return pl.pallas_call(
        _post_kernel,
        grid=(b, nt),
        in_specs=[
            tok(WIDTH_A),
            pl.BlockSpec((1, N_HEADS_B, tm, HEAD_DIM), lambda bi, ti: (bi, 0, ti, 0)),
            tok(d),
            pl.BlockSpec((1, 6, d), lambda bi, ti: (bi, 0, 0)),
            const(w_out.shape), const(ln_g.shape), const(ln_b.shape), const(w_router_t.shape),
        ],
        out_specs=[tok(d),
                   pl.BlockSpec((d, tm), lambda bi, ti: (0, bi * nt + ti)),
                   pl.BlockSpec((N_EXPERTS, tm), lambda bi, ti: (0, bi * nt + ti))],
        out_shape=[
            jax.ShapeDtypeStruct((b, s, d), F32),
            jax.ShapeDtypeStruct((d, b * s), BF16),
            jax.ShapeDtypeStruct((N_EXPERTS, b * s), F32),
        ],
        compiler_params=pltpu.CompilerParams(
            dimension_semantics=("parallel", "parallel"), vmem_limit_bytes=VMEM_LIMIT),
        name="post",
    )(oa, ob, x, mod, w_out, ln_g, ln_b, w_router_t)


def _route_kernel(aff_ref, spos_ref, cnt_ref, *, cap, idx_bits):
    aff = aff_ref[...]
    n_e, n_r, _ = aff.shape
    bits = pltpu.bitcast(aff, I32)
    idx = (lax.broadcasted_iota(I32, aff.shape, 1) * LANES
           + lax.broadcasted_iota(I32, aff.shape, 2))

    def count(pred):
        part = jnp.sum(jnp.where(pred, 1.0, 0.0), axis=1, keepdims=True)
        return jnp.sum(part, axis=2, keepdims=True)

    def value_step(t, prefix):
        cand = prefix | (1 << (30 - t))
        return jnp.where(count(bits >= cand) >= cap, cand, prefix)

    thr = lax.fori_loop(0, 31, value_step, jnp.zeros((n_e, 1, 1), I32))
    above = bits > thr
    tie = bits == thr
    need = cap - count(above)

    def index_step(t, prefix):
        cand = prefix | (1 << (idx_bits - 1 - t))
        return jnp.where(count(tie & (idx < cand)) < need, cand, prefix)

    last = lax.fori_loop(0, idx_bits, index_step, jnp.zeros((n_e, 1, 1), I32))
    sel = above | (tie & (idx <= last))

    upper = (lax.broadcasted_iota(I32, (LANES, LANES), 0)
             <= lax.broadcasted_iota(I32, (LANES, LANES), 1)).astype(BF16)
    strict_lower = (lax.broadcasted_iota(I32, (n_r, n_r), 1)
                    < lax.broadcasted_iota(I32, (n_r, n_r), 0)).astype(BF16)
    for e in range(n_e):
        sel_e = sel[e]
        m = sel_e.astype(BF16)
        incl = jnp.dot(m, upper, preferred_element_type=F32)
        row_tot = jnp.broadcast_to(incl[:, LANES - 1:], incl.shape).astype(BF16)
        offs = jnp.dot(strict_lower, row_tot, preferred_element_type=F32)
        cnt = (incl + offs).astype(I32)
        cnt_ref[e] = cnt
        spos_ref[e] = jnp.where(sel_e, cnt - 1, -1)


def _route_call(aff3, cap):
    n_e, n_r, _ = aff3.shape
    idx_bits = max(1, (n_r * LANES - 1).bit_length())
    return pl.pallas_call(
        functools.partial(_route_kernel, cap=cap, idx_bits=idx_bits),
        out_shape=[jax.ShapeDtypeStruct(aff3.shape, I32), jax.ShapeDtypeStruct(aff3.shape, I32)],
        compiler_params=pltpu.CompilerParams(vmem_limit_bytes=VMEM_LIMIT),
        name="route",
    )(aff3)


def _ffn_kernel(tend_ref, spos_ref, g_ref, x_ref, wg_ref, wu_ref, wd_ref, ye_ref,
                stage_sc, gate_sc, *, blk):
    e = pl.program_id(0)
    i = pl.program_id(1)
    prev_end = jnp.where(i > 0, tend_ref[e, jnp.maximum(i - 1, 0)], 0)
    base = (prev_end // blk) * blk
    fill = tend_ref[e, i] - base

    @pl.when(i == 0)
    def _():
        stage_sc[...] = jnp.zeros(stage_sc.shape, F32)
        gate_sc[...] = jnp.zeros(gate_sc.shape, F32)

    rel = spos_ref[0, 0] - base
    slot = lax.broadcasted_iota(I32, (blk, blk), 0)

    def place(lo, rel_lo):
        hit = slot == rel_lo
        onehot = jnp.where(hit, 1.0, 0.0).astype(BF16)
        cols = slice(lo, lo + blk)
        stage_sc[:, cols] += lax.dot_general(x_ref[...], onehot, _NT, preferred_element_type=F32)
        gate_sc[cols, :] += jnp.sum(jnp.where(hit, g_ref[0, 0], 0.0), axis=1, keepdims=True)

    place(0, rel)

    @pl.when(fill > blk)
    def _():
        place(blk, rel - blk)

    @pl.when(fill >= blk)
    def _():
        xb = stage_sc[:, :blk].T.astype(BF16)
        gate = jnp.dot(xb, wg_ref[0], preferred_element_type=F32)
        up = jnp.dot(xb, wu_ref[0], preferred_element_type=F32)
        hid = (gate * jax.nn.sigmoid(gate) * up).astype(BF16)
        y = jnp.dot(hid, wd_ref[0], preferred_element_type=F32) * gate_sc[:blk, :1]
        for w in range(blk // SLOT_BLOCK):
            ye_ref[0, w] = y[w * SLOT_BLOCK:(w + 1) * SLOT_BLOCK, :].T.astype(ye_ref.dtype)
        stage_sc[:, :blk] = stage_sc[:, blk:]
        stage_sc[:, blk:] = jnp.zeros((stage_sc.shape[0], blk), F32)
        gate_sc[:blk, :] = gate_sc[blk:, :]
        gate_sc[blk:, :] = jnp.zeros((blk, LANES), F32)


def _ffn_call(tile_end, spos4, aff4, h2t, wg, wu, wd, cap, blk):
    d, n = h2t.shape
    n_t = n // blk
    ff = wg.shape[2]
    n_blk = cap // blk
    sub = blk // SLOT_BLOCK
    tt = blk

    def out_map(e, i, tend):
        prev_end = jnp.where(i > 0, tend[e, jnp.maximum(i - 1, 0)], 0)
        return (e, jnp.clip(prev_end // blk, 0, n_blk - 1), 0, 0)

    grid_spec = pltpu.PrefetchScalarGridSpec(
        num_scalar_prefetch=1,
        grid=(N_EXPERTS, n_t),
        in_specs=[
            pl.BlockSpec((1, 1, 1, tt), lambda e, i, tend: (e, i, 0, 0)),
            pl.BlockSpec((1, 1, 1, tt), lambda e, i, tend: (e, i, 0, 0)),
            pl.BlockSpec((d, tt), lambda e, i, tend: (0, i)),
            pl.BlockSpec((1, d, ff), lambda e, i, tend: (e, 0, 0)),
            pl.BlockSpec((1, d, ff), lambda e, i, tend: (e, 0, 0)),
            pl.BlockSpec((1, ff, d), lambda e, i, tend: (e, 0, 0)),
        ],
        out_specs=pl.BlockSpec((1, sub, d, SLOT_BLOCK), out_map),
        scratch_shapes=[
            pltpu.VMEM((d, 2 * blk), F32),
            pltpu.VMEM((2 * blk, LANES), F32),
        ],
    )
    return pl.pallas_call(
        functools.partial(_ffn_kernel, blk=blk),
        grid_spec=grid_spec,
        out_shape=jax.ShapeDtypeStruct((N_EXPERTS, n_blk * sub, d, SLOT_BLOCK), BF16),
        compiler_params=pltpu.CompilerParams(
            dimension_semantics=("arbitrary", "arbitrary"), vmem_limit_bytes=VMEM_LIMIT),
        name="ffn",
    )(tile_end, spos4, aff4, h2t, wg, wu, wd)


def _slot_span(tend, e, i):
    lo = jnp.where(i > 0, tend[e, jnp.maximum(i - 1, 0)], 0)
    return lo, tend[e, i]


def _combine_kernel(tend_ref, spos_ref, x1_ref, mod_ref, g_ref, b_ref, *refs):
    first = refs[:N_EXPERTS]
    second = refs[N_EXPERTS:2 * N_EXPERTS]
    o_ref, acc_sc = refs[2 * N_EXPERTS:]
    i = pl.program_id(0)
    tt = x1_ref.shape[0]
    slot = lax.broadcasted_iota(I32, (SLOT_BLOCK, tt), 0)

    def onehot(e, block_offset):
        lo, _ = _slot_span(tend_ref, e, i)
        rel = spos_ref[e, 0] - (lo // SLOT_BLOCK + block_offset) * SLOT_BLOCK
        return jnp.where(slot == rel, 1.0, 0.0).astype(BF16)

    acc_sc[...] = jnp.dot(
        jnp.concatenate([first[e][0, 0] for e in range(N_EXPERTS)], axis=1),
        jnp.concatenate([onehot(e, 0) for e in range(N_EXPERTS)], axis=0),
        preferred_element_type=F32)

    for e in range(N_EXPERTS):
        lo, hi = _slot_span(tend_ref, e, i)

        @pl.when(hi > (lo // SLOT_BLOCK + 1) * SLOT_BLOCK)
        def _(e=e):
            acc_sc[...] += jnp.dot(second[e][0, 0], onehot(e, 1), preferred_element_type=F32)

    r = ALPHA * x1_ref[...] + mod_ref[0, 5:6, :] * acc_sc[...].T
    o_ref[...] = _layer_norm(r) * g_ref[...] + b_ref[...]


def _combine_call(tile_end, spos4, ye_t, x1, mod, ln_g, ln_b, tt, tokens_per_batch):
    n, d = x1.shape
    n_t = n // tt
    n_blk = ye_t.shape[1]
    tiles_per_batch = tokens_per_batch // tt

    def first_spec(e):
        def index(i, tend):
            lo, _ = _slot_span(tend, e, i)
            return (e, jnp.clip(lo // SLOT_BLOCK, 0, n_blk - 1), 0, 0)
        return pl.BlockSpec((1, 1, d, SLOT_BLOCK), index)

    def second_spec(e):
        def index(i, tend):
            lo, hi = _slot_span(tend, e, i)
            nxt = lo // SLOT_BLOCK + 1
            spill = (hi > nxt * SLOT_BLOCK) & (nxt < n_blk)
            return (jnp.where(spill, e, 0), jnp.where(spill, jnp.clip(nxt, 0, n_blk - 1), 0), 0, 0)
        return pl.BlockSpec((1, 1, d, SLOT_BLOCK), index)

    grid_spec = pltpu.PrefetchScalarGridSpec(
        num_scalar_prefetch=1,
        grid=(n_t,),
        in_specs=[
            pl.BlockSpec((N_EXPERTS, 1, 1, tt), lambda i, tend: (0, i, 0, 0)),
            pl.BlockSpec((tt, d), lambda i, tend: (i, 0)),
            pl.BlockSpec((1, 6, d), lambda i, tend: (i // tiles_per_batch, 0, 0)),
            pl.BlockSpec((1, d), lambda i, tend: (0, 0)),
            pl.BlockSpec((1, d), lambda i, tend: (0, 0)),
            *[first_spec(e) for e in range(N_EXPERTS)],
            *[second_spec(e) for e in range(N_EXPERTS)],
        ],
        out_specs=pl.BlockSpec((tt, d), lambda i, tend: (i, 0)),
        scratch_shapes=[pltpu.VMEM((d, tt), F32)],
    )
    return pl.pallas_call(
        _combine_kernel,
        grid_spec=grid_spec,
        out_shape=jax.ShapeDtypeStruct((n, d), F32),
        compiler_params=pltpu.CompilerParams(
            dimension_semantics=("parallel",), vmem_limit_bytes=VMEM_LIMIT),
        name="combine",
    )(tile_end, spos4, x1, mod, ln_g, ln_b, *([ye_t] * (2 * N_EXPERTS)))


def _rope_tables(s):
    pos = jnp.arange(s)
    half = HEAD_DIM // 2
    inv_freq = ROPE_THETA ** (-jnp.arange(0, half, 2, dtype=F32) / half)
    ang_r = (pos // GRID_W).astype(F32)[:, None] * inv_freq[None, :]
    ang_c = (pos % GRID_W).astype(F32)[:, None] * inv_freq[None, :]
    zero = jnp.zeros_like(ang_r)
    cos = jnp.concatenate([jnp.cos(ang_r)] * 2 + [jnp.cos(ang_c)] * 2, axis=1)
    slo = jnp.concatenate([-jnp.sin(ang_r), zero, -jnp.sin(ang_c), zero], axis=1)
    shi = jnp.concatenate([zero, jnp.sin(ang_r), zero, jnp.sin(ang_c)], axis=1)
    rep = LANES // HEAD_DIM
    return tuple(jnp.concatenate([t] * rep, axis=1) for t in (cos, slo, shi))


def _bf16_terms(value, n=3):
    terms, rem = [], np.float32(value)
    for _ in range(n):
        part = np.float32(np.asarray(rem, dtype=jnp.bfloat16))
        terms.append(float(part))
        rem = np.float32(rem - part)
    return terms


def _alibi_constants(t):
    cq = np.zeros((N_HEADS_A, LANES), np.float32)
    ck = np.zeros((N_HEADS_A, LANES), np.float32)
    d = jnp.arange(t)
    dist = jnp.abs(d[:, None] - d[None, :]).astype(F32)
    db, slopes_c = [], []
    for h in range(N_HEADS_A):
        terms = _bf16_terms(2.0 ** (-8.0 * (h + 1) / N_HEADS_A) * LOG2E)
        cq[h, 0:6] = terms + terms
        ck[h, 6:12] = terms + terms
        slopes_c.append(np.float32(sum(terms)))
        db.append(-slopes_c[-1] * dist)
    return jnp.asarray(cq), jnp.asarray(ck), jnp.stack(db), slopes_c


def _tile(dim, target):
    t = min(dim, target)
    assert dim % t == 0, (dim, t)
    return t


def _trunk(x, mod, lam, consts, w):
    b, s, d = x.shape
    n = b * s
    cap = CAPACITY_FACTOR * n // N_EXPERTS
    assert cap % SLOT_BLOCK == 0 and n % LANES == 0
    tm = _tile(s, 512)
    tt = _tile(n, SLOT_BLOCK)

    ta = _tile(s, 512)
    cq, ck, dbias, slopes_c = _alibi_constants(ta)
    assert tm == ta
    qa, ka, va, qb, kb, vb, norms = _proj_call(
        x, mod, w["w_in"], w["gq"], w["gk"], *consts["rope"], consts["bd"], cq, ck, tm)
    n_sub = min(ATTN_SUBTILES, s // ta)
    assert (s // ta) % n_sub == 0, (s, ta, n_sub)
    order = _kv_visit_order(s // ta)
    skip = _skip_flags(norms, order, slopes_c, ta, n_sub)
    oa = _diff_attn_call(qa, ka, va, dbias, lam, w["subln_g"], jnp.asarray(order), skip, ta, n_sub)
    ob = _gqa_call(qb, kb, vb, _tile(s, 256), ta, n_sub)
    x1, h2t, aff_t = _post_call(oa, ob, x, mod, w["w_out"], w["ln1_g"], w["ln1_b"],
                                w["w_router_t"], tm)

    spos, cnt = _route_call(aff_t.reshape(N_EXPERTS, n // LANES, LANES), cap)
    cnt = cnt.reshape(N_EXPERTS, n)

    def tiles(a, width):
        return a.reshape(N_EXPERTS, n // width, 1, width)

    def tile_ends(width):
        return cnt[:, width - 1::width]

    blk = math.gcd(FFN_BLOCK, cap)
    ye_t = _ffn_call(tile_ends(blk), tiles(spos, blk), tiles(aff_t, blk), h2t,
                     w["w_gate"], w["w_up"], w["w_down"], cap, blk)
    out = _combine_call(tile_ends(tt), tiles(spos, tt), ye_t, x1.reshape(n, d), mod,
                        w["ln2_g"], w["ln2_b"], tt, s)
    return out.reshape(b, s, d)


def kernel(x_prompt, x_sample, c_prompt, c_sample, w_ada, b_ada, w_in, lam_q1, lam_k1, lam_q2,
           lam_k2, subln_g, q_norm_g, k_norm_g, w_out, ln1_g, ln1_b, w_router, w_gate, w_up,
           w_down, ln2_g, ln2_b):
    l = 0
    d = x_prompt.shape[-1]
    c_all = jnp.concatenate([c_prompt, c_sample], axis=0)
    mod, lam = _mod_call(c_all, w_ada[l], b_ada[l][None, :], lam_q1[l][None, :], lam_k1[l][None, :],
                         lam_q2[l][None, :], lam_k2[l][None, :])
    mod = mod.reshape(c_all.shape[0], 6, d)

    w = {
        "w_in": w_in[l].astype(BF16),
        "gq": jnp.tile(q_norm_g[l], N_HEADS_B)[None, :],
        "gk": jnp.tile(k_norm_g[l], N_KV_B)[None, :],
        "subln_g": subln_g[l][None, :],
        "w_out": w_out[l].astype(BF16),
        "ln1_g": ln1_g[l][None, :], "ln1_b": ln1_b[l][None, :],
        "w_router_t": w_router[l].T,
        "w_gate": w_gate[l].astype(BF16), "w_up": w_up[l].astype(BF16),
        "w_down": w_down[l].astype(BF16),
        "ln2_g": ln2_g[l][None, :], "ln2_b": ln2_b[l][None, :],
    }
    head_of = jnp.arange(WIDTH_B) // HEAD_DIM
    shared = {"bd": (head_of[:, None] == head_of[None, :]).astype(BF16)}

    outs = []
    nb = x_prompt.shape[0]
    for x, m in ((x_prompt, mod[:nb]), (x_sample, mod[nb:])):
        consts = dict(shared, rope=_rope_tables(x.shape[1]))
        outs.append(_trunk(x, m, lam, consts, w))
    return tuple(outs)
```

```python
import functools
import math

import jax
import jax.numpy as jnp
import numpy as np
from jax import lax
from jax.experimental import pallas as pl
from jax.experimental.pallas import tpu as pltpu

F32 = jnp.float32
BF16 = jnp.bfloat16
I32 = jnp.int32

HEAD_DIM = 64
N_HEADS_A = 4
WIDTH_A = N_HEADS_A * 2 * HEAD_DIM
N_HEADS_B = 8
N_KV_B = 2
GROUP_B = N_HEADS_B // N_KV_B
WIDTH_B = N_HEADS_B * HEAD_DIM
KV_B = N_KV_B * HEAD_DIM
GRID_W = 64
ROPE_THETA = 10000.0
N_EXPERTS = 16
CAPACITY_FACTOR = 2
DEPTH = 1
ALPHA = (2.0 * DEPTH) ** 0.25
LN_EPS = 1e-5
RMS_EPS = 1e-6
LAM_INIT = 0.8 - 0.6 * math.exp(-0.3 * 0)

LOG2E = 1.4426950408889634
SOFTMAX_CHUNK = 32
SKIP_EXPONENT = 170.0
N_BIAS_COLS = 12
ATTN_SUBTILES = 8

LANES = 128
SLOT_BLOCK = 256
FFN_BLOCK = 512
VMEM_LIMIT = 56 * 1024 * 1024

_NT = (((1,), (1,)), ((), ()))


def _split_bf16(a):
    hi = a.astype(BF16)
    lo = (a - hi.astype(F32)).astype(BF16)
    return hi, lo


def _dot3(a, b, dims=(((1,), (0,)), ((), ()))):
    ah, al = _split_bf16(a)
    bh, bl = _split_bf16(b)
    d = functools.partial(lax.dot_general, dimension_numbers=dims, preferred_element_type=F32)
    return d(ah, bh) + (d(ah, bl) + d(al, bh))


def _layer_norm(x):
    mu = jnp.mean(x, axis=-1, keepdims=True)
    xc = x - mu
    var = jnp.mean(xc * xc, axis=-1, keepdims=True)
    return xc * lax.rsqrt(var + LN_EPS)


def _lanes(x, n):
    if n == LANES:
        return x
    if n < LANES:
        return x[:, :n]
    return jnp.concatenate([x] * (n // LANES), axis=1)


def _mod_kernel(c_ref, w_ref, b_ref, lq1_ref, lk1_ref, lq2_ref, lk2_ref, mod_ref, lam_ref):
    c = c_ref[...]
    a = c * jax.nn.sigmoid(c)
    mod_ref[...] = _dot3(a, w_ref[...]) + b_ref[...]
    s1 = jnp.sum(lq1_ref[...] * lk1_ref[...], axis=-1, keepdims=True)
    s2 = jnp.sum(lq2_ref[...] * lk2_ref[...], axis=-1, keepdims=True)
    lam = jnp.exp(s1) - jnp.exp(s2) + LAM_INIT
    lam_ref[...] = jnp.broadcast_to(lam, lam_ref.shape)


def _mod_call(c, w_ada, b_ada, lq1, lk1, lq2, lk2):
    bt, d = c.shape
    n_chunks = w_ada.shape[1] // d
    vec = pl.BlockSpec((1, HEAD_DIM), lambda j: (0, 0))
    return pl.pallas_call(
        _mod_kernel,
        grid=(n_chunks,),
        in_specs=[
            pl.BlockSpec((bt, d), lambda j: (0, 0)),
            pl.BlockSpec((d, d), lambda j: (0, j)),
            pl.BlockSpec((1, d), lambda j: (0, j)),
            vec, vec, vec, vec,
        ],
        out_specs=[
            pl.BlockSpec((bt, d), lambda j: (0, j)),
            pl.BlockSpec((1, LANES), lambda j: (0, 0)),
        ],
        out_shape=[
            jax.ShapeDtypeStruct((bt, n_chunks * d), F32),
            jax.ShapeDtypeStruct((1, LANES), F32),
        ],
        compiler_params=pltpu.CompilerParams(vmem_limit_bytes=VMEM_LIMIT),
        name="mod",
    )(c, w_ada, b_ada, lq1, lk1, lq2, lk2)


def _head_rms(x, bd, g):
    sq = x * x
    hi, lo = _split_bf16(sq)
    ss = jnp.dot(hi, bd, preferred_element_type=F32) + jnp.dot(lo, bd, preferred_element_type=F32)
    return x * lax.rsqrt(ss * (1.0 / HEAD_DIM) + RMS_EPS) * g


def _rope(x, cos, sin_lo, sin_hi):
    w = x.shape[1]
    nxt = pltpu.roll(x, w - 16, axis=1)
    prv = pltpu.roll(x, 16, axis=1)
    return x * cos + nxt * sin_lo + prv * sin_hi


def _proj_kernel(x_ref, mod_ref, w_ref, gq_ref, gk_ref, cos_ref, slo_ref, shi_ref, bd_ref, cq_ref, ck_ref,
                 qa_ref, ka_ref, va_ref, qb_ref, kb_ref, vb_ref, norm_ref):
    x = x_ref[0]
    h = _layer_norm(x) * (1.0 + mod_ref[0, 1:2, :]) + mod_ref[0, 0:1, :]
    hb = h.astype(BF16)

    def proj(lo, width):
        return jnp.dot(hb, w_ref[:, lo:lo + width], preferred_element_type=F32)

    scale = HEAD_DIM ** -0.5 * LOG2E
    qa = (proj(0, WIDTH_A) * scale).astype(BF16)
    ka = proj(WIDTH_A, WIDTH_A).astype(BF16)
    hw = 2 * HEAD_DIM
    tm = x.shape[0]
    pos = pl.program_id(1) * tm + lax.broadcasted_iota(I32, (tm, LANES), 0)
    pos_hi = ((pos // LANES) * LANES).astype(F32)
    pos_lo = (pos % LANES).astype(F32)
    lane = lax.broadcasted_iota(I32, (tm, LANES), 1)
    zero = jnp.zeros((tm, LANES), F32)
    for hd in range(N_HEADS_A):
        lo = hd * (hw + LANES)
        bias_q = jnp.where(lane < 6, cq_ref[hd:hd + 1, :],
                           jnp.where(lane < 9, pos_hi, jnp.where(lane < N_BIAS_COLS, pos_lo, zero)))
        bias_k = jnp.where(lane < 3, -pos_hi,
                           jnp.where(lane < 6, -pos_lo, jnp.where(lane < N_BIAS_COLS, ck_ref[hd:hd + 1, :], zero)))
        qa_ref[0, :, lo:lo + hw] = qa[:, hd * hw:(hd + 1) * hw]
        qa_ref[0, :, lo + hw:lo + hw + LANES] = bias_q.astype(BF16)
        ka_ref[0, :, lo:lo + hw] = ka[:, hd * hw:(hd + 1) * hw]
        ka_ref[0, :, lo + hw:lo + hw + LANES] = bias_k.astype(BF16)
        for row, val in ((hd, qa), (N_HEADS_A + hd, ka)):
            v32 = val[:, hd * hw:(hd + 1) * hw].astype(F32)
            n2 = jnp.max(jnp.sum(v32 * v32, axis=1, keepdims=True), axis=0, keepdims=True)
            norm_ref[0, 0, row:row + 1, :] = jnp.broadcast_to(n2, (1, LANES))
    va_ref[0] = proj(2 * WIDTH_A, WIDTH_A).astype(BF16)

    cos, slo, shi = cos_ref[...], slo_ref[...], shi_ref[...]
    bd = bd_ref[...]
    qb = _head_rms(proj(3 * WIDTH_A, WIDTH_B), bd, gq_ref[...])
    qb = _rope(qb, _lanes(cos, WIDTH_B), _lanes(slo, WIDTH_B), _lanes(shi, WIDTH_B)) * scale
    for hd in range(N_HEADS_B):
        qb_ref[0, hd] = qb[:, hd * HEAD_DIM:(hd + 1) * HEAD_DIM].astype(BF16)
    kb = _head_rms(proj(3 * WIDTH_A + WIDTH_B, KV_B), bd[:KV_B, :KV_B], gk_ref[...])
    kb = _rope(kb, cos, slo, shi)
    vb = proj(3 * WIDTH_A + WIDTH_B + KV_B, KV_B)
    ones_col = jnp.where(lane == HEAD_DIM, 1.0, 0.0)
    for hd in range(N_KV_B):
        kb_ref[0, hd] = kb[:, hd * HEAD_DIM:(hd + 1) * HEAD_DIM].astype(BF16)
        v_first = vb if hd == 0 else pltpu.roll(vb, KV_B - hd * HEAD_DIM, axis=1)
        vb_ref[0, hd] = jnp.where(lane < HEAD_DIM, v_first, ones_col).astype(BF16)


def _proj_call(x, mod, w_in, gq, gk, cos, slo, shi, bd, cq, ck, tm):
    b, s, d = x.shape
    nt = s // tm
    wide_a = N_HEADS_A * (2 * HEAD_DIM + LANES)
    bias_tab = pl.BlockSpec((N_HEADS_A, LANES), lambda bi, ti: (0, 0))
    const = lambda shape: pl.BlockSpec(shape, lambda bi, ti: (0,) * len(shape))
    tok = lambda width: pl.BlockSpec((1, tm, width), lambda bi, ti: (bi, ti, 0))
    heads = lambda n, width=HEAD_DIM: pl.BlockSpec((1, n, tm, width), lambda bi, ti: (bi, 0, ti, 0))
    tab = pl.BlockSpec((tm, LANES), lambda bi, ti: (ti, 0))
    return pl.pallas_call(
        _proj_kernel,
        grid=(b, nt),
        in_specs=[
            tok(d),
            pl.BlockSpec((1, 6, d), lambda bi, ti: (bi, 0, 0)),
            const(w_in.shape),
            const(gq.shape), const(gk.shape),
            tab, tab, tab,
            const(bd.shape),
            bias_tab, bias_tab,
        ],
        out_specs=[tok(wide_a), tok(wide_a), tok(WIDTH_A),
                   heads(N_HEADS_B), heads(N_KV_B), heads(N_KV_B, LANES),
                   pl.BlockSpec((1, 1, 2 * N_HEADS_A, LANES), lambda bi, ti: (bi, ti, 0, 0))],
        out_shape=[
            jax.ShapeDtypeStruct((b, s, wide_a), BF16),
            jax.ShapeDtypeStruct((b, s, wide_a), BF16),
            jax.ShapeDtypeStruct((b, s, WIDTH_A), BF16),
            jax.ShapeDtypeStruct((b, N_HEADS_B, s, HEAD_DIM), BF16),
            jax.ShapeDtypeStruct((b, N_KV_B, s, HEAD_DIM), BF16),
            jax.ShapeDtypeStruct((b, N_KV_B, s, LANES), BF16),
            jax.ShapeDtypeStruct((b, nt, 2 * N_HEADS_A, LANES), F32),
        ],
        compiler_params=pltpu.CompilerParams(
            dimension_semantics=("parallel", "parallel"), vmem_limit_bytes=VMEM_LIMIT),
        name="proj",
    )(x, mod, w_in, gq, gk, cos, slo, shi, bd, cq, ck)


def _lane_blocks(x, op):
    out = x[:, :LANES]
    for j in range(1, x.shape[1] // LANES):
        out = op(out, x[:, j * LANES:(j + 1) * LANES])
    return out


def _fold_row_max(s_ref, mt_ref, first):
    rows = s_ref.shape[0]
    for r0 in range(0, rows, SOFTMAX_CHUNK):
        sl = slice(r0, r0 + SOFTMAX_CHUNK)
        part = _lane_blocks(s_ref[sl, :], jnp.maximum)
        mt_ref[sl, :] = part if first else jnp.maximum(mt_ref[sl, :], part)


def _raise_max(mt_ref, m_ref, l_ref, acc_ref):
    rows = m_ref.shape[0]
    width = acc_ref.shape[-1]
    for r0 in range(0, rows, 4 * SOFTMAX_CHUNK):
        sl = slice(r0, r0 + 4 * SOFTMAX_CHUNK)
        m_prev = m_ref[sl, :]
        m_new = jnp.maximum(m_prev, jnp.max(mt_ref[sl, :], axis=1, keepdims=True))
        alpha = jnp.exp2(m_prev - m_new)
        m_ref[sl, :] = m_new
        if l_ref is not None:
            l_ref[sl, :] = alpha * l_ref[sl, :]
        acc_ref[sl, :] = _lanes(alpha, width) * acc_ref[sl, :]


def _write_probs(s_ref, p_ref, col0, m_ref, l_ref):
    rows, tk = s_ref.shape
    for r0 in range(0, rows, SOFTMAX_CHUNK):
        sl = slice(r0, r0 + SOFTMAX_CHUNK)
        p = jnp.exp2(s_ref[sl, :] - _lanes(m_ref[sl, :], tk))
        if l_ref is not None:
            l_ref[sl, :] += _lane_blocks(p, jnp.add)
        p_ref[sl, col0:col0 + tk] = p.astype(BF16)


def _softmax_update(s_ref, p_ref, m_ref, l_ref, acc_ref, bias_ref=None):
    rows, tk = s_ref.shape
    width = acc_ref.shape[-1]
    for r0 in range(0, rows, SOFTMAX_CHUNK):
        sl = slice(r0, r0 + SOFTMAX_CHUNK)
        s = s_ref[sl, :]
        if bias_ref is not None:
            s = s + bias_ref[sl, :]
            s_ref[sl, :] = s
        m_prev = m_ref[sl, :]
        m_new = jnp.maximum(m_prev, jnp.max(s, axis=1, keepdims=True))
        alpha = jnp.exp2(m_prev - m_new)
        m_ref[sl, :] = m_new
        l_ref[sl, :] = alpha * l_ref[sl, :]
        acc_ref[sl, :] = _lanes(alpha, width) * acc_ref[sl, :]
    _write_probs(s_ref, p_ref, 0, m_ref, l_ref)


def _diff_attn_kernel(order_ref, skip_ref, lam_ref, g_ref, dbias_ref, q_ref, *refs, n_sub):
    k_refs, v_refs = refs[:n_sub], refs[n_sub:2 * n_sub]
    o_ref, qv_sc, s_sc, p_sc, m_sc, l_sc, acc_sc = refs[2 * n_sub:]
    bi = pl.program_id(0)
    h = pl.program_id(1)
    qi = pl.program_id(2)
    ki = pl.program_id(3)
    n_q = pl.num_programs(2)
    n_k = pl.num_programs(3)
    hw = 2 * HEAD_DIM

    @pl.when(ki == 0)
    def _():
        m_sc[...] = jnp.full(m_sc.shape, -jnp.inf, F32)
        l_sc[...] = jnp.zeros(l_sc.shape, F32)
        acc_sc[...] = jnp.zeros(acc_sc.shape, F32)
        q = q_ref[0]
        zero = jnp.zeros_like(q)
        lane = lax.broadcasted_iota(I32, q.shape, 1)
        is_bias = lane >= hw
        for c in range(2):
            qc = jnp.where(is_bias | ((lane >= c * HEAD_DIM) & (lane < (c + 1) * HEAD_DIM)), q, zero)
            qv_sc[0, c] = qc
            qv_sc[1, c] = jnp.where(is_bias, -qc, qc)
            qv_sc[2, c] = jnp.where(is_bias, zero, qc)

    def body(starts_on_diagonal):
        for j in range(n_sub):
            diagonal = starts_on_diagonal and j == 0
            kt = order_ref[qi, ki * n_sub + j]
            version = 2 if diagonal else jnp.where(kt > qi, 0, 1)
            for c in range(2):
                s_sc[j, c] = lax.dot_general(qv_sc[version, c], k_refs[j][0], _NT,
                                             preferred_element_type=F32)
        for j in range(n_sub):
            diagonal = starts_on_diagonal and j == 0
            for c in range(2):
                _softmax_update(s_sc.at[j, c], p_sc.at[j, c], m_sc.at[c], l_sc.at[c], acc_sc.at[c],
                                dbias_ref.at[0] if diagonal else None)
                acc_sc[c] += jnp.dot(p_sc[j, c], v_refs[j][0], preferred_element_type=F32)

    @pl.when(ki == 0)
    def _():
        body(True)

    @pl.when((ki != 0) & (skip_ref[((bi * N_HEADS_A + h) * n_q + qi) * n_k + ki] == 0))
    def _():
        body(False)

    @pl.when(ki == n_k - 1)
    def _():
        l0 = jnp.sum(l_sc[0], axis=1, keepdims=True)
        l1 = jnp.sum(l_sc[1], axis=1, keepdims=True)
        o = acc_sc[0] / l0 - lam_ref[...] * (acc_sc[1] / l1)
        ms = jnp.mean(o * o, axis=-1, keepdims=True)
        o = o * lax.rsqrt(ms + RMS_EPS) * g_ref[...] * (1.0 - LAM_INIT)
        o_ref[0] = o.astype(o_ref.dtype)


def _kv_visit_order(n_kt):
    order = np.zeros((n_kt, n_kt), np.int32)
    for qi in range(n_kt):
        order[qi] = sorted(range(n_kt), key=lambda kt: (abs(kt - qi), qi - kt))
    return order


def _skip_flags(norms, order, slopes_c, t, n_sub):
    n_kt = order.shape[0]
    qn = jnp.sqrt(norms[:, :, :N_HEADS_A, 0])
    kn = jnp.sqrt(norms[:, :, N_HEADS_A:, 0])
    gap = (np.maximum(np.abs(order - np.arange(n_kt)[:, None]) - 1, 0) * t + 1).astype(np.float32)
    kn_seen = kn[:, order, :]
    bound = (qn[:, :, None, :] * (kn_seen + kn[:, :, None, :]) * 1.001
             - jnp.asarray(slopes_c, F32)[None, None, None, :] * gap[None, :, :, None])
    dead = (bound < -SKIP_EXPONENT).reshape(norms.shape[0], n_kt, n_kt // n_sub, n_sub, N_HEADS_A)
    flag = jnp.all(dead, axis=3).at[:, :, 0, :].set(False)
    return flag.transpose(0, 3, 1, 2).astype(I32).reshape(-1)


def _diff_attn_call(qa, ka, va, dbias, lam, g, order, skip, t, n_sub):
    b, s, _ = va.shape
    hw = 2 * HEAD_DIM
    qw = hw + LANES
    n_kt = s // t
    n_k = n_kt // n_sub

    def kv_spec(width, j):
        def index(bi, h, qi, ki, order_ref, skip_ref):
            dead = skip_ref[((bi * N_HEADS_A + h) * n_kt + qi) * n_k + ki] != 0
            return (bi, order_ref[qi, jnp.where(dead, 0, ki) * n_sub + j], h)
        return pl.BlockSpec((1, t, width), index)

    grid_spec = pltpu.PrefetchScalarGridSpec(
        num_scalar_prefetch=2,
        grid=(b, N_HEADS_A, n_kt, n_k),
        in_specs=[
            pl.BlockSpec((1, LANES), lambda bi, h, qi, ki, *_: (0, 0)),
            pl.BlockSpec((1, hw), lambda bi, h, qi, ki, *_: (0, 0)),
            pl.BlockSpec((1, t, t), lambda bi, h, qi, ki, *_: (h, 0, 0)),
            pl.BlockSpec((1, t, qw), lambda bi, h, qi, ki, *_: (bi, qi, h)),
            *[kv_spec(qw, j) for j in range(n_sub)],
            *[kv_spec(hw, j) for j in range(n_sub)],
        ],
        out_specs=pl.BlockSpec((1, t, hw), lambda bi, h, qi, ki, *_: (bi, qi, h)),
        scratch_shapes=[
            pltpu.VMEM((3, 2, t, qw), BF16),
            pltpu.VMEM((n_sub, 2, t, t), F32),
            pltpu.VMEM((n_sub, 2, t, t), BF16),
            pltpu.VMEM((2, t, LANES), F32),
            pltpu.VMEM((2, t, LANES), F32),
            pltpu.VMEM((2, t, hw), F32),
        ],
    )
    return pl.pallas_call(
        functools.partial(_diff_attn_kernel, n_sub=n_sub),
        grid_spec=grid_spec,
        out_shape=jax.ShapeDtypeStruct((b, s, WIDTH_A), BF16),
        compiler_params=pltpu.CompilerParams(
            dimension_semantics=("parallel", "parallel", "parallel", "arbitrary"),
            vmem_limit_bytes=VMEM_LIMIT),
        name="diff_attn",
    )(order, skip, lam, g, dbias, qa, *([ka] * n_sub), *([va] * n_sub))


def _gqa_kernel(q_ref, k_ref, v_ref, o_ref, s_sc, p_sc, mt_sc, m_sc, acc_sc, *, tq, tk):
    ki = pl.program_id(3)
    rows = GROUP_B * tq

    @pl.when(ki == 0)
    def _():
        m_sc[...] = jnp.full(m_sc.shape, -jnp.inf, F32)
        acc_sc[...] = jnp.zeros(acc_sc.shape, F32)

    n_sub = k_ref.shape[2] // tk
    q = q_ref[0].reshape(rows, HEAD_DIM)
    for j in range(n_sub):
        s_sc[j] = lax.dot_general(q, k_ref[0, 0, j * tk:(j + 1) * tk, :], _NT,
                                  preferred_element_type=F32)
    for j in range(n_sub):
        _fold_row_max(s_sc.at[j], mt_sc, j == 0)
    _raise_max(mt_sc, m_sc, None, acc_sc)
    for j in range(n_sub):
        _write_probs(s_sc.at[j], p_sc, j * tk, m_sc, None)
    acc_sc[...] += jnp.dot(p_sc[...], v_ref[0, 0], preferred_element_type=F32)

    @pl.when(ki == pl.num_programs(3) - 1)
    def _():
        acc = acc_sc[...]
        o = acc[:, :HEAD_DIM] / acc[:, HEAD_DIM:HEAD_DIM + 1]
        o_ref[0] = o.reshape(GROUP_B, tq, HEAD_DIM).astype(o_ref.dtype)


def _gqa_call(qb, kb, vb, tq, tk, n_sub):
    b, _, s, _ = qb.shape
    rows = GROUP_B * tq
    kv = n_sub * tk
    return pl.pallas_call(
        functools.partial(_gqa_kernel, tq=tq, tk=tk),
        grid=(b, N_KV_B, s // tq, s // kv),
        in_specs=[
            pl.BlockSpec((1, GROUP_B, tq, HEAD_DIM), lambda bi, n, qi, ki: (bi, n, qi, 0)),
            pl.BlockSpec((1, 1, kv, HEAD_DIM), lambda bi, n, qi, ki: (bi, n, ki, 0)),
            pl.BlockSpec((1, 1, kv, LANES), lambda bi, n, qi, ki: (bi, n, ki, 0)),
        ],
        out_specs=pl.BlockSpec((1, GROUP_B, tq, HEAD_DIM), lambda bi, n, qi, ki: (bi, n, qi, 0)),
        out_shape=jax.ShapeDtypeStruct((b, N_HEADS_B, s, HEAD_DIM), BF16),
        scratch_shapes=[
            pltpu.VMEM((n_sub, rows, tk), F32),
            pltpu.VMEM((rows, n_sub * tk), BF16),
            pltpu.VMEM((rows, LANES), F32),
            pltpu.VMEM((rows, LANES), F32),
            pltpu.VMEM((rows, LANES), F32),
        ],
        compiler_params=pltpu.CompilerParams(
            dimension_semantics=("parallel", "parallel", "parallel", "arbitrary"),
            vmem_limit_bytes=VMEM_LIMIT),
        name="gqa_attn",
    )(qb, kb, vb)


def _post_kernel(oa_ref, ob_ref, x_ref, mod_ref, w_ref, g_ref, b_ref, wrt_ref,
                 x1_ref, h2_ref, aff_ref):
    o = jnp.dot(oa_ref[0], w_ref[:WIDTH_A, :], preferred_element_type=F32)
    for hd in range(N_HEADS_B):
        lo = WIDTH_A + hd * HEAD_DIM
        o = o + jnp.dot(ob_ref[0, hd], w_ref[lo:lo + HEAD_DIM, :], preferred_element_type=F32)
    r = ALPHA * x_ref[0] + mod_ref[0, 2:3, :] * o
    x1 = _layer_norm(r) * g_ref[...] + b_ref[...]
    x1_ref[0] = x1
    h2 = _layer_norm(x1) * (1.0 + mod_ref[0, 4:5, :]) + mod_ref[0, 3:4, :]
    h2_ref[...] = h2.T.astype(BF16)
    logits = _dot3(wrt_ref[...], h2, _NT)
    z = jnp.exp(logits - jnp.max(logits, axis=0, keepdims=True))
    aff_ref[...] = z / jnp.sum(z, axis=0, keepdims=True)


def _post_call(oa, ob, x, mod, w_out, ln_g, ln_b, w_router_t, tm):
    b, s, d = x.shape
    nt = s // tm
    const = lambda shape: pl.BlockSpec(shape, lambda bi, ti: (0,) * len(shape))
    tok = lambda width: pl.BlockSpec((1, tm, width), lambda bi, ti: (bi, ti, 0))
    return pl.pallas_call(
        _post_kernel,
        grid=(b, nt),
        in_specs=[
            tok(WIDTH_A),
            pl.BlockSpec((1, N_HEADS_B, tm, HEAD_DIM), lambda bi, ti: (bi, 0, ti, 0)),
            tok(d),
            pl.BlockSpec((1, 6, d), lambda bi, ti: (bi, 0, 0)),
            const(w_out.shape), const(ln_g.shape), const(ln_b.shape), const(w_router_t.shape),
        ],
        out_specs=[tok(d),
                   pl.BlockSpec((d, tm), lambda bi, ti: (0, bi * nt + ti)),
                   pl.BlockSpec((N_EXPERTS, tm), lambda bi, ti: (0, bi * nt + ti))],
        out_shape=[
            jax.ShapeDtypeStruct((b, s, d), F32),
            jax.ShapeDtypeStruct((d, b * s), BF16),
            jax.ShapeDtypeStruct((N_EXPERTS, b * s), F32),
        ],
        compiler_params=pltpu.CompilerParams(
            dimension_semantics=("parallel", "parallel"), vmem_limit_bytes=VMEM_LIMIT),
        name="post",
    )(oa, ob, x, mod, w_out, ln_g, ln_b, w_router_t)


def _route_kernel(aff_ref, spos_ref, cnt_ref, *, cap, idx_bits):
    aff = aff_ref[...]
    n_e, n_r, _ = aff.shape
    bits = pltpu.bitcast(aff, I32)
    idx = (lax.broadcasted_iota(I32, aff.shape, 1) * LANES
           + lax.broadcasted_iota(I32, aff.shape, 2))

    def count(pred):
        part = jnp.sum(jnp.where(pred, 1.0, 0.0), axis=1, keepdims=True)
        return jnp.sum(part, axis=2, keepdims=True)

    def value_step(t, prefix):
        cand = prefix | (1 << (30 - t))
        return jnp.where(count(bits >= cand) >= cap, cand, prefix)

    thr = lax.fori_loop(0, 31, value_step, jnp.zeros((n_e, 1, 1), I32))
    above = bits > thr
    tie = bits == thr
    need = cap - count(above)

    def index_step(t, prefix):
        cand = prefix | (1 << (idx_bits - 1 - t))
        return jnp.where(count(tie & (idx < cand)) < need, cand, prefix)

    last = lax.fori_loop(0, idx_bits, index_step, jnp.zeros((n_e, 1, 1), I32))
    sel = above | (tie & (idx <= last))

    upper = (lax.broadcasted_iota(I32, (LANES, LANES), 0)
             <= lax.broadcasted_iota(I32, (LANES, LANES), 1)).astype(BF16)
    strict_lower = (lax.broadcasted_iota(I32, (n_r, n_r), 1)
                    < lax.broadcasted_iota(I32, (n_r, n_r), 0)).astype(BF16)
    for e in range(n_e):
        sel_e = sel[e]
        m = sel_e.astype(BF16)
        incl = jnp.dot(m, upper, preferred_element_type=F32)
        row_tot = jnp.broadcast_to(incl[:, LANES - 1:], incl.shape).astype(BF16)
        offs = jnp.dot(strict_lower, row_tot, preferred_element_type=F32)
        cnt = (incl + offs).astype(I32)
        cnt_ref[e] = cnt
        spos_ref[e] = jnp.where(sel_e, cnt - 1, -1)


def _route_call(aff3, cap):
    n_e, n_r, _ = aff3.shape
    idx_bits = max(1, (n_r * LANES - 1).bit_length())
    return pl.pallas_call(
        functools.partial(_route_kernel, cap=cap, idx_bits=idx_bits),
        out_shape=[jax.ShapeDtypeStruct(aff3.shape, I32), jax.ShapeDtypeStruct(aff3.shape, I32)],
        compiler_params=pltpu.CompilerParams(vmem_limit_bytes=VMEM_LIMIT),
        name="route",
    )(aff3)


def _ffn_kernel(tend_ref, spos_ref, g_ref, x_ref, wg_ref, wu_ref, wd_ref, ye_ref,
                stage_sc, gate_sc, *, blk):
    e = pl.program_id(0)
    i = pl.program_id(1)
    prev_end = jnp.where(i > 0, tend_ref[e, jnp.maximum(i - 1, 0)], 0)
    base = (prev_end // blk) * blk
    fill = tend_ref[e, i] - base

    @pl.when(i == 0)
    def _():
        stage_sc[...] = jnp.zeros(stage_sc.shape, F32)
        gate_sc[...] = jnp.zeros(gate_sc.shape, F32)

    rel = spos_ref[0, 0] - base
    slot = lax.broadcasted_iota(I32, (blk, blk), 0)

    def place(lo, rel_lo):
        hit = slot == rel_lo
        onehot = jnp.where(hit, 1.0, 0.0).astype(BF16)
        cols = slice(lo, lo + blk)
        stage_sc[:, cols] += lax.dot_general(x_ref[...], onehot, _NT, preferred_element_type=F32)
        gate_sc[cols, :] += jnp.sum(jnp.where(hit, g_ref[0, 0], 0.0), axis=1, keepdims=True)

    place(0, rel)

    @pl.when(fill > blk)
    def _():
        place(blk, rel - blk)

    @pl.when(fill >= blk)
    def _():
        xb = stage_sc[:, :blk].T.astype(BF16)
        gate = jnp.dot(xb, wg_ref[0], preferred_element_type=F32)
        up = jnp.dot(xb, wu_ref[0], preferred_element_type=F32)
        hid = (gate * jax.nn.sigmoid(gate) * up).astype(BF16)
        y = jnp.dot(hid, wd_ref[0], preferred_element_type=F32) * gate_sc[:blk, :1]
        for w in range(blk // SLOT_BLOCK):
            ye_ref[0, w] = y[w * SLOT_BLOCK:(w + 1) * SLOT_BLOCK, :].T.astype(ye_ref.dtype)
        stage_sc[:, :blk] = stage_sc[:, blk:]
        stage_sc[:, blk:] = jnp.zeros((stage_sc.shape[0], blk), F32)
        gate_sc[:blk, :] = gate_sc[blk:, :]
        gate_sc[blk:, :] = jnp.zeros((blk, LANES), F32)


def _ffn_call(tile_end, spos4, aff4, h2t, wg, wu, wd, cap, blk):
    d, n = h2t.shape
    n_t = n // blk
    ff = wg.shape[2]
    n_blk = cap // blk
    sub = blk // SLOT_BLOCK
    tt = blk

    def out_map(e, i, tend):
        prev_end = jnp.where(i > 0, tend[e, jnp.maximum(i - 1, 0)], 0)
        return (e, jnp.clip(prev_end // blk, 0, n_blk - 1), 0, 0)

    grid_spec = pltpu.PrefetchScalarGridSpec(
        num_scalar_prefetch=1,
        grid=(N_EXPERTS, n_t),
        in_specs=[
            pl.BlockSpec((1, 1, 1, tt), lambda e, i, tend: (e, i, 0, 0)),
            pl.BlockSpec((1, 1, 1, tt), lambda e, i, tend: (e, i, 0, 0)),
            pl.BlockSpec((d, tt), lambda e, i, tend: (0, i)),
            pl.BlockSpec((1, d, ff), lambda e, i, tend: (e, 0, 0)),
            pl.BlockSpec((1, d, ff), lambda e, i, tend: (e, 0, 0)),
            pl.BlockSpec((1, ff, d), lambda e, i, tend: (e, 0, 0)),
        ],
        out_specs=pl.BlockSpec((1, sub, d, SLOT_BLOCK), out_map),
        scratch_shapes=[
            pltpu.VMEM((d, 2 * blk), F32),
            pltpu.VMEM((2 * blk, LANES), F32),
        ],
    )
    return pl.pallas_call(
        functools.partial(_ffn_kernel, blk=blk),
        grid_spec=grid_spec,
        out_shape=jax.ShapeDtypeStruct((N_EXPERTS, n_blk * sub, d, SLOT_BLOCK), BF16),
        compiler_params=pltpu.CompilerParams(
            dimension_semantics=("arbitrary", "arbitrary"), vmem_limit_bytes=VMEM_LIMIT),
        name="ffn",
    )(tile_end, spos4, aff4, h2t, wg, wu, wd)


def _slot_span(tend, e, i):
    lo = jnp.where(i > 0, tend[e, jnp.maximum(i - 1, 0)], 0)
    return lo, tend[e, i]


def _combine_kernel(tend_ref, table_ref, spos_ref, x1_ref, mod_ref, g_ref, b_ref, *refs):
    first = refs[:N_EXPERTS]
    second = refs[N_EXPERTS:2 * N_EXPERTS]
    o_ref, acc_sc = refs[2 * N_EXPERTS:]
    i = pl.program_id(0)
    tt = x1_ref.shape[0]
    slot = lax.broadcasted_iota(I32, (SLOT_BLOCK, tt), 0)

    def onehot(e, block_offset):
        lo, _ = _slot_span(tend_ref, e, i)
        rel = spos_ref[e, 0] - (lo // SLOT_BLOCK + block_offset) * SLOT_BLOCK
        return jnp.where(slot == rel, 1.0, 0.0).astype(BF16)

    acc_sc[...] = jnp.dot(
        jnp.concatenate([first[e][0, 0] for e in range(N_EXPERTS)], axis=1),
        jnp.concatenate([onehot(e, 0) for e in range(N_EXPERTS)], axis=0),
        preferred_element_type=F32)

    for e in range(N_EXPERTS):
        lo, hi = _slot_span(tend_ref, e, i)

        @pl.when(hi > (lo // SLOT_BLOCK + 1) * SLOT_BLOCK)
        def _(e=e):
            acc_sc[...] += jnp.dot(second[e][0, 0], onehot(e, 1), preferred_element_type=F32)

    r = ALPHA * x1_ref[...] + mod_ref[0, 5:6, :] * acc_sc[...].T
    o_ref[...] = _layer_norm(r) * g_ref[...] + b_ref[...]


def _combine_call(tile_end, spos4, ye_t, x1, mod, ln_g, ln_b, tt, tokens_per_batch):
    n, d = x1.shape
    n_t = n // tt
    n_blk = ye_t.shape[1]
    tiles_per_batch = tokens_per_batch // tt

    lo = jnp.concatenate([jnp.zeros((N_EXPERTS, 1), I32), tile_end[:, :-1]], axis=1)
    nxt = lo // SLOT_BLOCK + 1
    spill = (tile_end > nxt * SLOT_BLOCK) & (nxt < n_blk)
    expert = jnp.arange(N_EXPERTS, dtype=I32)[:, None]
    table = jnp.concatenate([jnp.clip(lo // SLOT_BLOCK, 0, n_blk - 1),
                             jnp.where(spill, expert, 0),
                             jnp.where(spill, jnp.clip(nxt, 0, n_blk - 1), 0)], axis=0).astype(I32)

    def first_spec(e):
        return pl.BlockSpec((1, 1, d, SLOT_BLOCK), lambda i, tend, tab: (e, tab[e, i], 0, 0))

    def second_spec(e):
        return pl.BlockSpec((1, 1, d, SLOT_BLOCK),
                            lambda i, tend, tab: (tab[N_EXPERTS + e, i], tab[2 * N_EXPERTS + e, i], 0, 0))

    grid_spec = pltpu.PrefetchScalarGridSpec(
        num_scalar_prefetch=2,
        grid=(n_t,),
        in_specs=[
            pl.BlockSpec((N_EXPERTS, 1, 1, tt), lambda i, *_: (0, i, 0, 0)),
            pl.BlockSpec((tt, d), lambda i, *_: (i, 0)),
            pl.BlockSpec((1, 6, d), lambda i, *_: (i // tiles_per_batch, 0, 0)),
            pl.BlockSpec((1, d), lambda i, *_: (0, 0)),
            pl.BlockSpec((1, d), lambda i, *_: (0, 0)),
            *[first_spec(e) for e in range(N_EXPERTS)],
            *[second_spec(e) for e in range(N_EXPERTS)],
        ],
        out_specs=pl.BlockSpec((tt, d), lambda i, *_: (i, 0)),
        scratch_shapes=[pltpu.VMEM((d, tt), F32)],
    )
    return pl.pallas_call(
        _combine_kernel,
        grid_spec=grid_spec,
        out_shape=jax.ShapeDtypeStruct((n, d), F32),
        compiler_params=pltpu.CompilerParams(
            dimension_semantics=("parallel",), vmem_limit_bytes=VMEM_LIMIT),
        name="combine",
    )(tile_end, table, spos4, x1, mod, ln_g, ln_b, *([ye_t] * (2 * N_EXPERTS)))


def _rope_tables(s):
    pos = jnp.arange(s)
    half = HEAD_DIM // 2
    inv_freq = ROPE_THETA ** (-jnp.arange(0, half, 2, dtype=F32) / half)
    ang_r = (pos // GRID_W).astype(F32)[:, None] * inv_freq[None, :]
    ang_c = (pos % GRID_W).astype(F32)[:, None] * inv_freq[None, :]
    zero = jnp.zeros_like(ang_r)
    cos = jnp.concatenate([jnp.cos(ang_r)] * 2 + [jnp.cos(ang_c)] * 2, axis=1)
    slo = jnp.concatenate([-jnp.sin(ang_r), zero, -jnp.sin(ang_c), zero], axis=1)
    shi = jnp.concatenate([zero, jnp.sin(ang_r), zero, jnp.sin(ang_c)], axis=1)
    rep = LANES // HEAD_DIM
    return tuple(jnp.concatenate([t] * rep, axis=1) for t in (cos, slo, shi))


def _bf16_terms(value, n=3):
    terms, rem = [], np.float32(value)
    for _ in range(n):
        part = np.float32(np.asarray(rem, dtype=jnp.bfloat16))
        terms.append(float(part))
        rem = np.float32(rem - part)
    return terms


def _alibi_constants(t):
    cq = np.zeros((N_HEADS_A, LANES), np.float32)
    ck = np.zeros((N_HEADS_A, LANES), np.float32)
    d = jnp.arange(t)
    dist = jnp.abs(d[:, None] - d[None, :]).astype(F32)
    db, slopes_c = [], []
    for h in range(N_HEADS_A):
        terms = _bf16_terms(2.0 ** (-8.0 * (h + 1) / N_HEADS_A) * LOG2E)
        cq[h, 0:6] = terms + terms
        ck[h, 6:12] = terms + terms
        slopes_c.append(np.float32(sum(terms)))
        db.append(-slopes_c[-1] * dist)
    return jnp.asarray(cq), jnp.asarray(ck), jnp.stack(db), slopes_c


def _tile(dim, target):
    t = min(dim, target)
    assert dim % t == 0, (dim, t)
    return t


def _trunk(x, mod, lam, consts, w):
    b, s, d = x.shape
    n = b * s
    cap = CAPACITY_FACTOR * n // N_EXPERTS
    assert cap % SLOT_BLOCK == 0 and n % LANES == 0
    tm = _tile(s, 512)
    tt = _tile(n, SLOT_BLOCK)

    ta = _tile(s, 512)
    cq, ck, dbias, slopes_c = _alibi_constants(ta)
    assert tm == ta
    qa, ka, va, qb, kb, vb, norms = _proj_call(
        x, mod, w["w_in"], w["gq"], w["gk"], *consts["rope"], consts["bd"], cq, ck, tm)
    n_sub = min(ATTN_SUBTILES, s // ta)
    assert (s // ta) % n_sub == 0, (s, ta, n_sub)
    order = _kv_visit_order(s // ta)
    skip = _skip_flags(norms, order, slopes_c, ta, n_sub)
    oa = _diff_attn_call(qa, ka, va, dbias, lam, w["subln_g"], jnp.asarray(order), skip, ta, n_sub)
    ob = _gqa_call(qb, kb, vb, _tile(s, 256), ta, n_sub)
    x1, h2t, aff_t = _post_call(oa, ob, x, mod, w["w_out"], w["ln1_g"], w["ln1_b"],
                                w["w_router_t"], tm)

    spos, cnt = _route_call(aff_t.reshape(N_EXPERTS, n // LANES, LANES), cap)
    cnt = cnt.reshape(N_EXPERTS, n)

    def tiles(a, width):
        return a.reshape(N_EXPERTS, n // width, 1, width)

    def tile_ends(width):
        return cnt[:, width - 1::width]

    blk = math.gcd(FFN_BLOCK, cap)
    ye_t = _ffn_call(tile_ends(blk), tiles(spos, blk), tiles(aff_t, blk), h2t,
                     w["w_gate"], w["w_up"], w["w_down"], cap, blk)
    out = _combine_call(tile_ends(tt), tiles(spos, tt), ye_t, x1.reshape(n, d), mod,
                        w["ln2_g"], w["ln2_b"], tt, s)
    return out.reshape(b, s, d)


def kernel(x_prompt, x_sample, c_prompt, c_sample, w_ada, b_ada, w_in, lam_q1, lam_k1, lam_q2,
           lam_k2, subln_g, q_norm_g, k_norm_g, w_out, ln1_g, ln1_b, w_router, w_gate, w_up,
           w_down, ln2_g, ln2_b):
    l = 0
    d = x_prompt.shape[-1]
    c_all = jnp.concatenate([c_prompt, c_sample], axis=0)
    mod, lam = _mod_call(c_all, w_ada[l], b_ada[l][None, :], lam_q1[l][None, :], lam_k1[l][None, :],
                         lam_q2[l][None, :], lam_k2[l][None, :])
    mod = mod.reshape(c_all.shape[0], 6, d)

    w = {
        "w_in": w_in[l].astype(BF16),
        "gq": jnp.tile(q_norm_g[l], N_HEADS_B)[None, :],
        "gk": jnp.tile(k_norm_g[l], N_KV_B)[None, :],
        "subln_g": subln_g[l][None, :],
        "w_out": w_out[l].astype(BF16),
        "ln1_g": ln1_g[l][None, :], "ln1_b": ln1_b[l][None, :],
        "w_router_t": w_router[l].T,
        "w_gate": w_gate[l].astype(BF16), "w_up": w_up[l].astype(BF16),
        "w_down": w_down[l].astype(BF16),
        "ln2_g": ln2_g[l][None, :], "ln2_b": ln2_b[l][None, :],
    }
    head_of = jnp.arange(WIDTH_B) // HEAD_DIM
    shared = {"bd": (head_of[:, None] == head_of[None, :]).astype(BF16)}

    outs = []
    nb = x_prompt.shape[0]
    for x, m in ((x_prompt, mod[:nb]), (x_sample, mod[nb:])):
        consts = dict(shared, rope=_rope_tables(x.shape[1]))
        outs.append(_trunk(x, m, lam, consts, w))
    return tuple(outs)
```

```python
import functools
import math

import jax
import jax.numpy as jnp
import numpy as np
from jax import lax
from jax.experimental import pallas as pl
from jax.experimental.pallas import tpu as pltpu

F32 = jnp.float32
BF16 = jnp.bfloat16
I32 = jnp.int32

HEAD_DIM = 64
N_HEADS_A = 4
WIDTH_A = N_HEADS_A * 2 * HEAD_DIM
N_HEADS_B = 8
N_KV_B = 2
GROUP_B = N_HEADS_B // N_KV_B
WIDTH_B = N_HEADS_B * HEAD_DIM
KV_B = N_KV_B * HEAD_DIM
GRID_W = 64
ROPE_THETA = 10000.0
N_EXPERTS = 16
CAPACITY_FACTOR = 2
DEPTH = 1
ALPHA = (2.0 * DEPTH) ** 0.25
LN_EPS = 1e-5
RMS_EPS = 1e-6
LAM_INIT = 0.8 - 0.6 * math.exp(-0.3 * 0)

LOG2E = 1.4426950408889634
SOFTMAX_CHUNK = 32
SKIP_EXPONENT = 170.0
N_BIAS_COLS = 12
ATTN_SUBTILES = 8

LANES = 128
SLOT_BLOCK = 256
FFN_BLOCK = 512
VMEM_LIMIT = 56 * 1024 * 1024

_NT = (((1,), (1,)), ((), ()))


def _split_bf16(a):
    hi = a.astype(BF16)
    lo = (a - hi.astype(F32)).astype(BF16)
    return hi, lo


def _dot3(a, b, dims=(((1,), (0,)), ((), ()))):
    ah, al = _split_bf16(a)
    bh, bl = _split_bf16(b)
    d = functools.partial(lax.dot_general, dimension_numbers=dims, preferred_element_type=F32)
    return d(ah, bh) + (d(ah, bl) + d(al, bh))


def _layer_norm(x):
    mu = jnp.mean(x, axis=-1, keepdims=True)
    xc = x - mu
    var = jnp.mean(xc * xc, axis=-1, keepdims=True)
    return xc * lax.rsqrt(var + LN_EPS)


def _lanes(x, n):
    if n == LANES:
        return x
    if n < LANES:
        return x[:, :n]
    return jnp.concatenate([x] * (n // LANES), axis=1)


def _mod_kernel(c_ref, w_ref, b_ref, lq1_ref, lk1_ref, lq2_ref, lk2_ref, mod_ref, lam_ref):
    c = c_ref[...]
    a = c * jax.nn.sigmoid(c)
    mod_ref[...] = _dot3(a, w_ref[...]) + b_ref[...]
    s1 = jnp.sum(lq1_ref[...] * lk1_ref[...], axis=-1, keepdims=True)
    s2 = jnp.sum(lq2_ref[...] * lk2_ref[...], axis=-1, keepdims=True)
    lam = jnp.exp(s1) - jnp.exp(s2) + LAM_INIT
    lam_ref[...] = jnp.broadcast_to(lam, lam_ref.shape)


def _mod_call(c, w_ada, b_ada, lq1, lk1, lq2, lk2):
    bt, d = c.shape
    n_chunks = w_ada.shape[1] // d
    vec = pl.BlockSpec((1, HEAD_DIM), lambda j: (0, 0))
    return pl.pallas_call(
        _mod_kernel,
        grid=(n_chunks,),
        in_specs=[
            pl.BlockSpec((bt, d), lambda j: (0, 0)),
            pl.BlockSpec((d, d), lambda j: (0, j)),
            pl.BlockSpec((1, d), lambda j: (0, j)),
            vec, vec, vec, vec,
        ],
        out_specs=[
            pl.BlockSpec((bt, d), lambda j: (0, j)),
            pl.BlockSpec((1, LANES), lambda j: (0, 0)),
        ],
        out_shape=[
            jax.ShapeDtypeStruct((bt, n_chunks * d), F32),
            jax.ShapeDtypeStruct((1, LANES), F32),
        ],
        compiler_params=pltpu.CompilerParams(vmem_limit_bytes=VMEM_LIMIT),
        name="mod",
    )(c, w_ada, b_ada, lq1, lk1, lq2, lk2)


def _head_rms(x, bd, g):
    sq = x * x
    hi, lo = _split_bf16(sq)
    ss = jnp.dot(hi, bd, preferred_element_type=F32) + jnp.dot(lo, bd, preferred_element_type=F32)
    return x * lax.rsqrt(ss * (1.0 / HEAD_DIM) + RMS_EPS) * g


def _rope(x, cos, sin_lo, sin_hi):
    w = x.shape[1]
    nxt = pltpu.roll(x, w - 16, axis=1)
    prv = pltpu.roll(x, 16, axis=1)
    return x * cos + nxt * sin_lo + prv * sin_hi


def _proj_kernel(x_ref, mod_ref, w_ref, gq_ref, gk_ref, cos_ref, slo_ref, shi_ref, bd_ref, cq_ref, ck_ref,
                 qa_ref, ka_ref, va_ref, qb_ref, kb_ref, vb_ref, norm_ref):
    x = x_ref[0]
    h = _layer_norm(x) * (1.0 + mod_ref[0, 1:2, :]) + mod_ref[0, 0:1, :]
    hb = h.astype(BF16)

    def proj(lo, width):
        return jnp.dot(hb, w_ref[:, lo:lo + width], preferred_element_type=F32)

    scale = HEAD_DIM ** -0.5 * LOG2E
    qa = (proj(0, WIDTH_A) * scale).astype(BF16)
    ka = proj(WIDTH_A, WIDTH_A).astype(BF16)
    hw = 2 * HEAD_DIM
    tm = x.shape[0]
    pos = pl.program_id(1) * tm + lax.broadcasted_iota(I32, (tm, LANES), 0)
    pos_hi = ((pos // LANES) * LANES).astype(F32)
    pos_lo = (pos % LANES).astype(F32)
    lane = lax.broadcasted_iota(I32, (tm, LANES), 1)
    zero = jnp.zeros((tm, LANES), F32)
    for hd in range(N_HEADS_A):
        lo = hd * (hw + LANES)
        bias_q = jnp.where(lane < 6, cq_ref[hd:hd + 1, :],
                           jnp.where(lane < 9, pos_hi, jnp.where(lane < N_BIAS_COLS, pos_lo, zero)))
        bias_k = jnp.where(lane < 3, -pos_hi,
                           jnp.where(lane < 6, -pos_lo, jnp.where(lane < N_BIAS_COLS, ck_ref[hd:hd + 1, :], zero)))
        qa_ref[0, :, lo:lo + hw] = qa[:, hd * hw:(hd + 1) * hw]
        qa_ref[0, :, lo + hw:lo + hw + LANES] = bias_q.astype(BF16)
        ka_ref[0, :, lo:lo + hw] = ka[:, hd * hw:(hd + 1) * hw]
        ka_ref[0, :, lo + hw:lo + hw + LANES] = bias_k.astype(BF16)
        for row, val in ((hd, qa), (N_HEADS_A + hd, ka)):
            v32 = val[:, hd * hw:(hd + 1) * hw].astype(F32)
            n2 = jnp.max(jnp.sum(v32 * v32, axis=1, keepdims=True), axis=0, keepdims=True)
            norm_ref[0, 0, row:row + 1, :] = jnp.broadcast_to(n2, (1, LANES))
    va_ref[0] = proj(2 * WIDTH_A, WIDTH_A).astype(BF16)

    cos, slo, shi = cos_ref[...], slo_ref[...], shi_ref[...]
    bd = bd_ref[...]
    qb = _head_rms(proj(3 * WIDTH_A, WIDTH_B), bd, gq_ref[...])
    qb = _rope(qb, _lanes(cos, WIDTH_B), _lanes(slo, WIDTH_B), _lanes(shi, WIDTH_B)) * scale
    for hd in range(N_HEADS_B):
        qb_ref[0, hd] = qb[:, hd * HEAD_DIM:(hd + 1) * HEAD_DIM].astype(BF16)
    kb = _head_rms(proj(3 * WIDTH_A + WIDTH_B, KV_B), bd[:KV_B, :KV_B], gk_ref[...])
    kb = _rope(kb, cos, slo, shi)
    vb = proj(3 * WIDTH_A + WIDTH_B + KV_B, KV_B)
    ones_col = jnp.where(lane == HEAD_DIM, 1.0, 0.0)
    for hd in range(N_KV_B):
        kb_ref[0, hd] = kb[:, hd * HEAD_DIM:(hd + 1) * HEAD_DIM].astype(BF16)
        v_first = vb if hd == 0 else pltpu.roll(vb, KV_B - hd * HEAD_DIM, axis=1)
        vb_ref[0, hd] = jnp.where(lane < HEAD_DIM, v_first, ones_col).astype(BF16)


def _proj_call(x, mod, w_in, gq, gk, cos, slo, shi, bd, cq, ck, tm):
    b, s, d = x.shape
    nt = s // tm
    wide_a = N_HEADS_A * (2 * HEAD_DIM + LANES)
    bias_tab = pl.BlockSpec((N_HEADS_A, LANES), lambda bi, ti: (0, 0))
    const = lambda shape: pl.BlockSpec(shape, lambda bi, ti: (0,) * len(shape))
    tok = lambda width: pl.BlockSpec((1, tm, width), lambda bi, ti: (bi, ti, 0))
    heads = lambda n, width=HEAD_DIM: pl.BlockSpec((1, n, tm, width), lambda bi, ti: (bi, 0, ti, 0))
    tab = pl.BlockSpec((tm, LANES), lambda bi, ti: (ti, 0))
    return pl.pallas_call(
        _proj_kernel,
        grid=(b, nt),
        in_specs=[
            tok(d),
            pl.BlockSpec((1, 6, d), lambda bi, ti: (bi, 0, 0)),
            const(w_in.shape),
            const(gq.shape), const(gk.shape),
            tab, tab, tab,
            const(bd.shape),
            bias_tab, bias_tab,
        ],
        out_specs=[tok(wide_a), tok(wide_a), tok(WIDTH_A),
                   heads(N_HEADS_B), heads(N_KV_B), heads(N_KV_B, LANES),
                   pl.BlockSpec((1, 1, 2 * N_HEADS_A, LANES), lambda bi, ti: (bi, ti, 0, 0))],
        out_shape=[
            jax.ShapeDtypeStruct((b, s, wide_a), BF16),
            jax.ShapeDtypeStruct((b, s, wide_a), BF16),
            jax.ShapeDtypeStruct((b, s, WIDTH_A), BF16),
            jax.ShapeDtypeStruct((b, N_HEADS_B, s, HEAD_DIM), BF16),
            jax.ShapeDtypeStruct((b, N_KV_B, s, HEAD_DIM), BF16),
            jax.ShapeDtypeStruct((b, N_KV_B, s, LANES), BF16),
            jax.ShapeDtypeStruct((b, nt, 2 * N_HEADS_A, LANES), F32),
        ],
        compiler_params=pltpu.CompilerParams(
            dimension_semantics=("parallel", "parallel"), vmem_limit_bytes=VMEM_LIMIT),
        name="proj",
    )(x, mod, w_in, gq, gk, cos, slo, shi, bd, cq, ck)


def _lane_blocks(x, op):
    out = x[:, :LANES]
    for j in range(1, x.shape[1] // LANES):
        out = op(out, x[:, j * LANES:(j + 1) * LANES])
    return out


def _fold_row_max(s_ref, mt_ref, first):
    rows = s_ref.shape[0]
    for r0 in range(0, rows, SOFTMAX_CHUNK):
        sl = slice(r0, r0 + SOFTMAX_CHUNK)
        part = _lane_blocks(s_ref[sl, :], jnp.maximum)
        mt_ref[sl, :] = part if first else jnp.maximum(mt_ref[sl, :], part)


def _raise_max(mt_ref, m_ref, l_ref, acc_ref):
    rows = m_ref.shape[0]
    width = acc_ref.shape[-1]
    for r0 in range(0, rows, 4 * SOFTMAX_CHUNK):
        sl = slice(r0, r0 + 4 * SOFTMAX_CHUNK)
        m_prev = m_ref[sl, :]
        m_new = jnp.maximum(m_prev, jnp.max(mt_ref[sl, :], axis=1, keepdims=True))
        alpha = jnp.exp2(m_prev - m_new)
        m_ref[sl, :] = m_new
        if l_ref is not None:
            l_ref[sl, :] = alpha * l_ref[sl, :]
        acc_ref[sl, :] = _lanes(alpha, width) * acc_ref[sl, :]


def _write_probs(s_ref, p_ref, col0, m_ref, l_ref):
    rows, tk = s_ref.shape
    for r0 in range(0, rows, SOFTMAX_CHUNK):
        sl = slice(r0, r0 + SOFTMAX_CHUNK)
        p = jnp.exp2(s_ref[sl, :] - _lanes(m_ref[sl, :], tk))
        if l_ref is not None:
            l_ref[sl, :] += _lane_blocks(p, jnp.add)
        p_ref[sl, col0:col0 + tk] = p.astype(BF16)


def _softmax_update(s_ref, p_ref, m_ref, l_ref, acc_ref, bias_ref=None):
    rows, tk = s_ref.shape
    width = acc_ref.shape[-1]
    for r0 in range(0, rows, SOFTMAX_CHUNK):
        sl = slice(r0, r0 + SOFTMAX_CHUNK)
        s = s_ref[sl, :]
        if bias_ref is not None:
            s = s + bias_ref[sl, :]
            s_ref[sl, :] = s
        m_prev = m_ref[sl, :]
        m_new = jnp.maximum(m_prev, jnp.max(s, axis=1, keepdims=True))
        alpha = jnp.exp2(m_prev - m_new)
        m_ref[sl, :] = m_new
        l_ref[sl, :] = alpha * l_ref[sl, :]
        acc_ref[sl, :] = _lanes(alpha, width) * acc_ref[sl, :]
    _write_probs(s_ref, p_ref, 0, m_ref, l_ref)


def _diff_attn_kernel(order_ref, skip_ref, lam_ref, g_ref, dbias_ref, q_ref, *refs, n_sub):
    k_refs, v_refs = refs[:n_sub], refs[n_sub:2 * n_sub]
    o_ref, qv_sc, s_sc, p_sc, m_sc, l_sc, acc_sc = refs[2 * n_sub:]
    bi = pl.program_id(0)
    h = pl.program_id(1)
    qi = pl.program_id(2)
    ki = pl.program_id(3)
    n_q = pl.num_programs(2)
    n_k = pl.num_programs(3)
    hw = 2 * HEAD_DIM

    @pl.when(ki == 0)
    def _():
        m_sc[...] = jnp.full(m_sc.shape, -jnp.inf, F32)
        l_sc[...] = jnp.zeros(l_sc.shape, F32)
        acc_sc[...] = jnp.zeros(acc_sc.shape, F32)
        q = q_ref[0]
        zero = jnp.zeros_like(q)
        lane = lax.broadcasted_iota(I32, q.shape, 1)
        is_bias = lane >= hw
        for c in range(2):
            qc = jnp.where(is_bias | ((lane >= c * HEAD_DIM) & (lane < (c + 1) * HEAD_DIM)), q, zero)
            qv_sc[0, c] = qc
            qv_sc[1, c] = jnp.where(is_bias, -qc, qc)
            qv_sc[2, c] = jnp.where(is_bias, zero, qc)

    def body(starts_on_diagonal):
        for j in range(n_sub):
            diagonal = starts_on_diagonal and j == 0
            kt = order_ref[qi, ki * n_sub + j]
            version = 2 if diagonal else jnp.where(kt > qi, 0, 1)
            for c in range(2):
                s_sc[j, c] = lax.dot_general(qv_sc[version, c], k_refs[j][0], _NT,
                                             preferred_element_type=F32)
        for j in range(n_sub):
            diagonal = starts_on_diagonal and j == 0
            for c in range(2):
                _softmax_update(s_sc.at[j, c], p_sc.at[j, c], m_sc.at[c], l_sc.at[c], acc_sc.at[c],
                                dbias_ref.at[0] if diagonal else None)
                acc_sc[c] += jnp.dot(p_sc[j, c], v_refs[j][0], preferred_element_type=F32)

    @pl.when(ki == 0)
    def _():
        body(True)

    @pl.when((ki != 0) & (skip_ref[((bi * N_HEADS_A + h) * n_q + qi) * n_k + ki] == 0))
    def _():
        body(False)

    @pl.when(ki == n_k - 1)
    def _():
        l0 = jnp.sum(l_sc[0], axis=1, keepdims=True)
        l1 = jnp.sum(l_sc[1], axis=1, keepdims=True)
        o = acc_sc[0] / l0 - lam_ref[...] * (acc_sc[1] / l1)
        ms = jnp.mean(o * o, axis=-1, keepdims=True)
        o = o * lax.rsqrt(ms + RMS_EPS) * g_ref[...] * (1.0 - LAM_INIT)
        o_ref[0] = o.astype(o_ref.dtype)


def _kv_visit_order(n_kt):
    order = np.zeros((n_kt, n_kt), np.int32)
    for qi in range(n_kt):
        order[qi] = sorted(range(n_kt), key=lambda kt: (abs(kt - qi), qi - kt))
    return order


def _skip_flags(norms, order, slopes_c, t, n_sub):
    n_kt = order.shape[0]
    qn = jnp.sqrt(norms[:, :, :N_HEADS_A, 0])
    kn = jnp.sqrt(norms[:, :, N_HEADS_A:, 0])
    gap = (np.maximum(np.abs(order - np.arange(n_kt)[:, None]) - 1, 0) * t + 1).astype(np.float32)
    kn_seen = kn[:, order, :]
    bound = (qn[:, :, None, :] * (kn_seen + kn[:, :, None, :]) * 1.001
             - jnp.asarray(slopes_c, F32)[None, None, None, :] * gap[None, :, :, None])
    dead = (bound < -SKIP_EXPONENT).reshape(norms.shape[0], n_kt, n_kt // n_sub, n_sub, N_HEADS_A)
    flag = jnp.all(dead, axis=3).at[:, :, 0, :].set(False)
    return flag.transpose(0, 3, 1, 2).astype(I32).reshape(-1)


def _diff_attn_call(qa, ka, va, dbias, lam, g, order, skip, t, n_sub):
    b, s, _ = va.shape
    hw = 2 * HEAD_DIM
    qw = hw + LANES
    n_kt = s // t
    n_k = n_kt // n_sub

    def kv_spec(width, j):
        def index(bi, h, qi, ki, order_ref, skip_ref):
            dead = skip_ref[((bi * N_HEADS_A + h) * n_kt + qi) * n_k + ki] != 0
            return (bi, order_ref[qi, jnp.where(dead, 0, ki) * n_sub + j], h)
        return pl.BlockSpec((1, t, width), index)

    grid_spec = pltpu.PrefetchScalarGridSpec(
        num_scalar_prefetch=2,
        grid=(b, N_HEADS_A, n_kt, n_k),
        in_specs=[
            pl.BlockSpec((1, LANES), lambda bi, h, qi, ki, *_: (0, 0)),
            pl.BlockSpec((1, hw), lambda bi, h, qi, ki, *_: (0, 0)),
            pl.BlockSpec((1, t, t), lambda bi, h, qi, ki, *_: (h, 0, 0)),
            pl.BlockSpec((1, t, qw), lambda bi, h, qi, ki, *_: (bi, qi, h)),
            *[kv_spec(qw, j) for j in range(n_sub)],
            *[kv_spec(hw, j) for j in range(n_sub)],
        ],
        out_specs=pl.BlockSpec((1, t, hw), lambda bi, h, qi, ki, *_: (bi, qi, h)),
        scratch_shapes=[
            pltpu.VMEM((3, 2, t, qw), BF16),
            pltpu.VMEM((n_sub, 2, t, t), F32),
            pltpu.VMEM((n_sub, 2, t, t), BF16),
            pltpu.VMEM((2, t, LANES), F32),
            pltpu.VMEM((2, t, LANES), F32),
            pltpu.VMEM((2, t, hw), F32),
        ],
    )
    return pl.pallas_call(
        functools.partial(_diff_attn_kernel, n_sub=n_sub),
        grid_spec=grid_spec,
        out_shape=jax.ShapeDtypeStruct((b, s, WIDTH_A), BF16),
        compiler_params=pltpu.CompilerParams(
            dimension_semantics=("parallel", "parallel", "parallel", "arbitrary"),
            vmem_limit_bytes=VMEM_LIMIT),
        name="diff_attn",
    )(order, skip, lam, g, dbias, qa, *([ka] * n_sub), *([va] * n_sub))


def _gqa_kernel(q_ref, k_ref, v_ref, o_ref, s_sc, p_sc, mt_sc, m_sc, acc_sc, *, tq, tk):
    ki = pl.program_id(3)
    rows = GROUP_B * tq

    @pl.when(ki == 0)
    def _():
        m_sc[...] = jnp.full(m_sc.shape, -jnp.inf, F32)
        acc_sc[...] = jnp.zeros(acc_sc.shape, F32)

    n_sub = k_ref.shape[2] // tk
    q = q_ref[0].reshape(rows, HEAD_DIM)
    for j in range(n_sub):
        s_sc[j] = lax.dot_general(q, k_ref[0, 0, j * tk:(j + 1) * tk, :], _NT,
                                  preferred_element_type=F32)
    for j in range(n_sub):
        _fold_row_max(s_sc.at[j], mt_sc, j == 0)
    _raise_max(mt_sc, m_sc, None, acc_sc)
    for j in range(n_sub):
        _write_probs(s_sc.at[j], p_sc, j * tk, m_sc, None)
    acc_sc[...] += jnp.dot(p_sc[...], v_ref[0, 0], preferred_element_type=F32)

    @pl.when(ki == pl.num_programs(3) - 1)
    def _():
        acc = acc_sc[...]
        o = acc[:, :HEAD_DIM] / acc[:, HEAD_DIM:HEAD_DIM + 1]
        o_ref[0] = o.reshape(GROUP_B, tq, HEAD_DIM).astype(o_ref.dtype)


def _gqa_call(qb, kb, vb, tq, tk, n_sub):
    b, _, s, _ = qb.shape
    rows = GROUP_B * tq
    kv = n_sub * tk
    return pl.pallas_call(
        functools.partial(_gqa_kernel, tq=tq, tk=tk),
        grid=(b, N_KV_B, s // tq, s // kv),
        in_specs=[
            pl.BlockSpec((1, GROUP_B, tq, HEAD_DIM), lambda bi, n, qi, ki: (bi, n, qi, 0)),
            pl.BlockSpec((1, 1, kv, HEAD_DIM), lambda bi, n, qi, ki: (bi, n, ki, 0)),
            pl.BlockSpec((1, 1, kv, LANES), lambda bi, n, qi, ki: (bi, n, ki, 0)),
        ],
        out_specs=pl.BlockSpec((1, GROUP_B, tq, HEAD_DIM), lambda bi, n, qi, ki: (bi, n, qi, 0)),
        out_shape=jax.ShapeDtypeStruct((b, N_HEADS_B, s, HEAD_DIM), BF16),
        scratch_shapes=[
            pltpu.VMEM((n_sub, rows, tk), F32),
            pltpu.VMEM((rows, n_sub * tk), BF16),
            pltpu.VMEM((rows, LANES), F32),
            pltpu.VMEM((rows, LANES), F32),
            pltpu.VMEM((rows, LANES), F32),
        ],
        compiler_params=pltpu.CompilerParams(
            dimension_semantics=("parallel", "parallel", "parallel", "arbitrary"),
            vmem_limit_bytes=VMEM_LIMIT),
        name="gqa_attn",
    )(qb, kb, vb)


def _post_kernel(oa_ref, ob_ref, x_ref, mod_ref, w_ref, g_ref, b_ref, wrt_ref,
                 x1_ref, h2_ref, aff_ref):
    o = jnp.dot(oa_ref[0], w_ref[:WIDTH_A, :], preferred_element_type=F32)
    for hd in range(N_HEADS_B):
        lo = WIDTH_A + hd * HEAD_DIM
        o = o + jnp.dot(ob_ref[0, hd], w_ref[lo:lo + HEAD_DIM, :], preferred_element_type=F32)
    r = ALPHA * x_ref[0] + mod_ref[0, 2:3, :] * o
    x1 = _layer_norm(r) * g_ref[...] + b_ref[...]
    x1_ref[0] = x1
    h2 = _layer_norm(x1) * (1.0 + mod_ref[0, 4:5, :]) + mod_ref[0, 3:4, :]
    h2_ref[...] = h2.T.astype(BF16)
    logits = _dot3(wrt_ref[...], h2, _NT)
    z = jnp.exp(logits - jnp.max(logits, axis=0, keepdims=True))
    aff_ref[...] = z / jnp.sum(z, axis=0, keepdims=True)


def _post_call(oa, ob, x, mod, w_out, ln_g, ln_b, w_router_t, tm):
    b, s, d = x.shape
    nt = s // tm
    const = lambda shape: pl.BlockSpec(shape, lambda bi, ti: (0,) * len(shape))
    tok = lambda width: pl.BlockSpec((1, tm, width), lambda bi, ti: (bi, ti, 0))
    return pl.pallas_call(
        _post_kernel,
        grid=(b, nt),
        in_specs=[
            tok(WIDTH_A),
            pl.BlockSpec((1, N_HEADS_B, tm, HEAD_DIM), lambda bi, ti: (bi, 0, ti, 0)),
            tok(d),
            pl.BlockSpec((1, 6, d), lambda bi, ti: (bi, 0, 0)),
            const(w_out.shape), const(ln_g.shape), const(ln_b.shape), const(w_router_t.shape),
        ],
        out_specs=[tok(d),
                   pl.BlockSpec((d, tm), lambda bi, ti: (0, bi * nt + ti)),
                   pl.BlockSpec((N_EXPERTS, tm), lambda bi, ti: (0, bi * nt + ti))],
        out_shape=[
            jax.ShapeDtypeStruct((b, s, d), F32),
            jax.ShapeDtypeStruct((d, b * s), BF16),
            jax.ShapeDtypeStruct((N_EXPERTS, b * s), F32),
        ],
        compiler_params=pltpu.CompilerParams(
            dimension_semantics=("parallel", "parallel"), vmem_limit_bytes=VMEM_LIMIT),
        name="post",
    )(oa, ob, x, mod, w_out, ln_g, ln_b, w_router_t)


def _route_kernel(aff_ref, spos_ref, cnt_ref, *, cap, idx_bits):
    aff = aff_ref[...]
    n_e, n_r, _ = aff.shape
    bits = pltpu.bitcast(aff, I32)
    idx = (lax.broadcasted_iota(I32, aff.shape, 1) * LANES
           + lax.broadcasted_iota(I32, aff.shape, 2))

    def count(pred):
        part = jnp.sum(jnp.where(pred, 1.0, 0.0), axis=1, keepdims=True)
        return jnp.sum(part, axis=2, keepdims=True)

    def value_step(t, prefix):
        cand = prefix | (1 << (30 - t))
        return jnp.where(count(bits >= cand) >= cap, cand, prefix)

    thr = lax.fori_loop(0, 31, value_step, jnp.zeros((n_e, 1, 1), I32))
    above = bits > thr
    tie = bits == thr
    need = cap - count(above)

    def index_step(t, prefix):
        cand = prefix | (1 << (idx_bits - 1 - t))
        return jnp.where(count(tie & (idx < cand)) < need, cand, prefix)

    last = lax.fori_loop(0, idx_bits, index_step, jnp.zeros((n_e, 1, 1), I32))
    sel = above | (tie & (idx <= last))

    upper = (lax.broadcasted_iota(I32, (LANES, LANES), 0)
             <= lax.broadcasted_iota(I32, (LANES, LANES), 1)).astype(BF16)
    strict_lower = (lax.broadcasted_iota(I32, (n_r, n_r), 1)
                    < lax.broadcasted_iota(I32, (n_r, n_r), 0)).astype(BF16)
    for e in range(n_e):
        sel_e = sel[e]
        m = sel_e.astype(BF16)
        incl = jnp.dot(m, upper, preferred_element_type=F32)
        row_tot = jnp.broadcast_to(incl[:, LANES - 1:], incl.shape).astype(BF16)
        offs = jnp.dot(strict_lower, row_tot, preferred_element_type=F32)
        cnt = (incl + offs).astype(I32)
        cnt_ref[e] = cnt
        spos_ref[e] = jnp.where(sel_e, cnt - 1, -1)


def _route_call(aff3, cap):
    n_e, n_r, _ = aff3.shape
    idx_bits = max(1, (n_r * LANES - 1).bit_length())
    return pl.pallas_call(
        functools.partial(_route_kernel, cap=cap, idx_bits=idx_bits),
        out_shape=[jax.ShapeDtypeStruct(aff3.shape, I32), jax.ShapeDtypeStruct(aff3.shape, I32)],
        compiler_params=pltpu.CompilerParams(vmem_limit_bytes=VMEM_LIMIT),
        name="route",
    )(aff3)


def _ffn_kernel(tend_ref, spos_ref, g_ref, x_ref, wg_ref, wu_ref, wd_ref, ye_ref,
                stage_sc, gate_sc, *, blk):
    e = pl.program_id(0)
    i = pl.program_id(1)
    prev_end = jnp.where(i > 0, tend_ref[e, jnp.maximum(i - 1, 0)], 0)
    base = (prev_end // blk) * blk
    fill = tend_ref[e, i] - base

    @pl.when(i == 0)
    def _():
        stage_sc[...] = jnp.zeros(stage_sc.shape, F32)
        gate_sc[...] = jnp.zeros(gate_sc.shape, F32)

    rel = spos_ref[0, 0] - base
    first = prev_end - base
    half = blk // 2

    def place(lo, width):
        hit = lax.broadcasted_iota(I32, (width, blk), 0) == rel - lo
        onehot = jnp.where(hit, 1.0, 0.0).astype(BF16)
        cols = slice(lo, lo + width)
        stage_sc[:, cols] += lax.dot_general(x_ref[...], onehot, _NT, preferred_element_type=F32)
        gate_sc[cols, :] += jnp.sum(jnp.where(hit, g_ref[0, 0], 0.0), axis=1, keepdims=True)

    in_low = fill <= half
    in_high = (first >= half) & (fill <= blk)

    @pl.when(in_low)
    def _():
        place(0, half)

    @pl.when(in_high)
    def _():
        place(half, half)

    @pl.when(jnp.logical_not(in_low | in_high))
    def _():
        place(0, blk)

    @pl.when(fill > blk)
    def _():
        place(blk, blk)

    @pl.when(fill >= blk)
    def _():
        xb = stage_sc[:, :blk].T.astype(BF16)
        gate = jnp.dot(xb, wg_ref[0], preferred_element_type=F32)
        up = jnp.dot(xb, wu_ref[0], preferred_element_type=F32)
        hid = (gate * jax.nn.sigmoid(gate) * up).astype(BF16)
        y = jnp.dot(hid, wd_ref[0], preferred_element_type=F32) * gate_sc[:blk, :1]
        for w in range(blk // SLOT_BLOCK):
            ye_ref[0, w] = y[w * SLOT_BLOCK:(w + 1) * SLOT_BLOCK, :].T.astype(ye_ref.dtype)
        stage_sc[:, :blk] = stage_sc[:, blk:]
        stage_sc[:, blk:] = jnp.zeros((stage_sc.shape[0], blk), F32)
        gate_sc[:blk, :] = gate_sc[blk:, :]
        gate_sc[blk:, :] = jnp.zeros((blk, LANES), F32)


def _ffn_call(tile_end, spos4, aff4, h2t, wg, wu, wd, cap, blk):
    d, n = h2t.shape
    n_t = n // blk
    ff = wg.shape[2]
    n_blk = cap // blk
    sub = blk // SLOT_BLOCK
    tt = blk

    def out_map(e, i, tend):
        prev_end = jnp.where(i > 0, tend[e, jnp.maximum(i - 1, 0)], 0)
        return (e, jnp.clip(prev_end // blk, 0, n_blk - 1), 0, 0)

    grid_spec = pltpu.PrefetchScalarGridSpec(
        num_scalar_prefetch=1,
        grid=(N_EXPERTS, n_t),
        in_specs=[
            pl.BlockSpec((1, 1, 1, tt), lambda e, i, tend: (e, i, 0, 0)),
            pl.BlockSpec((1, 1, 1, tt), lambda e, i, tend: (e, i, 0, 0)),
            pl.BlockSpec((d, tt), lambda e, i, tend: (0, i)),
            pl.BlockSpec((1, d, ff), lambda e, i, tend: (e, 0, 0)),
            pl.BlockSpec((1, d, ff), lambda e, i, tend: (e, 0, 0)),
            pl.BlockSpec((1, ff, d), lambda e, i, tend: (e, 0, 0)),
        ],
        out_specs=pl.BlockSpec((1, sub, d, SLOT_BLOCK), out_map),
        scratch_shapes=[
            pltpu.VMEM((d, 2 * blk), F32),
            pltpu.VMEM((2 * blk, LANES), F32),
        ],
    )
    return pl.pallas_call(
        functools.partial(_ffn_kernel, blk=blk),
        grid_spec=grid_spec,
        out_shape=jax.ShapeDtypeStruct((N_EXPERTS, n_blk * sub, d, SLOT_BLOCK), BF16),
        compiler_params=pltpu.CompilerParams(
            dimension_semantics=("arbitrary", "arbitrary"), vmem_limit_bytes=VMEM_LIMIT),
        name="ffn",
    )(tile_end, spos4, aff4, h2t, wg, wu, wd)


def _slot_span(tend, e, i):
    lo = jnp.where(i > 0, tend[e, jnp.maximum(i - 1, 0)], 0)
    return lo, tend[e, i]


def _combine_kernel(tend_ref, table_ref, spos_ref, x1_ref, mod_ref, g_ref, b_ref, *refs):
    first = refs[:N_EXPERTS]
    second = refs[N_EXPERTS:2 * N_EXPERTS]
    o_ref, acc_sc = refs[2 * N_EXPERTS:]
    i = pl.program_id(0)
    tt = x1_ref.shape[0]
    slot = lax.broadcasted_iota(I32, (SLOT_BLOCK, tt), 0)

    def onehot(e, block_offset):
        lo, _ = _slot_span(tend_ref, e, i)
        rel = spos_ref[e, 0] - (lo // SLOT_BLOCK + block_offset) * SLOT_BLOCK
        return jnp.where(slot == rel, 1.0, 0.0).astype(BF16)

    acc_sc[...] = jnp.dot(
        jnp.concatenate([first[e][0, 0] for e in range(N_EXPERTS)], axis=1),
        jnp.concatenate([onehot(e, 0) for e in range(N_EXPERTS)], axis=0),
        preferred_element_type=F32)

    for e in range(N_EXPERTS):
        lo, hi = _slot_span(tend_ref, e, i)

        @pl.when(hi > (lo // SLOT_BLOCK + 1) * SLOT_BLOCK)
        def _(e=e):
            acc_sc[...] += jnp.dot(second[e][0, 0], onehot(e, 1), preferred_element_type=F32)

    r = ALPHA * x1_ref[...] + mod_ref[0, 5:6, :] * acc_sc[...].T
    o_ref[...] = _layer_norm(r) * g_ref[...] + b_ref[...]


def _combine_call(tile_end, spos4, ye_t, x1, mod, ln_g, ln_b, tt, tokens_per_batch):
    n, d = x1.shape
    n_t = n // tt
    n_blk = ye_t.shape[1]
    tiles_per_batch = tokens_per_batch // tt

    lo = jnp.concatenate([jnp.zeros((N_EXPERTS, 1), I32), tile_end[:, :-1]], axis=1)
    nxt = lo // SLOT_BLOCK + 1
    spill = (tile_end > nxt * SLOT_BLOCK) & (nxt < n_blk)
    expert = jnp.arange(N_EXPERTS, dtype=I32)[:, None]
    table = jnp.concatenate([jnp.clip(lo // SLOT_BLOCK, 0, n_blk - 1),
                             jnp.where(spill, expert, 0),
                             jnp.where(spill, jnp.clip(nxt, 0, n_blk - 1), 0)], axis=0).astype(I32)

    def first_spec(e):
        return pl.BlockSpec((1, 1, d, SLOT_BLOCK), lambda i, tend, tab: (e, tab[e, i], 0, 0))

    def second_spec(e):
        return pl.BlockSpec((1, 1, d, SLOT_BLOCK),
                            lambda i, tend, tab: (tab[N_EXPERTS + e, i], tab[2 * N_EXPERTS + e, i], 0, 0))

    grid_spec = pltpu.PrefetchScalarGridSpec(
        num_scalar_prefetch=2,
        grid=(n_t,),
        in_specs=[
            pl.BlockSpec((N_EXPERTS, 1, 1, tt), lambda i, *_: (0, i, 0, 0)),
            pl.BlockSpec((tt, d), lambda i, *_: (i, 0)),
            pl.BlockSpec((1, 6, d), lambda i, *_: (i // tiles_per_batch, 0, 0)),
            pl.BlockSpec((1, d), lambda i, *_: (0, 0)),
            pl.BlockSpec((1, d), lambda i, *_: (0, 0)),
            *[first_spec(e) for e in range(N_EXPERTS)],
            *[second_spec(e) for e in range(N_EXPERTS)],
        ],
        out_specs=pl.BlockSpec((tt, d), lambda i, *_: (i, 0)),
        scratch_shapes=[pltpu.VMEM((d, tt), F32)],
    )
    return pl.pallas_call(
        _combine_kernel,
        grid_spec=grid_spec,
        out_shape=jax.ShapeDtypeStruct((n, d), F32),
        compiler_params=pltpu.CompilerParams(
            dimension_semantics=("parallel",), vmem_limit_bytes=VMEM_LIMIT),
        name="combine",
    )(tile_end, table, spos4, x1, mod, ln_g, ln_b, *([ye_t] * (2 * N_EXPERTS)))


def _rope_tables(s):
    pos = jnp.arange(s)
    half = HEAD_DIM // 2
    inv_freq = ROPE_THETA ** (-jnp.arange(0, half, 2, dtype=F32) / half)
    ang_r = (pos // GRID_W).astype(F32)[:, None] * inv_freq[None, :]
    ang_c = (pos % GRID_W).astype(F32)[:, None] * inv_freq[None, :]
    zero = jnp.zeros_like(ang_r)
    cos = jnp.concatenate([jnp.cos(ang_r)] * 2 + [jnp.cos(ang_c)] * 2, axis=1)
    slo = jnp.concatenate([-jnp.sin(ang_r), zero, -jnp.sin(ang_c), zero], axis=1)
    shi = jnp.concatenate([zero, jnp.sin(ang_r), zero, jnp.sin(ang_c)], axis=1)
    rep = LANES // HEAD_DIM
    return tuple(jnp.concatenate([t] * rep, axis=1) for t in (cos, slo, shi))


def _bf16_terms(value, n=3):
    terms, rem = [], np.float32(value)
    for _ in range(n):
        part = np.float32(np.asarray(rem, dtype=jnp.bfloat16))
        terms.append(float(part))
        rem = np.float32(rem - part)
    return terms


def _alibi_constants(t):
    cq = np.zeros((N_HEADS_A, LANES), np.float32)
    ck = np.zeros((N_HEADS_A, LANES), np.float32)
    d = jnp.arange(t)
    dist = jnp.abs(d[:, None] - d[None, :]).astype(F32)
    db, slopes_c = [], []
    for h in range(N_HEADS_A):
        terms = _bf16_terms(2.0 ** (-8.0 * (h + 1) / N_HEADS_A) * LOG2E)
        cq[h, 0:6] = terms + terms
        ck[h, 6:12] = terms + terms
        slopes_c.append(np.float32(sum(terms)))
        db.append(-slopes_c[-1] * dist)
    return jnp.asarray(cq), jnp.asarray(ck), jnp.stack(db), slopes_c


def _tile(dim, target):
    t = min(dim, target)
    assert dim % t == 0, (dim, t)
    return t


def _trunk(x, mod, lam, consts, w):
    b, s, d = x.shape
    n = b * s
    cap = CAPACITY_FACTOR * n // N_EXPERTS
    assert cap % SLOT_BLOCK == 0 and n % LANES == 0
    tm = _tile(s, 512)
    tt = _tile(n, SLOT_BLOCK)

    ta = _tile(s, 512)
    cq, ck, dbias, slopes_c = _alibi_constants(ta)
    assert tm == ta
    qa, ka, va, qb, kb, vb, norms = _proj_call(
        x, mod, w["w_in"], w["gq"], w["gk"], *consts["rope"], consts["bd"], cq, ck, tm)
    n_sub = min(ATTN_SUBTILES, s // ta)
    assert (s // ta) % n_sub == 0, (s, ta, n_sub)
    order = _kv_visit_order(s // ta)
    skip = _skip_flags(norms, order, slopes_c, ta, n_sub)
    oa = _diff_attn_call(qa, ka, va, dbias, lam, w["subln_g"], jnp.asarray(order), skip, ta, n_sub)
    ob = _gqa_call(qb, kb, vb, _tile(s, 256), ta, n_sub)
    x1, h2t, aff_t = _post_call(oa, ob, x, mod, w["w_out"], w["ln1_g"], w["ln1_b"],
                                w["w_router_t"], tm)

    spos, cnt = _route_call(aff_t.reshape(N_EXPERTS, n // LANES, LANES), cap)
    cnt = cnt.reshape(N_EXPERTS, n)

    def tiles(a, width):
        return a.reshape(N_EXPERTS, n // width, 1, width)

    def tile_ends(width):
        return cnt[:, width - 1::width]

    blk = math.gcd(FFN_BLOCK, cap)
    ye_t = _ffn_call(tile_ends(blk), tiles(spos, blk), tiles(aff_t, blk), h2t,
                     w["w_gate"], w["w_up"], w["w_down"], cap, blk)
    out = _combine_call(tile_ends(tt), tiles(spos, tt), ye_t, x1.reshape(n, d), mod,
                        w["ln2_g"], w["ln2_b"], tt, s)
    return out.reshape(b, s, d)


def kernel(x_prompt, x_sample, c_prompt, c_sample, w_ada, b_ada, w_in, lam_q1, lam_k1, lam_q2,
           lam_k2, subln_g, q_norm_g, k_norm_g, w_out, ln1_g, ln1_b, w_router, w_gate, w_up,
           w_down, ln2_g, ln2_b):
    l = 0
    d = x_prompt.shape[-1]
    c_all = jnp.concatenate([c_prompt, c_sample], axis=0)
    mod, lam = _mod_call(c_all, w_ada[l], b_ada[l][None, :], lam_q1[l][None, :], lam_k1[l][None, :],
                         lam_q2[l][None, :], lam_k2[l][None, :])
    mod = mod.reshape(c_all.shape[0], 6, d)

    w = {
        "w_in": w_in[l].astype(BF16),
        "gq": jnp.tile(q_norm_g[l], N_HEADS_B)[None, :],
        "gk": jnp.tile(k_norm_g[l], N_KV_B)[None, :],
        "subln_g": subln_g[l][None, :],
        "w_out": w_out[l].astype(BF16),
        "ln1_g": ln1_g[l][None, :], "ln1_b": ln1_b[l][None, :],
        "w_router_t": w_router[l].T,
        "w_gate": w_gate[l].astype(BF16), "w_up": w_up[l].astype(BF16),
        "w_down": w_down[l].astype(BF16),
        "ln2_g": ln2_g[l][None, :], "ln2_b": ln2_b[l][None, :],
    }
    head_of = jnp.arange(WIDTH_B) // HEAD_DIM
    shared = {"bd": (head_of[:, None] == head_of[None, :]).astype(BF16)}

    outs = []
    nb = x_prompt.shape[0]
    for x, m in ((x_prompt, mod[:nb]), (x_sample, mod[nb:])):
        consts = dict(shared, rope=_rope_tables(x.shape[1]))
        outs.append(_trunk(x, m, lam, consts, w))
    return tuple(outs)
```

```python
import functools
import math

import jax
import jax.numpy as jnp
import numpy as np
from jax import lax
from jax.experimental import pallas as pl
from jax.experimental.pallas import tpu as pltpu

F32 = jnp.float32
BF16 = jnp.bfloat16
I32 = jnp.int32

HEAD_DIM = 64
N_HEADS_A = 4
WIDTH_A = N_HEADS_A * 2 * HEAD_DIM
N_HEADS_B = 8
N_KV_B = 2
GROUP_B = N_HEADS_B // N_KV_B
WIDTH_B = N_HEADS_B * HEAD_DIM
KV_B = N_KV_B * HEAD_DIM
GRID_W = 64
ROPE_THETA = 10000.0
N_EXPERTS = 16
CAPACITY_FACTOR = 2
DEPTH = 1
ALPHA = (2.0 * DEPTH) ** 0.25
LN_EPS = 1e-5
RMS_EPS = 1e-6
LAM_INIT = 0.8 - 0.6 * math.exp(-0.3 * 0)

LOG2E = 1.4426950408889634
SOFTMAX_CHUNK = 16
SKIP_EXPONENT = 170.0
N_BIAS_COLS = 12
ATTN_SUBTILES = 8

LANES = 128
SLOT_BLOCK = 256
FFN_BLOCK = 512
VMEM_LIMIT = 56 * 1024 * 1024

_NT = (((1,), (1,)), ((), ()))


def _split_bf16(a):
    hi = a.astype(BF16)
    lo = (a - hi.astype(F32)).astype(BF16)
    return hi, lo


def _dot3(a, b, dims=(((1,), (0,)), ((), ()))):
    ah, al = _split_bf16(a)
    bh, bl = _split_bf16(b)
    d = functools.partial(lax.dot_general, dimension_numbers=dims, preferred_element_type=F32)
    return d(ah, bh) + (d(ah, bl) + d(al, bh))


def _layer_norm(x):
    mu = jnp.mean(x, axis=-1, keepdims=True)
    xc = x - mu
    var = jnp.mean(xc * xc, axis=-1, keepdims=True)
    return xc * lax.rsqrt(var + LN_EPS)


def _lanes(x, n):
    if n == LANES:
        return x
    if n < LANES:
        return x[:, :n]
    return jnp.concatenate([x] * (n // LANES), axis=1)


def _mod_kernel(c_ref, w_ref, b_ref, lq1_ref, lk1_ref, lq2_ref, lk2_ref, mod_ref, lam_ref):
    c = c_ref[...]
    a = c * jax.nn.sigmoid(c)
    mod_ref[...] = _dot3(a, w_ref[...]) + b_ref[...]
    s1 = jnp.sum(lq1_ref[...] * lk1_ref[...], axis=-1, keepdims=True)
    s2 = jnp.sum(lq2_ref[...] * lk2_ref[...], axis=-1, keepdims=True)
    lam = jnp.exp(s1) - jnp.exp(s2) + LAM_INIT
    lam_ref[...] = jnp.broadcast_to(lam, lam_ref.shape)


def _mod_call(c, w_ada, b_ada, lq1, lk1, lq2, lk2):
    bt, d = c.shape
    n_chunks = w_ada.shape[1] // d
    vec = pl.BlockSpec((1, HEAD_DIM), lambda j: (0, 0))
    return pl.pallas_call(
        _mod_kernel,
        grid=(n_chunks,),
        in_specs=[
            pl.BlockSpec((bt, d), lambda j: (0, 0)),
            pl.BlockSpec((d, d), lambda j: (0, j)),
            pl.BlockSpec((1, d), lambda j: (0, j)),
            vec, vec, vec, vec,
        ],
        out_specs=[
            pl.BlockSpec((bt, d), lambda j: (0, j)),
            pl.BlockSpec((1, LANES), lambda j: (0, 0)),
        ],
        out_shape=[
            jax.ShapeDtypeStruct((bt, n_chunks * d), F32),
            jax.ShapeDtypeStruct((1, LANES), F32),
        ],
        compiler_params=pltpu.CompilerParams(vmem_limit_bytes=VMEM_LIMIT),
        name="mod",
    )(c, w_ada, b_ada, lq1, lk1, lq2, lk2)


def _head_rms(x, bd, g):
    sq = x * x
    hi, lo = _split_bf16(sq)
    ss = jnp.dot(hi, bd, preferred_element_type=F32) + jnp.dot(lo, bd, preferred_element_type=F32)
    return x * lax.rsqrt(ss * (1.0 / HEAD_DIM) + RMS_EPS) * g


def _rope(x, cos, sin_lo, sin_hi):
    w = x.shape[1]
    nxt = pltpu.roll(x, w - 16, axis=1)
    prv = pltpu.roll(x, 16, axis=1)
    return x * cos + nxt * sin_lo + prv * sin_hi


def _proj_kernel(x_ref, mod_ref, w_ref, gq_ref, gk_ref, cos_ref, slo_ref, shi_ref, bd_ref, cq_ref, ck_ref,
                 qa_ref, ka_ref, va_ref, qb_ref, kb_ref, vb_ref, norm_ref):
    x = x_ref[0]
    h = _layer_norm(x) * (1.0 + mod_ref[0, 1:2, :]) + mod_ref[0, 0:1, :]
    hb = h.astype(BF16)

    def proj(lo, width):
        return jnp.dot(hb, w_ref[:, lo:lo + width], preferred_element_type=F32)

    scale = HEAD_DIM ** -0.5 * LOG2E
    qa = (proj(0, WIDTH_A) * scale).astype(BF16)
    ka = proj(WIDTH_A, WIDTH_A).astype(BF16)
    hw = 2 * HEAD_DIM
    tm = x.shape[0]
    pos = pl.program_id(1) * tm + lax.broadcasted_iota(I32, (tm, LANES), 0)
    pos_hi = ((pos // LANES) * LANES).astype(F32)
    pos_lo = (pos % LANES).astype(F32)
    lane = lax.broadcasted_iota(I32, (tm, LANES), 1)
    zero = jnp.zeros((tm, LANES), F32)
    for hd in range(N_HEADS_A):
        lo = hd * (hw + LANES)
        bias_q = jnp.where(lane < 6, cq_ref[hd:hd + 1, :],
                           jnp.where(lane < 9, pos_hi, jnp.where(lane < N_BIAS_COLS, pos_lo, zero)))
        bias_k = jnp.where(lane < 3, -pos_hi,
                           jnp.where(lane < 6, -pos_lo, jnp.where(lane < N_BIAS_COLS, ck_ref[hd:hd + 1, :], zero)))
        qa_ref[0, :, lo:lo + hw] = qa[:, hd * hw:(hd + 1) * hw]
        qa_ref[0, :, lo + hw:lo + hw + LANES] = bias_q.astype(BF16)
        ka_ref[0, :, lo:lo + hw] = ka[:, hd * hw:(hd + 1) * hw]
        ka_ref[0, :, lo + hw:lo + hw + LANES] = bias_k.astype(BF16)
        for row, val in ((hd, qa), (N_HEADS_A + hd, ka)):
            v32 = val[:, hd * hw:(hd + 1) * hw].astype(F32)
            n2 = jnp.max(jnp.sum(v32 * v32, axis=1, keepdims=True), axis=0, keepdims=True)
            norm_ref[0, 0, row:row + 1, :] = jnp.broadcast_to(n2, (1, LANES))
    va_ref[0] = proj(2 * WIDTH_A, WIDTH_A).astype(BF16)

    cos, slo, shi = cos_ref[...], slo_ref[...], shi_ref[...]
    bd = bd_ref[...]
    qb = _head_rms(proj(3 * WIDTH_A, WIDTH_B), bd, gq_ref[...])
    qb = _rope(qb, _lanes(cos, WIDTH_B), _lanes(slo, WIDTH_B), _lanes(shi, WIDTH_B)) * scale
    for hd in range(N_HEADS_B):
        qb_ref[0, hd] = qb[:, hd * HEAD_DIM:(hd + 1) * HEAD_DIM].astype(BF16)
    kb = _head_rms(proj(3 * WIDTH_A + WIDTH_B, KV_B), bd[:KV_B, :KV_B], gk_ref[...])
    kb = _rope(kb, cos, slo, shi)
    vb = proj(3 * WIDTH_A + WIDTH_B + KV_B, KV_B)
    ones_col = jnp.where(lane == HEAD_DIM, 1.0, 0.0)
    for hd in range(N_KV_B):
        kb_ref[0, hd] = kb[:, hd * HEAD_DIM:(hd + 1) * HEAD_DIM].astype(BF16)
        v_first = vb if hd == 0 else pltpu.roll(vb, KV_B - hd * HEAD_DIM, axis=1)
        vb_ref[0, hd] = jnp.where(lane < HEAD_DIM, v_first, ones_col).astype(BF16)


def _proj_call(x, mod, w_in, gq, gk, cos, slo, shi, bd, cq, ck, tm):
    b, s, d = x.shape
    nt = s // tm
    wide_a = N_HEADS_A * (2 * HEAD_DIM + LANES)
    bias_tab = pl.BlockSpec((N_HEADS_A, LANES), lambda bi, ti: (0, 0))
    const = lambda shape: pl.BlockSpec(shape, lambda bi, ti: (0,) * len(shape))
    tok = lambda width: pl.BlockSpec((1, tm, width), lambda bi, ti: (bi, ti, 0))
    heads = lambda n, width=HEAD_DIM: pl.BlockSpec((1, n, tm, width), lambda bi, ti: (bi, 0, ti, 0))
    tab = pl.BlockSpec((tm, LANES), lambda bi, ti: (ti, 0))
    return pl.pallas_call(
        _proj_kernel,
        grid=(b, nt),
        in_specs=[
            tok(d),
            pl.BlockSpec((1, 6, d), lambda bi, ti: (bi, 0, 0)),
            const(w_in.shape),
            const(gq.shape), const(gk.shape),
            tab, tab, tab,
            const(bd.shape),
            bias_tab, bias_tab,
        ],
        out_specs=[tok(wide_a), tok(wide_a), tok(WIDTH_A),
                   heads(N_HEADS_B), heads(N_KV_B), heads(N_KV_B, LANES),
                   pl.BlockSpec((1, 1, 2 * N_HEADS_A, LANES), lambda bi, ti: (bi, ti, 0, 0))],
        out_shape=[
            jax.ShapeDtypeStruct((b, s, wide_a), BF16),
            jax.ShapeDtypeStruct((b, s, wide_a), BF16),
            jax.ShapeDtypeStruct((b, s, WIDTH_A), BF16),
            jax.ShapeDtypeStruct((b, N_HEADS_B, s, HEAD_DIM), BF16),
            jax.ShapeDtypeStruct((b, N_KV_B, s, HEAD_DIM), BF16),
            jax.ShapeDtypeStruct((b, N_KV_B, s, LANES), BF16),
            jax.ShapeDtypeStruct((b, nt, 2 * N_HEADS_A, LANES), F32),
        ],
        compiler_params=pltpu.CompilerParams(
            dimension_semantics=("parallel", "parallel"), vmem_limit_bytes=VMEM_LIMIT),
        name="proj",
    )(x, mod, w_in, gq, gk, cos, slo, shi, bd, cq, ck)


def _lane_blocks(x, op):
    out = x[:, :LANES]
    for j in range(1, x.shape[1] // LANES):
        out = op(out, x[:, j * LANES:(j + 1) * LANES])
    return out


def _fold_row_max(s_ref, mt_ref, first):
    rows = s_ref.shape[0]
    for r0 in range(0, rows, SOFTMAX_CHUNK):
        sl = slice(r0, r0 + SOFTMAX_CHUNK)
        part = _lane_blocks(s_ref[sl, :], jnp.maximum)
        mt_ref[sl, :] = part if first else jnp.maximum(mt_ref[sl, :], part)


def _raise_max(mt_ref, m_ref, l_ref, acc_ref):
    rows = m_ref.shape[0]
    width = acc_ref.shape[-1]
    for r0 in range(0, rows, 4 * SOFTMAX_CHUNK):
        sl = slice(r0, r0 + 4 * SOFTMAX_CHUNK)
        m_prev = m_ref[sl, :]
        m_new = jnp.maximum(m_prev, jnp.max(mt_ref[sl, :], axis=1, keepdims=True))
        alpha = jnp.exp2(m_prev - m_new)
        m_ref[sl, :] = m_new
        if l_ref is not None:
            l_ref[sl, :] = alpha * l_ref[sl, :]
        acc_ref[sl, :] = _lanes(alpha, width) * acc_ref[sl, :]


def _write_probs(s_ref, p_ref, col0, m_ref, l_ref):
    rows, tk = s_ref.shape
    for r0 in range(0, rows, SOFTMAX_CHUNK):
        sl = slice(r0, r0 + SOFTMAX_CHUNK)
        p = jnp.exp2(s_ref[sl, :] - _lanes(m_ref[sl, :], tk))
        if l_ref is not None:
            l_ref[sl, :] += _lane_blocks(p, jnp.add)
        p_ref[sl, col0:col0 + tk] = p.astype(BF16)


def _softmax_update(s_ref, p_ref, m_ref, l_ref, acc_ref, bias_ref=None):
    rows, tk = s_ref.shape
    width = acc_ref.shape[-1]
    for r0 in range(0, rows, SOFTMAX_CHUNK):
        sl = slice(r0, r0 + SOFTMAX_CHUNK)
        s = s_ref[sl, :]
        if bias_ref is not None:
            s = s + bias_ref[sl, :]
            s_ref[sl, :] = s
        m_prev = m_ref[sl, :]
        m_new = jnp.maximum(m_prev, jnp.max(s, axis=1, keepdims=True))
        alpha = jnp.exp2(m_prev - m_new)
        m_ref[sl, :] = m_new
        l_ref[sl, :] = alpha * l_ref[sl, :]
        acc_ref[sl, :] = _lanes(alpha, width) * acc_ref[sl, :]
    _write_probs(s_ref, p_ref, 0, m_ref, l_ref)


def _diff_attn_kernel(order_ref, skip_ref, lam_ref, g_ref, dbias_ref, q_ref, *refs, n_sub):
    k_refs, v_refs = refs[:n_sub], refs[n_sub:2 * n_sub]
    o_ref, qv_sc, s_sc, p_sc, m_sc, l_sc, acc_sc = refs[2 * n_sub:]
    bi = pl.program_id(0)
    h = pl.program_id(1)
    qi = pl.program_id(2)
    ki = pl.program_id(3)
    n_q = pl.num_programs(2)
    n_k = pl.num_programs(3)
    hw = 2 * HEAD_DIM

    @pl.when(ki == 0)
    def _():
        m_sc[...] = jnp.full(m_sc.shape, -jnp.inf, F32)
        l_sc[...] = jnp.zeros(l_sc.shape, F32)
        acc_sc[...] = jnp.zeros(acc_sc.shape, F32)
        q = q_ref[0]
        zero = jnp.zeros_like(q)
        lane = lax.broadcasted_iota(I32, q.shape, 1)
        is_bias = lane >= hw
        for c in range(2):
            qc = jnp.where(is_bias | ((lane >= c * HEAD_DIM) & (lane < (c + 1) * HEAD_DIM)), q, zero)
            qv_sc[0, c] = qc
            qv_sc[1, c] = jnp.where(is_bias, -qc, qc)
            qv_sc[2, c] = jnp.where(is_bias, zero, qc)

    def body(starts_on_diagonal):
        for j in range(n_sub):
            diagonal = starts_on_diagonal and j == 0
            kt = order_ref[qi, ki * n_sub + j]
            version = 2 if diagonal else jnp.where(kt > qi, 0, 1)
            for c in range(2):
                s_sc[j, c] = lax.dot_general(qv_sc[version, c], k_refs[j][0], _NT,
                                             preferred_element_type=F32)
        for j in range(n_sub):
            diagonal = starts_on_diagonal and j == 0
            for c in range(2):
                _softmax_update(s_sc.at[j, c], p_sc.at[j, c], m_sc.at[c], l_sc.at[c], acc_sc.at[c],
                                dbias_ref.at[0] if diagonal else None)
                acc_sc[c] += jnp.dot(p_sc[j, c], v_refs[j][0], preferred_element_type=F32)

    @pl.when(ki == 0)
    def _():
        body(True)

    @pl.when((ki != 0) & (skip_ref[((bi * N_HEADS_A + h) * n_q + qi) * n_k + ki] == 0))
    def _():
        body(False)

    @pl.when(ki == n_k - 1)
    def _():
        l0 = jnp.sum(l_sc[0], axis=1, keepdims=True)
        l1 = jnp.sum(l_sc[1], axis=1, keepdims=True)
        o = acc_sc[0] / l0 - lam_ref[...] * (acc_sc[1] / l1)
        ms = jnp.mean(o * o, axis=-1, keepdims=True)
        o = o * lax.rsqrt(ms + RMS_EPS) * g_ref[...] * (1.0 - LAM_INIT)
        o_ref[0] = o.astype(o_ref.dtype)


def _kv_visit_order(n_kt):
    order = np.zeros((n_kt, n_kt), np.int32)
    for qi in range(n_kt):
        order[qi] = sorted(range(n_kt), key=lambda kt: (abs(kt - qi), qi - kt))
    return order


def _skip_flags(norms, order, slopes_c, t, n_sub):
    n_kt = order.shape[0]
    qn = jnp.sqrt(norms[:, :, :N_HEADS_A, 0])
    kn = jnp.sqrt(norms[:, :, N_HEADS_A:, 0])
    gap = (np.maximum(np.abs(order - np.arange(n_kt)[:, None]) - 1, 0) * t + 1).astype(np.float32)
    kn_seen = kn[:, order, :]
    bound = (qn[:, :, None, :] * (kn_seen + kn[:, :, None, :]) * 1.001
             - jnp.asarray(slopes_c, F32)[None, None, None, :] * gap[None, :, :, None])
    dead = (bound < -SKIP_EXPONENT).reshape(norms.shape[0], n_kt, n_kt // n_sub, n_sub, N_HEADS_A)
    flag = jnp.all(dead, axis=3).at[:, :, 0, :].set(False)
    return flag.transpose(0, 3, 1, 2).astype(I32).reshape(-1)


def _diff_attn_call(qa, ka, va, dbias, lam, g, order, skip, t, n_sub):
    b, s, _ = va.shape
    hw = 2 * HEAD_DIM
    qw = hw + LANES
    n_kt = s // t
    n_k = n_kt // n_sub

    def kv_spec(width, j):
        def index(bi, h, qi, ki, order_ref, skip_ref):
            dead = skip_ref[((bi * N_HEADS_A + h) * n_kt + qi) * n_k + ki] != 0
            return (bi, order_ref[qi, jnp.where(dead, 0, ki) * n_sub + j], h)
        return pl.BlockSpec((1, t, width), index)

    grid_spec = pltpu.PrefetchScalarGridSpec(
        num_scalar_prefetch=2,
        grid=(b, N_HEADS_A, n_kt, n_k),
        in_specs=[
            pl.BlockSpec((1, LANES), lambda bi, h, qi, ki, *_: (0, 0)),
            pl.BlockSpec((1, hw), lambda bi, h, qi, ki, *_: (0, 0)),
            pl.BlockSpec((1, t, t), lambda bi, h, qi, ki, *_: (h, 0, 0)),
            pl.BlockSpec((1, t, qw), lambda bi, h, qi, ki, *_: (bi, qi, h)),
            *[kv_spec(qw, j) for j in range(n_sub)],
            *[kv_spec(hw, j) for j in range(n_sub)],
        ],
        out_specs=pl.BlockSpec((1, t, hw), lambda bi, h, qi, ki, *_: (bi, qi, h)),
        scratch_shapes=[
            pltpu.VMEM((3, 2, t, qw), BF16),
            pltpu.VMEM((n_sub, 2, t, t), F32),
            pltpu.VMEM((n_sub, 2, t, t), BF16),
            pltpu.VMEM((2, t, LANES), F32),
            pltpu.VMEM((2, t, LANES), F32),
            pltpu.VMEM((2, t, hw), F32),
        ],
    )
    return pl.pallas_call(
        functools.partial(_diff_attn_kernel, n_sub=n_sub),
        grid_spec=grid_spec,
        out_shape=jax.ShapeDtypeStruct((b, s, WIDTH_A), BF16),
        compiler_params=pltpu.CompilerParams(
            dimension_semantics=("parallel", "parallel", "parallel", "arbitrary"),
            vmem_limit_bytes=VMEM_LIMIT),
        name="diff_attn",
    )(order, skip, lam, g, dbias, qa, *([ka] * n_sub), *([va] * n_sub))


def _gqa_kernel(q_ref, k_ref, v_ref, o_ref, s_sc, p_sc, mt_sc, m_sc, acc_sc, *, tq, tk):
    ki = pl.program_id(3)
    rows = GROUP_B * tq

    @pl.when(ki == 0)
    def _():
        m_sc[...] = jnp.full(m_sc.shape, -jnp.inf, F32)
        acc_sc[...] = jnp.zeros(acc_sc.shape, F32)

    n_sub = k_ref.shape[2] // tk
    q = q_ref[0].reshape(rows, HEAD_DIM)
    for j in range(n_sub):
        s_sc[j] = lax.dot_general(q, k_ref[0, 0, j * tk:(j + 1) * tk, :], _NT,
                                  preferred_element_type=F32)
    for j in range(n_sub):
        _fold_row_max(s_sc.at[j], mt_sc, j == 0)
    _raise_max(mt_sc, m_sc, None, acc_sc)
    for j in range(n_sub):
        _write_probs(s_sc.at[j], p_sc, j * tk, m_sc, None)
    acc_sc[...] += jnp.dot(p_sc[...], v_ref[0, 0], preferred_element_type=F32)

    @pl.when(ki == pl.num_programs(3) - 1)
    def _():
        acc = acc_sc[...]
        o = acc[:, :HEAD_DIM] / acc[:, HEAD_DIM:HEAD_DIM + 1]
        o_ref[0] = o.reshape(GROUP_B, tq, HEAD_DIM).astype(o_ref.dtype)


def _gqa_call(qb, kb, vb, tq, tk, n_sub):
    b, _, s, _ = qb.shape
    rows = GROUP_B * tq
    kv = n_sub * tk
    return pl.pallas_call(
        functools.partial(_gqa_kernel, tq=tq, tk=tk),
        grid=(b, N_KV_B, s // tq, s // kv),
        in_specs=[
            pl.BlockSpec((1, GROUP_B, tq, HEAD_DIM), lambda bi, n, qi, ki: (bi, n, qi, 0)),
            pl.BlockSpec((1, 1, kv, HEAD_DIM), lambda bi, n, qi, ki: (bi, n, ki, 0)),
            pl.BlockSpec((1, 1, kv, LANES), lambda bi, n, qi, ki: (bi, n, ki, 0)),
        ],
        out_specs=pl.BlockSpec((1, GROUP_B, tq, HEAD_DIM), lambda bi, n, qi, ki: (bi, n, qi, 0)),
        out_shape=jax.ShapeDtypeStruct((b, N_HEADS_B, s, HEAD_DIM), BF16),
        scratch_shapes=[
            pltpu.VMEM((n_sub, rows, tk), F32),
            pltpu.VMEM((rows, n_sub * tk), BF16),
            pltpu.VMEM((rows, LANES), F32),
            pltpu.VMEM((rows, LANES), F32),
            pltpu.VMEM((rows, LANES), F32),
        ],
        compiler_params=pltpu.CompilerParams(
            dimension_semantics=("parallel", "parallel", "parallel", "arbitrary"),
            vmem_limit_bytes=VMEM_LIMIT),
        name="gqa_attn",
    )(qb, kb, vb)


def _post_kernel(oa_ref, ob_ref, x_ref, mod_ref, w_ref, g_ref, b_ref, wrt_ref,
                 x1_ref, h2_ref, aff_ref):
    o = jnp.dot(oa_ref[0], w_ref[:WIDTH_A, :], preferred_element_type=F32)
    for hd in range(N_HEADS_B):
        lo = WIDTH_A + hd * HEAD_DIM
        o = o + jnp.dot(ob_ref[0, hd], w_ref[lo:lo + HEAD_DIM, :], preferred_element_type=F32)
    r = ALPHA * x_ref[0] + mod_ref[0, 2:3, :] * o
    x1 = _layer_norm(r) * g_ref[...] + b_ref[...]
    x1_ref[0] = x1
    h2 = _layer_norm(x1) * (1.0 + mod_ref[0, 4:5, :]) + mod_ref[0, 3:4, :]
    h2_ref[...] = h2.T.astype(BF16)
    logits = _dot3(wrt_ref[...], h2, _NT)
    z = jnp.exp(logits - jnp.max(logits, axis=0, keepdims=True))
    aff_ref[...] = z / jnp.sum(z, axis=0, keepdims=True)


def _post_call(oa, ob, x, mod, w_out, ln_g, ln_b, w_router_t, tm):
    b, s, d = x.shape
    nt = s // tm
    const = lambda shape: pl.BlockSpec(shape, lambda bi, ti: (0,) * len(shape))
    tok = lambda width: pl.BlockSpec((1, tm, width), lambda bi, ti: (bi, ti, 0))
    return pl.pallas_call(
        _post_kernel,
        grid=(b, nt),
        in_specs=[
            tok(WIDTH_A),
            pl.BlockSpec((1, N_HEADS_B, tm, HEAD_DIM), lambda bi, ti: (bi, 0, ti, 0)),
            tok(d),
            pl.BlockSpec((1, 6, d), lambda bi, ti: (bi, 0, 0)),
            const(w_out.shape), const(ln_g.shape), const(ln_b.shape), const(w_router_t.shape),
        ],
        out_specs=[tok(d),
                   pl.BlockSpec((d, tm), lambda bi, ti: (0, bi * nt + ti)),
                   pl.BlockSpec((N_EXPERTS, tm), lambda bi, ti: (0, bi * nt + ti))],
        out_shape=[
            jax.ShapeDtypeStruct((b, s, d), F32),
            jax.ShapeDtypeStruct((d, b * s), BF16),
            jax.ShapeDtypeStruct((N_EXPERTS, b * s), F32),
        ],
        compiler_params=pltpu.CompilerParams(
            dimension_semantics=("parallel", "parallel"), vmem_limit_bytes=VMEM_LIMIT),
        name="post",
    )(oa, ob, x, mod, w_out, ln_g, ln_b, w_router_t)


def _route_kernel(aff_ref, spos_ref, cnt_ref, *, cap, idx_bits):
    aff = aff_ref[...]
    n_e, n_r, _ = aff.shape
    bits = pltpu.bitcast(aff, I32)
    idx = (lax.broadcasted_iota(I32, aff.shape, 1) * LANES
           + lax.broadcasted_iota(I32, aff.shape, 2))

    def count(pred):
        part = jnp.sum(jnp.where(pred, 1.0, 0.0), axis=1, keepdims=True)
        return jnp.sum(part, axis=2, keepdims=True)

    def value_step(t, prefix):
        cand = prefix | (1 << (30 - t))
        return jnp.where(count(bits >= cand) >= cap, cand, prefix)

    thr = lax.fori_loop(0, 31, value_step, jnp.zeros((n_e, 1, 1), I32))
    above = bits > thr
    tie = bits == thr
    need = cap - count(above)

    def index_step(t, prefix):
        cand = prefix | (1 << (idx_bits - 1 - t))
        return jnp.where(count(tie & (idx < cand)) < need, cand, prefix)

    last = lax.fori_loop(0, idx_bits, index_step, jnp.zeros((n_e, 1, 1), I32))
    sel = above | (tie & (idx <= last))

    upper = (lax.broadcasted_iota(I32, (LANES, LANES), 0)
             <= lax.broadcasted_iota(I32, (LANES, LANES), 1)).astype(BF16)
    strict_lower = (lax.broadcasted_iota(I32, (n_r, n_r), 1)
                    < lax.broadcasted_iota(I32, (n_r, n_r), 0)).astype(BF16)
    for e in range(n_e):
        sel_e = sel[e]
        m = sel_e.astype(BF16)
        incl = jnp.dot(m, upper, preferred_element_type=F32)
        row_tot = jnp.broadcast_to(incl[:, LANES - 1:], incl.shape).astype(BF16)
        offs = jnp.dot(strict_lower, row_tot, preferred_element_type=F32)
        cnt = (incl + offs).astype(I32)
        cnt_ref[e] = cnt
        spos_ref[e] = jnp.where(sel_e, cnt - 1, -1)


def _route_call(aff3, cap):
    n_e, n_r, _ = aff3.shape
    idx_bits = max(1, (n_r * LANES - 1).bit_length())
    return pl.pallas_call(
        functools.partial(_route_kernel, cap=cap, idx_bits=idx_bits),
        out_shape=[jax.ShapeDtypeStruct(aff3.shape, I32), jax.ShapeDtypeStruct(aff3.shape, I32)],
        compiler_params=pltpu.CompilerParams(vmem_limit_bytes=VMEM_LIMIT),
        name="route",
    )(aff3)


def _ffn_kernel(tend_ref, spos_ref, g_ref, x_ref, wg_ref, wu_ref, wd_ref, ye_ref,
                stage_sc, gate_sc, *, blk):
    e = pl.program_id(0)
    i = pl.program_id(1)
    prev_end = jnp.where(i > 0, tend_ref[e, jnp.maximum(i - 1, 0)], 0)
    base = (prev_end // blk) * blk
    fill = tend_ref[e, i] - base

    @pl.when(i == 0)
    def _():
        stage_sc[...] = jnp.zeros(stage_sc.shape, F32)
        gate_sc[...] = jnp.zeros(gate_sc.shape, F32)

    rel = spos_ref[0, 0] - base
    first = prev_end - base
    half = blk // 2

    def place(lo, width):
        hit = lax.broadcasted_iota(I32, (width, blk), 0) == rel - lo
        onehot = jnp.where(hit, 1.0, 0.0).astype(BF16)
        cols = slice(lo, lo + width)
        stage_sc[:, cols] += lax.dot_general(x_ref[...], onehot, _NT, preferred_element_type=F32)
        gate_sc[cols, :] += jnp.sum(jnp.where(hit, g_ref[0, 0], 0.0), axis=1, keepdims=True)

    in_low = fill <= half
    in_high = (first >= half) & (fill <= blk)

    @pl.when(in_low)
    def _():
        place(0, half)

    @pl.when(in_high)
    def _():
        place(half, half)

    @pl.when(jnp.logical_not(in_low | in_high))
    def _():
        place(0, blk)

    @pl.when(fill > blk)
    def _():
        place(blk, blk)

    @pl.when(fill >= blk)
    def _():
        xb = stage_sc[:, :blk].T.astype(BF16)
        gate = jnp.dot(xb, wg_ref[0], preferred_element_type=F32)
        up = jnp.dot(xb, wu_ref[0], preferred_element_type=F32)
        hid = (gate * jax.nn.sigmoid(gate) * up).astype(BF16)
        y = jnp.dot(hid, wd_ref[0], preferred_element_type=F32) * gate_sc[:blk, :1]
        for w in range(blk // SLOT_BLOCK):
            ye_ref[0, w] = y[w * SLOT_BLOCK:(w + 1) * SLOT_BLOCK, :].T.astype(ye_ref.dtype)
        stage_sc[:, :blk] = stage_sc[:, blk:]
        stage_sc[:, blk:] = jnp.zeros((stage_sc.shape[0], blk), F32)
        gate_sc[:blk, :] = gate_sc[blk:, :]
        gate_sc[blk:, :] = jnp.zeros((blk, LANES), F32)


def _ffn_call(tile_end, spos4, aff4, h2t, wg, wu, wd, cap, blk):
    d, n = h2t.shape
    n_t = n // blk
    ff = wg.shape[2]
    n_blk = cap // blk
    sub = blk // SLOT_BLOCK
    tt = blk

    def out_map(e, i, tend):
        prev_end = jnp.where(i > 0, tend[e, jnp.maximum(i - 1, 0)], 0)
        return (e, jnp.clip(prev_end // blk, 0, n_blk - 1), 0, 0)

    grid_spec = pltpu.PrefetchScalarGridSpec(
        num_scalar_prefetch=1,
        grid=(N_EXPERTS, n_t),
        in_specs=[
            pl.BlockSpec((1, 1, 1, tt), lambda e, i, tend: (e, i, 0, 0)),
            pl.BlockSpec((1, 1, 1, tt), lambda e, i, tend: (e, i, 0, 0)),
            pl.BlockSpec((d, tt), lambda e, i, tend: (0, i)),
            pl.BlockSpec((1, d, ff), lambda e, i, tend: (e, 0, 0)),
            pl.BlockSpec((1, d, ff), lambda e, i, tend: (e, 0, 0)),
            pl.BlockSpec((1, ff, d), lambda e, i, tend: (e, 0, 0)),
        ],
        out_specs=pl.BlockSpec((1, sub, d, SLOT_BLOCK), out_map),
        scratch_shapes=[
            pltpu.VMEM((d, 2 * blk), F32),
            pltpu.VMEM((2 * blk, LANES), F32),
        ],
    )
    return pl.pallas_call(
        functools.partial(_ffn_kernel, blk=blk),
        grid_spec=grid_spec,
        out_shape=jax.ShapeDtypeStruct((N_EXPERTS, n_blk * sub, d, SLOT_BLOCK), BF16),
        compiler_params=pltpu.CompilerParams(
            dimension_semantics=("arbitrary", "arbitrary"), vmem_limit_bytes=VMEM_LIMIT),
        name="ffn",
    )(tile_end, spos4, aff4, h2t, wg, wu, wd)


def _slot_span(tend, e, i):
    lo = jnp.where(i > 0, tend[e, jnp.maximum(i - 1, 0)], 0)
    return lo, tend[e, i]


def _combine_kernel(tend_ref, table_ref, spos_ref, x1_ref, mod_ref, g_ref, b_ref, *refs):
    first = refs[:N_EXPERTS]
    second = refs[N_EXPERTS:2 * N_EXPERTS]
    o_ref, acc_sc = refs[2 * N_EXPERTS:]
    i = pl.program_id(0)
    tt = x1_ref.shape[0]
    slot = lax.broadcasted_iota(I32, (SLOT_BLOCK, tt), 0)

    def onehot(e, block_offset):
        lo, _ = _slot_span(tend_ref, e, i)
        rel = spos_ref[e, 0] - (lo // SLOT_BLOCK + block_offset) * SLOT_BLOCK
        return jnp.where(slot == rel, 1.0, 0.0).astype(BF16)

    acc_sc[...] = jnp.dot(
        jnp.concatenate([first[e][0, 0] for e in range(N_EXPERTS)], axis=1),
        jnp.concatenate([onehot(e, 0) for e in range(N_EXPERTS)], axis=0),
        preferred_element_type=F32)

    for e in range(N_EXPERTS):
        lo, hi = _slot_span(tend_ref, e, i)

        @pl.when(hi > (lo // SLOT_BLOCK + 1) * SLOT_BLOCK)
        def _(e=e):
            acc_sc[...] += jnp.dot(second[e][0, 0], onehot(e, 1), preferred_element_type=F32)

    r = ALPHA * x1_ref[...] + mod_ref[0, 5:6, :] * acc_sc[...].T
    o_ref[...] = _layer_norm(r) * g_ref[...] + b_ref[...]


def _combine_call(tile_end, spos4, ye_t, x1, mod, ln_g, ln_b, tt, tokens_per_batch):
    n, d = x1.shape
    n_t = n // tt
    n_blk = ye_t.shape[1]
    tiles_per_batch = tokens_per_batch // tt

    lo = jnp.concatenate([jnp.zeros((N_EXPERTS, 1), I32), tile_end[:, :-1]], axis=1)
    nxt = lo // SLOT_BLOCK + 1
    spill = (tile_end > nxt * SLOT_BLOCK) & (nxt < n_blk)
    expert = jnp.arange(N_EXPERTS, dtype=I32)[:, None]
    table = jnp.concatenate([jnp.clip(lo // SLOT_BLOCK, 0, n_blk - 1),
                             jnp.where(spill, expert, 0),
                             jnp.where(spill, jnp.clip(nxt, 0, n_blk - 1), 0)], axis=0).astype(I32)

    def first_spec(e):
        return pl.BlockSpec((1, 1, d, SLOT_BLOCK), lambda i, tend, tab: (e, tab[e, i], 0, 0))

    def second_spec(e):
        return pl.BlockSpec((1, 1, d, SLOT_BLOCK),
                            lambda i, tend, tab: (tab[N_EXPERTS + e, i], tab[2 * N_EXPERTS + e, i], 0, 0))

    grid_spec = pltpu.PrefetchScalarGridSpec(
        num_scalar_prefetch=2,
        grid=(n_t,),
        in_specs=[
            pl.BlockSpec((N_EXPERTS, 1, 1, tt), lambda i, *_: (0, i, 0, 0)),
            pl.BlockSpec((tt, d), lambda i, *_: (i, 0)),
            pl.BlockSpec((1, 6, d), lambda i, *_: (i // tiles_per_batch, 0, 0)),
            pl.BlockSpec((1, d), lambda i, *_: (0, 0)),
            pl.BlockSpec((1, d), lambda i, *_: (0, 0)),
            *[first_spec(e) for e in range(N_EXPERTS)],
            *[second_spec(e) for e in range(N_EXPERTS)],
        ],
        out_specs=pl.BlockSpec((tt, d), lambda i, *_: (i, 0)),
        scratch_shapes=[pltpu.VMEM((d, tt), F32)],
    )
    return pl.pallas_call(
        _combine_kernel,
        grid_spec=grid_spec,
        out_shape=jax.ShapeDtypeStruct((n, d), F32),
        compiler_params=pltpu.CompilerParams(
            dimension_semantics=("parallel",), vmem_limit_bytes=VMEM_LIMIT),
        name="combine",
    )(tile_end, table, spos4, x1, mod, ln_g, ln_b, *([ye_t] * (2 * N_EXPERTS)))


def _rope_tables(s):
    pos = jnp.arange(s)
    half = HEAD_DIM // 2
    inv_freq = ROPE_THETA ** (-jnp.arange(0, half, 2, dtype=F32) / half)
    ang_r = (pos // GRID_W).astype(F32)[:, None] * inv_freq[None, :]
    ang_c = (pos % GRID_W).astype(F32)[:, None] * inv_freq[None, :]
    zero = jnp.zeros_like(ang_r)
    cos = jnp.concatenate([jnp.cos(ang_r)] * 2 + [jnp.cos(ang_c)] * 2, axis=1)
    slo = jnp.concatenate([-jnp.sin(ang_r), zero, -jnp.sin(ang_c), zero], axis=1)
    shi = jnp.concatenate([zero, jnp.sin(ang_r), zero, jnp.sin(ang_c)], axis=1)
    rep = LANES // HEAD_DIM
    return tuple(jnp.concatenate([t] * rep, axis=1) for t in (cos, slo, shi))


def _bf16_terms(value, n=3):
    terms, rem = [], np.float32(value)
    for _ in range(n):
        part = np.float32(np.asarray(rem, dtype=jnp.bfloat16))
        terms.append(float(part))
        rem = np.float32(rem - part)
    return terms


def _alibi_constants(t):
    cq = np.zeros((N_HEADS_A, LANES), np.float32)
    ck = np.zeros((N_HEADS_A, LANES), np.float32)
    d = jnp.arange(t)
    dist = jnp.abs(d[:, None] - d[None, :]).astype(F32)
    db, slopes_c = [], []
    for h in range(N_HEADS_A):
        terms = _bf16_terms(2.0 ** (-8.0 * (h + 1) / N_HEADS_A) * LOG2E)
        cq[h, 0:6] = terms + terms
        ck[h, 6:12] = terms + terms
        slopes_c.append(np.float32(sum(terms)))
        db.append(-slopes_c[-1] * dist)
    return jnp.asarray(cq), jnp.asarray(ck), jnp.stack(db), slopes_c


def _tile(dim, target):
    t = min(dim, target)
    assert dim % t == 0, (dim, t)
    return t


def _trunk(x, mod, lam, consts, w):
    b, s, d = x.shape
    n = b * s
    cap = CAPACITY_FACTOR * n // N_EXPERTS
    assert cap % SLOT_BLOCK == 0 and n % LANES == 0
    tm = _tile(s, 512)
    tt = _tile(n, SLOT_BLOCK)

    ta = _tile(s, 512)
    cq, ck, dbias, slopes_c = _alibi_constants(ta)
    assert tm == ta
    qa, ka, va, qb, kb, vb, norms = _proj_call(
        x, mod, w["w_in"], w["gq"], w["gk"], *consts["rope"], consts["bd"], cq, ck, tm)
    n_sub = min(ATTN_SUBTILES, s // ta)
    assert (s // ta) % n_sub == 0, (s, ta, n_sub)
    order = _kv_visit_order(s // ta)
    skip = _skip_flags(norms, order, slopes_c, ta, n_sub)
    oa = _diff_attn_call(qa, ka, va, dbias, lam, w["subln_g"], jnp.asarray(order), skip, ta, n_sub)
    ob = _gqa_call(qb, kb, vb, _tile(s, 256), ta, n_sub)
    x1, h2t, aff_t = _post_call(oa, ob, x, mod, w["w_out"], w["ln1_g"], w["ln1_b"],
                                w["w_router_t"], tm)

    spos, cnt = _route_call(aff_t.reshape(N_EXPERTS, n // LANES, LANES), cap)
    cnt = cnt.reshape(N_EXPERTS, n)

    def tiles(a, width):
        return a.reshape(N_EXPERTS, n // width, 1, width)

    def tile_ends(width):
        return cnt[:, width - 1::width]

    blk = math.gcd(FFN_BLOCK, cap)
    ye_t = _ffn_call(tile_ends(blk), tiles(spos, blk), tiles(aff_t, blk), h2t,
                     w["w_gate"], w["w_up"], w["w_down"], cap, blk)
    out = _combine_call(tile_ends(tt), tiles(spos, tt), ye_t, x1.reshape(n, d), mod,
                        w["ln2_g"], w["ln2_b"], tt, s)
    return out.reshape(b, s, d)


def kernel(x_prompt, x_sample, c_prompt, c_sample, w_ada, b_ada, w_in, lam_q1, lam_k1, lam_q2,
           lam_k2, subln_g, q_norm_g, k_norm_g, w_out, ln1_g, ln1_b, w_router, w_gate, w_up,
           w_down, ln2_g, ln2_b):
    l = 0
    d = x_prompt.shape[-1]
    c_all = jnp.concatenate([c_prompt, c_sample], axis=0)
    mod, lam = _mod_call(c_all, w_ada[l], b_ada[l][None, :], lam_q1[l][None, :], lam_k1[l][None, :],
                         lam_q2[l][None, :], lam_k2[l][None, :])
    mod = mod.reshape(c_all.shape[0], 6, d)

    w = {
        "w_in": w_in[l].astype(BF16),
        "gq": jnp.tile(q_norm_g[l], N_HEADS_B)[None, :],
        "gk": jnp.tile(k_norm_g[l], N_KV_B)[None, :],
        "subln_g": subln_g[l][None, :],
        "w_out": w_out[l].astype(BF16),
        "ln1_g": ln1_g[l][None, :], "ln1_b": ln1_b[l][None, :],
        "w_router_t": w_router[l].T,
        "w_gate": w_gate[l].astype(BF16), "w_up": w_up[l].astype(BF16),
        "w_down": w_down[l].astype(BF16),
        "ln2_g": ln2_g[l][None, :], "ln2_b": ln2_b[l][None, :],
    }
    head_of = jnp.arange(WIDTH_B) // HEAD_DIM
    shared = {"bd": (head_of[:, None] == head_of[None, :]).astype(BF16)}

    outs = []
    nb = x_prompt.shape[0]
    for x, m in ((x_prompt, mod[:nb]), (x_sample, mod[nb:])):
        consts = dict(shared, rope=_rope_tables(x.shape[1]))
        outs.append(_trunk(x, m, lam, consts, w))
    return tuple(outs)
```
